```python
import math
import jax, jax.numpy as jnp
from jax import lax
import numpy as np

D_MODEL = 1024
BATCH = 16
SEQ = 2048
DEPTH = 2
DEC_BATCH = 32
DEC_SEQ = 64
PAST_LEN = 2048

CHUNK = 64
N_A = DEPTH // 2
N_B = DEPTH - N_A
EPS = 1e-6

SSM_EXPAND = 2
D_INNER = SSM_EXPAND * D_MODEL
SSM_HEADDIM = 64
SSM_HEADS = D_INNER // SSM_HEADDIM
SSM_GROUPS = 4
SSM_HPG = SSM_HEADS // SSM_GROUPS
D_STATE = 128
SSM_CONV = 4
CONV_DIM = D_INNER + 2 * SSM_GROUPS * D_STATE
IN_PROJ_DIM = D_INNER + CONV_DIM + SSM_HEADS

MLA_HEADS = 16
Q_RANK = 384
KV_RANK = 256
QK_NOPE = 128
QK_ROPE = 64
V_DIM = 128
ROPE_THETA = 10000.0
Q_BLOCK = 128

D_FF = 2816
FFN_CONV = 3

kernel_name = 'streaming_ssd_mla_yoco_step'

F32 = jnp.float32


def rmsnorm(x, g):
    xf = x.astype(F32)
    y = xf * lax.rsqrt(jnp.mean(xf * xf, axis=-1, keepdims=True) + EPS)
    return (y * g.astype(F32)).astype(x.dtype)


def causal_dwconv(u, buf, w, b):
    k = w.shape[0]
    t = u.shape[1]
    ext = jnp.concatenate([buf.astype(u.dtype), u], axis=1)
    out = ext[:, 0:t] * w[0]
    for i in range(1, k):
        out = out + ext[:, i:i + t] * w[i]
    return out + b, ext[:, t:]


def rope(x, pos):
    half = QK_ROPE // 2
    inv = jnp.exp(-math.log(ROPE_THETA) * jnp.arange(half, dtype=F32) / half)
    ang = pos.astype(F32)[:, None] * inv[None, :]
    shp = (pos.shape[0],) + (1,) * (x.ndim - 3) + (half,)
    cos = jnp.cos(ang).reshape(shp)
    sin = jnp.sin(ang).reshape(shp)
    xf = x.astype(F32)
    x1, x2 = xf[..., :half], xf[..., half:]
    return jnp.concatenate([x1 * cos - x2 * sin, x2 * cos + x1 * sin], axis=-1).astype(x.dtype)


def ssd_scan(x, dt, a, bmat, cmat, h0):
    bsz, t = x.shape[0], x.shape[1]
    l = min(CHUNK, t)
    nc = t // l
    xc = x.astype(F32).reshape(bsz, nc, l, SSM_GROUPS, SSM_HPG, SSM_HEADDIM)
    dtc = dt.astype(F32).reshape(bsz, nc, l, SSM_GROUPS, SSM_HPG)
    bc = bmat.astype(F32).reshape(bsz, nc, l, SSM_GROUPS, D_STATE)
    cc = cmat.astype(F32).reshape(bsz, nc, l, SSM_GROUPS, D_STATE)
    da = dtc * a.astype(F32).reshape(SSM_GROUPS, SSM_HPG)
    acs = jnp.cumsum(da, axis=2)
    seg = acs[:, :, :, None] - acs[:, :, None]
    causal = jnp.tril(jnp.ones((l, l), dtype=bool))[:, :, None, None]
    decay = jnp.exp(jnp.where(causal, seg, -jnp.inf))
    cb = jnp.einsum('bclgn,bcsgn->bclsg', cc, bc)
    wts = cb[..., None] * decay * dtc[:, :, None]
    y_diag = jnp.einsum('bclsge,bcsgep->bclgep', wts, xc)
    decay_end = jnp.exp(acs[:, :, -1:] - acs)
    states = jnp.einsum('bclgn,bclge,bclgep->bcgepn', bc, decay_end * dtc, xc)
    chunk_decay = jnp.exp(acs[:, :, -1])

    def step(h, inp):
        s_c, d_c = inp
        return h * d_c[..., None, None] + s_c, h

    h_init = h0.astype(F32).reshape(bsz, SSM_GROUPS, SSM_HPG, SSM_HEADDIM, D_STATE)
    h_last, h_prev = lax.scan(step, h_init, (jnp.moveaxis(states, 1, 0), jnp.moveaxis(chunk_decay, 1, 0)))
    h_prev = jnp.moveaxis(h_prev, 0, 1)
    y_off = jnp.einsum('bclgn,bcgepn,bclge->bclgep', cc, h_prev, jnp.exp(acs))
    y = (y_diag + y_off).reshape(bsz, t, SSM_HEADS, SSM_HEADDIM)
    return y.astype(x.dtype), h_last.reshape(bsz, SSM_HEADS, SSM_HEADDIM, D_STATE).astype(h0.dtype)


def mamba2_mixer(h, conv_buf, ssm_state, w_in, conv_w, conv_b, dt_bias, a_log, d_skip, g_norm, w_out):
    bsz, t, _ = h.shape
    zxbcdt = h @ w_in
    z, xbc, dt = jnp.split(zxbcdt, [D_INNER, D_INNER + CONV_DIM], axis=-1)
    xbc_c, new_buf = causal_dwconv(xbc, conv_buf, conv_w, conv_b)
    xbc_c = jax.nn.silu(xbc_c)
    xs, bm, cm = jnp.split(xbc_c, [D_INNER, D_INNER + SSM_GROUPS * D_STATE], axis=-1)
    xs = xs.reshape(bsz, t, SSM_HEADS, SSM_HEADDIM)
    bm = bm.reshape(bsz, t, SSM_GROUPS, D_STATE)
    cm = cm.reshape(bsz, t, SSM_GROUPS, D_STATE)
    dtp = jax.nn.softplus(dt.astype(F32) + dt_bias.astype(F32))
    a = -jnp.exp(a_log.astype(F32))
    y, new_state = ssd_scan(xs, dtp, a, bm, cm, ssm_state)
    y = y + xs * d_skip[:, None]
    yg = (y.reshape(bsz, t, D_INNER) * jax.nn.silu(z)).reshape(bsz, t, SSM_GROUPS, D_INNER // SSM_GROUPS)
    yg = rmsnorm(yg, g_norm.reshape(SSM_GROUPS, D_INNER // SSM_GROUPS)).reshape(bsz, t, D_INNER)
    return yg @ w_out, new_buf, new_state


def conv_ffn(h, buf, w_up, conv_w, conv_b, w_down):
    u = h @ w_up
    u, new_buf = causal_dwconv(u, buf, conv_w, conv_b)
    val, gate = jnp.split(u, 2, axis=-1)
    return (jax.nn.gelu(gate, approximate=True) * val) @ w_down, new_buf


def mla_shared_kv(x, pos, g_in, w_dkv, g_kv, w_kr):
    h = rmsnorm(x, g_in)
    c_kv = rmsnorm(h @ w_dkv, g_kv)
    k_pe = rope(h @ w_kr, pos)
    return c_kv, k_pe


def mla_attend(h, pos_q, c_kv, k_pe, pos_k, w_dq, g_q, w_uq, w_uk, w_uv, w_o):
    bsz, t, _ = h.shape
    c_q = rmsnorm(h @ w_dq, g_q)
    q = (c_q @ w_uq).reshape(bsz, t, MLA_HEADS, QK_NOPE + QK_ROPE)
    q_nope, q_pe = q[..., :QK_NOPE], q[..., QK_NOPE:]
    q_pe = rope(q_pe, pos_q)
    q_lat = jnp.einsum('bthd,rhd->bthr', q_nope, w_uk)
    scale = (QK_NOPE + QK_ROPE) ** -0.5
    blk = min(Q_BLOCK, t)
    nb = t // blk
    ql_b = q_lat.reshape(bsz, nb, blk, MLA_HEADS, KV_RANK).swapaxes(0, 1)
    qp_b = q_pe.reshape(bsz, nb, blk, MLA_HEADS, QK_ROPE).swapaxes(0, 1)
    pq_b = pos_q.reshape(nb, blk)
    k_chunk = pos_k // CHUNK

    def attend_block(args):
        ql, qp, pq = args
        s = jnp.einsum('bqhr,bkr->bhqk', ql, c_kv).astype(F32) + jnp.einsum('bqhd,bkd->bhqk', qp, k_pe).astype(F32)
        s = s * scale
        mask = (pq // CHUNK)[:, None] >= k_chunk[None, :]
        s = jnp.where(mask[None, None], s, -jnp.inf)
        p = jax.nn.softmax(s, axis=-1).astype(c_kv.dtype)
        return jnp.einsum('bhqk,bkr->bqhr', p, c_kv)

    o_lat = lax.map(attend_block, (ql_b, qp_b, pq_b))
    o_lat = o_lat.swapaxes(0, 1).reshape(bsz, t, MLA_HEADS, KV_RANK)
    o = jnp.einsum('bthr,rhv->bthv', o_lat, w_uv).reshape(bsz, t, MLA_HEADS * V_DIM)
    return o @ w_o


def trunk(x, ssm_conv_buf, ssm_state, ffn_buf, past_lat, past_kpe, weights):
    (norm_mix_pre, norm_mix_post, norm_ffn_pre, norm_ffn_post,
     ssm_w_in, ssm_conv_w, ssm_conv_b, ssm_dt_bias, ssm_a_log, ssm_d, ssm_norm, ssm_w_out,
     kv_norm_in, kv_w_dkv, kv_norm, kv_w_kr, kv_w_uk, kv_w_uv,
     mla_w_dq, mla_q_norm, mla_w_uq, mla_w_o,
     ffn_w_up, ffn_conv_w, ffn_conv_b, ffn_w_down) = weights
    t = x.shape[1]
    past = past_lat.shape[1]
    pos_q = past + jnp.arange(t, dtype=jnp.int32)
    pos_k = jnp.arange(past + t, dtype=jnp.int32)
    new_ssm_conv, new_ssm_state, new_ffn = [], [], []
    c_kv = k_pe = new_lat = new_kpe = None
    for layer in range(DEPTH):
        h = rmsnorm(x, norm_mix_pre[layer])
        if layer < N_A:
            mix, cbuf, st = mamba2_mixer(h, ssm_conv_buf[layer], ssm_state[layer], ssm_w_in[layer], ssm_conv_w[layer],
                                         ssm_conv_b[layer], ssm_dt_bias[layer], ssm_a_log[layer], ssm_d[layer],
                                         ssm_norm[layer], ssm_w_out[layer])
            new_ssm_conv.append(cbuf)
            new_ssm_state.append(st)
        else:
            j = layer - N_A
            mix = mla_attend(h, pos_q, c_kv, k_pe, pos_k, mla_w_dq[j], mla_q_norm[j], mla_w_uq[j], kv_w_uk, kv_w_uv, mla_w_o[j])
        x = x + rmsnorm(mix, norm_mix_post[layer])
        f, fbuf = conv_ffn(rmsnorm(x, norm_ffn_pre[layer]), ffn_buf[layer], ffn_w_up[layer], ffn_conv_w[layer],
                           ffn_conv_b[layer], ffn_w_down[layer])
        x = x + rmsnorm(f, norm_ffn_post[layer])
        new_ffn.append(fbuf)
        if layer == N_A - 1:
            new_lat, new_kpe = mla_shared_kv(x, pos_q, kv_norm_in, kv_w_dkv, kv_norm, kv_w_kr)
            c_kv = jnp.concatenate([past_lat.astype(new_lat.dtype), new_lat], axis=1)
            k_pe = jnp.concatenate([past_kpe.astype(new_kpe.dtype), new_kpe], axis=1)
    return (x, jnp.stack(new_ssm_state), jnp.stack(new_ssm_conv), jnp.stack(new_ffn), new_lat, new_kpe)


def setup_inputs(seed: int = 0) -> dict:
    key = jax.random.key(seed)
    ks = iter(jax.random.split(key, 48))

    def nrm(shape, scale):
        return jax.random.normal(next(ks), shape, F32) * scale

    def gain(shape):
        return 1.0 + nrm(shape, 0.02)

    dt0 = jnp.exp(jax.random.uniform(next(ks), (N_A, SSM_HEADS), F32, math.log(1e-3), math.log(1e-1)))
    return {
        'x_prompt': nrm((BATCH, SEQ, D_MODEL), 1.0),
        'x_sample': nrm((DEC_BATCH, DEC_SEQ, D_MODEL), 1.0),
        'state_ssm': nrm((N_A, DEC_BATCH, SSM_HEADS, SSM_HEADDIM, D_STATE), 0.1),
        'state_ssm_conv': nrm((N_A, DEC_BATCH, SSM_CONV - 1, CONV_DIM), 1.0),
        'state_ffn_conv': nrm((DEPTH, DEC_BATCH, FFN_CONV - 1, 2 * D_FF), 1.0),
        'cache_kv_latent': nrm((DEC_BATCH, PAST_LEN, KV_RANK), 1.0),
        'cache_k_rope': nrm((DEC_BATCH, PAST_LEN, QK_ROPE), 1.0),
        'norm_mix_pre': gain((DEPTH, D_MODEL)),
        'norm_mix_post': gain((DEPTH, D_MODEL)),
        'norm_ffn_pre': gain((DEPTH, D_MODEL)),
        'norm_ffn_post': gain((DEPTH, D_MODEL)),
        'ssm_w_in': nrm((N_A, D_MODEL, IN_PROJ_DIM), D_MODEL ** -0.5),
        'ssm_conv_w': nrm((N_A, SSM_CONV, CONV_DIM), SSM_CONV ** -0.5),
        'ssm_conv_b': nrm((N_A, CONV_DIM), 0.02),
        'ssm_dt_bias': dt0 + jnp.log(-jnp.expm1(-dt0)),
        'ssm_a_log': jnp.log(jax.random.uniform(next(ks), (N_A, SSM_HEADS), F32, 1.0, 16.0)),
        'ssm_d': gain((N_A, SSM_HEADS)),
        'ssm_norm': gain((N_A, D_INNER)),
        'ssm_w_out': nrm((N_A, D_INNER, D_MODEL), D_INNER ** -0.5),
        'kv_norm_in': gain((D_MODEL,)),
        'kv_w_dkv': nrm((D_MODEL, KV_RANK), D_MODEL ** -0.5),
        'kv_norm': gain((KV_RANK,)),
        'kv_w_kr': nrm((D_MODEL, QK_ROPE), D_MODEL ** -0.5),
        'kv_w_uk': nrm((KV_RANK, MLA_HEADS, QK_NOPE), QK_NOPE ** -0.5),
        'kv_w_uv': nrm((KV_RANK, MLA_HEADS, V_DIM), KV_RANK ** -0.5),
        'mla_w_dq': nrm((N_B, D_MODEL, Q_RANK), D_MODEL ** -0.5),
        'mla_q_norm': gain((N_B, Q_RANK)),
        'mla_w_uq': nrm((N_B, Q_RANK, MLA_HEADS * (QK_NOPE + QK_ROPE)), Q_RANK ** -0.5),
        'mla_w_o': nrm((N_B, MLA_HEADS * V_DIM, D_MODEL), (MLA_HEADS * V_DIM) ** -0.5),
        'ffn_w_up': nrm((DEPTH, D_MODEL, 2 * D_FF), D_MODEL ** -0.5),
        'ffn_conv_w': nrm((DEPTH, FFN_CONV, 2 * D_FF), FFN_CONV ** -0.5),
        'ffn_conv_b': nrm((DEPTH, 2 * D_FF), 0.02),
        'ffn_w_down': nrm((DEPTH, D_FF, D_MODEL), D_FF ** -0.5),
    }


def reference(x_prompt, x_sample, state_ssm, state_ssm_conv, state_ffn_conv, cache_kv_latent, cache_k_rope,
              norm_mix_pre, norm_mix_post, norm_ffn_pre, norm_ffn_post,
              ssm_w_in, ssm_conv_w, ssm_conv_b, ssm_dt_bias, ssm_a_log, ssm_d, ssm_norm, ssm_w_out,
              kv_norm_in, kv_w_dkv, kv_norm, kv_w_kr, kv_w_uk, kv_w_uv,
              mla_w_dq, mla_q_norm, mla_w_uq, mla_w_o,
              ffn_w_up, ffn_conv_w, ffn_conv_b, ffn_w_down):
    weights = (norm_mix_pre, norm_mix_post, norm_ffn_pre, norm_ffn_post,
               ssm_w_in, ssm_conv_w, ssm_conv_b, ssm_dt_bias, ssm_a_log, ssm_d, ssm_norm, ssm_w_out,
               kv_norm_in, kv_w_dkv, kv_norm, kv_w_kr, kv_w_uk, kv_w_uv,
               mla_w_dq, mla_q_norm, mla_w_uq, mla_w_o,
               ffn_w_up, ffn_conv_w, ffn_conv_b, ffn_w_down)
    bp = x_prompt.shape[0]
    dtp = x_prompt.dtype
    y_prompt, p_ssm, p_ssm_conv, p_ffn_conv, p_lat, p_kpe = trunk(
        x_prompt,
        jnp.zeros((N_A, bp, SSM_CONV - 1, CONV_DIM), dtp),
        jnp.zeros((N_A, bp, SSM_HEADS, SSM_HEADDIM, D_STATE), dtp),
        jnp.zeros((DEPTH, bp, FFN_CONV - 1, 2 * D_FF), dtp),
        jnp.zeros((bp, 0, KV_RANK), dtp),
        jnp.zeros((bp, 0, QK_ROPE), dtp),
        weights)
    y_sample, s_ssm, s_ssm_conv, s_ffn_conv, s_lat, s_kpe = trunk(
        x_sample, state_ssm_conv, state_ssm, state_ffn_conv, cache_kv_latent, cache_k_rope, weights)
    return (y_prompt, y_sample, p_ssm, p_ssm_conv, p_ffn_conv, p_lat, p_kpe, s_ssm, s_ssm_conv, s_ffn_conv, s_lat, s_kpe)
```

```python
import functools
import math

import jax
import jax.numpy as jnp
from jax import lax
from jax.experimental import pallas as pl
from jax.experimental.pallas import tpu as pltpu

F32 = jnp.float32
BF16 = jnp.bfloat16

D_MODEL = 1024
CHUNK = 64
EPS = 1e-6
D_INNER = 2048
SSM_HEADDIM = 64
SSM_HEADS = 32
SSM_GROUPS = 4
SSM_HPG = 8
D_STATE = 128
SSM_CONV = 4
GROUP_W = D_INNER // SSM_GROUPS
BC_W = SSM_GROUPS * D_STATE
CONV_DIM = D_INNER + 2 * BC_W
MLA_HEADS = 16
Q_RANK = 384
KV_RANK = 256
QK_NOPE = 128
QK_ROPE = 64
V_DIM = 128
QK_CAT = KV_RANK + QK_ROPE
ROPE_THETA = 10000.0
D_FF = 2816
FFN_CONV = 3
LANE = 128
CARRY_ROWS = 8
VMEM_LIMIT = 56 * 1024 * 1024

NT_DIMS = (((1,), (1,)), ((), ()))
TN_DIMS = (((0,), (0,)), ((), ()))


def _rms(x, g):
    return x * lax.rsqrt(jnp.mean(x * x, axis=-1, keepdims=True) + EPS) * g


def _silu(x):
    return x * (1.0 / (1.0 + jnp.exp(-x)))


def _softplus(x):
    return jnp.maximum(x, 0.0) + jnp.log(1.0 + jnp.exp(-jnp.abs(x)))


def _const_spec(shape):
    zeros = (0,) * len(shape)
    return pl.BlockSpec(shape, lambda *_: zeros, pipeline_mode=pl.Buffered(1))


def _params(*sem):
    return pltpu.CompilerParams(dimension_semantics=sem, vmem_limit_bytes=VMEM_LIMIT)


def _causal_conv(ext_ref, raw, w_ref, b_ref, taps, first_step, hist_ref):
    tm = raw.shape[0]
    hist = taps - 1

    @pl.when(first_step)
    def _():
        ext_ref[0:CARRY_ROWS, :] = jnp.zeros((CARRY_ROWS, raw.shape[1]), F32)
        ext_ref[CARRY_ROWS - hist:CARRY_ROWS, :] = hist_ref[...]

    ext_ref[CARRY_ROWS:CARRY_ROWS + tm, :] = raw
    acc = b_ref[...] + raw * w_ref[hist:hist + 1, :]
    for k in range(1, taps):
        acc = acc + ext_ref[CARRY_ROWS - k:CARRY_ROWS - k + tm, :] * w_ref[hist - k:hist - k + 1, :]
    tail = ext_ref[CARRY_ROWS + tm - hist:CARRY_ROWS + tm, :]
    ext_ref[0:CARRY_ROWS, :] = ext_ref[tm:tm + CARRY_ROWS, :]
    return acc, tail


def _mamba_in_kernel(x_ref, buf_ref, g_ref, wzx_ref, wdt_ref, wdtt_ref, cw_ref, cb_ref, dtb_ref, dtbt_ref,
                     z_ref, xbc_ref, dt_ref, dtt_ref, nbuf_ref, ext_ref):
    h = _rms(x_ref[...], g_ref[...]).astype(BF16)
    zx = jnp.dot(h, wzx_ref[...], preferred_element_type=F32)
    z_ref[...] = zx[:, :D_INNER].astype(BF16)
    conv, tail = _causal_conv(ext_ref, zx[:, D_INNER:], cw_ref, cb_ref, SSM_CONV, pl.program_id(1) == 0, buf_ref)
    xbc_ref[...] = _silu(conv).astype(BF16)
    nbuf_ref[...] = tail
    dt = jnp.dot(h, wdt_ref[...], preferred_element_type=F32)[:, :SSM_HEADS]
    dt_ref[...] = _softplus(dt + dtb_ref[...])
    dtt = lax.dot_general(wdtt_ref[...], h, NT_DIMS, preferred_element_type=F32)
    dtt_ref[...] = _softplus(dtt + dtbt_ref[...])


def _mamba_in(x, conv_buf, g, wzx, wdt, wdtt, cw, cb, dtb, dtbt, tm):
    b, t, _ = x.shape
    grid = (b, t // tm)
    row = lambda w: pl.BlockSpec((None, tm, w), lambda i, j: (i, j, 0))
    return pl.pallas_call(
        _mamba_in_kernel,
        grid=grid,
        in_specs=[
            row(D_MODEL),
            pl.BlockSpec((None, SSM_CONV - 1, CONV_DIM), lambda i, j: (i, 0, 0)),
            _const_spec((1, D_MODEL)),
            _const_spec((D_MODEL, D_INNER + CONV_DIM)),
            _const_spec((D_MODEL, LANE)),
            _const_spec((SSM_HEADS, D_MODEL)),
            _const_spec((SSM_CONV, CONV_DIM)),
            _const_spec((1, CONV_DIM)),
            _const_spec((1, SSM_HEADS)),
            _const_spec((SSM_HEADS, 1)),
        ],
        out_specs=[
            row(D_INNER),
            row(CONV_DIM),
            row(SSM_HEADS),
            pl.BlockSpec((None, SSM_HEADS, tm), lambda i, j: (i, 0, j)),
            pl.BlockSpec((None, SSM_CONV - 1, CONV_DIM), lambda i, j: (i, 0, 0)),
        ],
        out_shape=[
            jax.ShapeDtypeStruct((b, t, D_INNER), BF16),
            jax.ShapeDtypeStruct((b, t, CONV_DIM), BF16),
            jax.ShapeDtypeStruct((b, t, SSM_HEADS), F32),
            jax.ShapeDtypeStruct((b, SSM_HEADS, t), F32),
            jax.ShapeDtypeStruct((b, SSM_CONV - 1, CONV_DIM), F32),
        ],
        scratch_shapes=[pltpu.VMEM((CARRY_ROWS + tm, CONV_DIM), F32)],
        compiler_params=_params("parallel", "arbitrary"),
        name="mamba_in",
    )(x, conv_buf, g, wzx, wdt, wdtt, cw, cb, dtb, dtbt)


def _ssd_kernel(xs_ref, bm_ref, cm_ref, z_ref, dt_ref, dtt_ref, arow_ref, acol_ref, dskip_ref, gn_ref, h0_ref,
                yg_ref, hout_ref, s_ref, y_ref, *, nc):
    step = pl.program_id(1)

    @pl.when(step == 0)
    def _():
        s_ref[...] = h0_ref[...]

    li = lax.broadcasted_iota(jnp.int32, (CHUNK, CHUNK), 0)
    si = lax.broadcasted_iota(jnp.int32, (CHUNK, CHUNK), 1)
    causal = si <= li
    tril = causal.astype(F32)
    triu = (li <= si).astype(F32)

    def chunk(c, carry):
        r0 = pl.multiple_of(c * CHUNK, CHUNK)
        rows = pl.ds(r0, CHUNK)
        dt = dt_ref[rows, :]
        dtt = dtt_ref[c]
        acs = jnp.dot(tril, dt * arow_ref[...], precision=lax.Precision.HIGHEST,
                      preferred_element_type=F32)
        acst = jnp.dot(dtt * acol_ref[...], triu, precision=lax.Precision.HIGHEST,
                       preferred_element_type=F32)
        e_acs = jnp.exp(acs)
        last = acs[CHUNK - 1:CHUNK, :]
        wst = jnp.exp(last - acs) * dt
        e_last = jnp.exp(last)
        for g in range(SSM_GROUPS):
            bg = bm_ref[rows, g * D_STATE:(g + 1) * D_STATE]
            cg = cm_ref[rows, g * D_STATE:(g + 1) * D_STATE]
            cb = lax.dot_general(cg, bg, NT_DIMS, preferred_element_type=F32)
            for e in range(g * SSM_HPG, (g + 1) * SSM_HPG):
                xe = xs_ref[rows, e * SSM_HEADDIM:(e + 1) * SSM_HEADDIM]
                seg = acs[:, e:e + 1] - acst[e:e + 1, :]
                wts = cb * jnp.exp(jnp.where(causal, seg, -jnp.inf)) * dtt[e:e + 1, :]
                y = jnp.dot(wts.astype(BF16), xe, preferred_element_type=F32)
                se = s_ref[e]
                yo = lax.dot_general(cg, se.astype(BF16), NT_DIMS, preferred_element_type=F32)
                y = y + yo * e_acs[:, e:e + 1]
                y_ref[:, e * SSM_HEADDIM:(e + 1) * SSM_HEADDIM] = y
                xw = (xe.astype(F32) * wst[:, e:e + 1]).astype(BF16)
                upd = lax.dot_general(xw, bg, TN_DIMS, preferred_element_type=F32)
                s_ref[e] = se * e_last[:, e:e + 1] + upd
        xs = xs_ref[rows, :].astype(F32)
        y = y_ref[...] + xs * dskip_ref[...]
        yg = y * _silu(z_ref[rows, :].astype(F32))
        for g in range(SSM_GROUPS):
            sl = slice(g * GROUP_W, (g + 1) * GROUP_W)
            yg_ref[rows, sl] = _rms(yg[:, sl], gn_ref[:, sl]).astype(BF16)
        return carry

    lax.fori_loop(0, nc, chunk, 0)

    @pl.when(step == pl.num_programs(1) - 1)
    def _():
        hout_ref[...] = s_ref[...]


def _ssd(xbc, z, dt, dtt4, arow, acol, dskip, gn, h0, nc):
    b, t, _ = xbc.shape
    lt = nc * CHUNK
    grid = (b, t // lt)
    state_spec = pl.BlockSpec((None, SSM_HEADS, SSM_HEADDIM, D_STATE), lambda i, j: (i, 0, 0, 0))
    return pl.pallas_call(
        functools.partial(_ssd_kernel, nc=nc),
        grid=grid,
        in_specs=[
            pl.BlockSpec((None, lt, D_INNER), lambda i, j: (i, j, 0)),
            pl.BlockSpec((None, lt, BC_W), lambda i, j: (i, j, D_INNER // BC_W)),
            pl.BlockSpec((None, lt, BC_W), lambda i, j: (i, j, D_INNER // BC_W + 1)),
            pl.BlockSpec((None, lt, D_INNER), lambda i, j: (i, j, 0)),
            pl.BlockSpec((None, lt, SSM_HEADS), lambda i, j: (i, j, 0)),
            pl.BlockSpec((None, nc, SSM_HEADS, CHUNK), lambda i, j: (i, j, 0, 0)),
            _const_spec((1, SSM_HEADS)),
            _const_spec((SSM_HEADS, 1)),
            _const_spec((1, D_INNER)),
            _const_spec((1, D_INNER)),
            state_spec,
        ],
        out_specs=[
            pl.BlockSpec((None, lt, D_INNER), lambda i, j: (i, j, 0)),
            state_spec,
        ],
        out_shape=[
            jax.ShapeDtypeStruct((b, t, D_INNER), BF16),
            jax.ShapeDtypeStruct((b, SSM_HEADS, SSM_HEADDIM, D_STATE), F32),
        ],
        scratch_shapes=[
            pltpu.VMEM((SSM_HEADS, SSM_HEADDIM, D_STATE), F32),
            pltpu.VMEM((CHUNK, D_INNER), F32),
        ],
        compiler_params=_params("parallel", "arbitrary"),
        name="ssd_scan",
    )(xbc, xbc, xbc, z, dt, dtt4, arow, acol, dskip, gn, h0)


def _proj_res_kernel(a_ref, x_ref, w_ref, g_ref, o_ref):
    y = jnp.dot(a_ref[...], w_ref[...], preferred_element_type=F32)
    o_ref[...] = x_ref[...] + _rms(y, g_ref[...])


def _proj_res(a, x, w, g, tm):
    b, t, k = a.shape
    grid = (b, t // tm)
    return pl.pallas_call(
        _proj_res_kernel,
        grid=grid,
        in_specs=[
            pl.BlockSpec((None, tm, k), lambda i, j: (i, j, 0)),
            pl.BlockSpec((None, tm, D_MODEL), lambda i, j: (i, j, 0)),
            _const_spec((k, D_MODEL)),
            _const_spec((1, D_MODEL)),
        ],
        out_specs=pl.BlockSpec((None, tm, D_MODEL), lambda i, j: (i, j, 0)),
        out_shape=jax.ShapeDtypeStruct((b, t, D_MODEL), F32),
        compiler_params=_params("parallel", "parallel"),
        name="proj_res",
    )(a, x, w, g)


def _mla_out_kernel(o_ref, x_ref, wuv_ref, wo_ref, g_ref, out_ref):
    parts = []
    for h in range(MLA_HEADS):
        parts.append(jnp.dot(o_ref[h], wuv_ref[h], preferred_element_type=F32).astype(BF16))
    o = jnp.concatenate(parts, axis=-1)
    y = jnp.dot(o, wo_ref[...], preferred_element_type=F32)
    out_ref[...] = x_ref[...] + _rms(y, g_ref[...])


def _mla_out(o_lat, x, wuv, wo, g, tm):
    b, t, _ = x.shape
    grid = (b, t // tm)
    return pl.pallas_call(
        _mla_out_kernel,
        grid=grid,
        in_specs=[
            pl.BlockSpec((None, MLA_HEADS, tm, KV_RANK), lambda i, j: (i, 0, j, 0)),
            pl.BlockSpec((None, tm, D_MODEL), lambda i, j: (i, j, 0)),
            _const_spec((MLA_HEADS, KV_RANK, V_DIM)),
            _const_spec((MLA_HEADS * V_DIM, D_MODEL)),
            _const_spec((1, D_MODEL)),
        ],
        out_specs=pl.BlockSpec((None, tm, D_MODEL), lambda i, j: (i, j, 0)),
        out_shape=jax.ShapeDtypeStruct((b, t, D_MODEL), F32),
        compiler_params=_params("parallel", "parallel"),
        name="mla_out",
    )(o_lat, x, wuv, wo, g)


def _ffn_kernel(x_ref, buf_ref, gpre_ref, wup_ref, cw_ref, cb_ref, wdn_ref, gpost_ref,
                o_ref, nbuf_ref, ext_ref):
    x = x_ref[...]
    h = _rms(x, gpre_ref[...]).astype(BF16)
    u = jnp.dot(h, wup_ref[...], preferred_element_type=F32)
    conv, tail = _causal_conv(ext_ref, u, cw_ref, cb_ref, FFN_CONV, pl.program_id(1) == 0, buf_ref)
    nbuf_ref[...] = tail
    act = (jax.nn.gelu(conv[:, D_FF:], approximate=True) * conv[:, :D_FF]).astype(BF16)
    f = jnp.dot(act, wdn_ref[...], preferred_element_type=F32)
    o_ref[...] = x + _rms(f, gpost_ref[...])


def _ffn(x, buf, gpre, wup, cw, cb, wdn, gpost, tm):
    b, t, _ = x.shape
    grid = (b, t // tm)
    buf_spec = pl.BlockSpec((None, FFN_CONV - 1, 2 * D_FF), lambda i, j: (i, 0, 0))
    return pl.pallas_call(
        _ffn_kernel,
        grid=grid,
        in_specs=[
            pl.BlockSpec((None, tm, D_MODEL), lambda i, j: (i, j, 0)),
            buf_spec,
            _const_spec((1, D_MODEL)),
            _const_spec((D_MODEL, 2 * D_FF)),
            _const_spec((FFN_CONV, 2 * D_FF)),
            _const_spec((1, 2 * D_FF)),
            _const_spec((D_FF, D_MODEL)),
            _const_spec((1, D_MODEL)),
        ],
        out_specs=[pl.BlockSpec((None, tm, D_MODEL), lambda i, j: (i, j, 0)), buf_spec],
        out_shape=[
            jax.ShapeDtypeStruct((b, t, D_MODEL), F32),
            jax.ShapeDtypeStruct((b, FFN_CONV - 1, 2 * D_FF), F32),
        ],
        scratch_shapes=[pltpu.VMEM((CARRY_ROWS + tm, 2 * D_FF), F32)],
        compiler_params=_params("parallel", "arbitrary"),
        name="conv_ffn",
    )(x, buf, gpre, wup, cw, cb, wdn, gpost)


def _kv_kernel(x_ref, gin_ref, w_ref, gkv_ref, cos_ref, sin_ref, lat_ref, kpe_ref, kcat_ref):
    h = _rms(x_ref[...], gin_ref[...]).astype(BF16)
    hw = jnp.dot(h, w_ref[...], preferred_element_type=F32)
    lat = _rms(hw[:, :KV_RANK], gkv_ref[...])
    kpe = hw[:, KV_RANK:QK_CAT] * cos_ref[...] + hw[:, QK_CAT:] * sin_ref[...]
    lat_ref[...] = lat
    kpe_ref[...] = kpe
    kcat_ref[:, :KV_RANK] = lat.astype(BF16)
    kcat_ref[:, KV_RANK:] = kpe.astype(BF16)


def _kv_shared(x, gin, w, gkv, cos, sin, tm):
    b, t, _ = x.shape
    grid = (b, t // tm)
    row = lambda wd: pl.BlockSpec((None, tm, wd), lambda i, j: (i, j, 0))
    tab = pl.BlockSpec((tm, QK_ROPE), lambda i, j: (j, 0))
    return pl.pallas_call(
        _kv_kernel,
        grid=grid,
        in_specs=[
            row(D_MODEL),
            _const_spec((1, D_MODEL)),
            _const_spec((D_MODEL, KV_RANK + 2 * QK_ROPE)),
            _const_spec((1, KV_RANK)),
            tab,
            tab,
        ],
        out_specs=[row(KV_RANK), row(QK_ROPE), row(QK_CAT)],
        out_shape=[
            jax.ShapeDtypeStruct((b, t, KV_RANK), F32),
            jax.ShapeDtypeStruct((b, t, QK_ROPE), F32),
            jax.ShapeDtypeStruct((b, t, QK_CAT), BF16),
        ],
        compiler_params=_params("parallel", "parallel"),
        name="kv_shared",
    )(x, gin, w, gkv, cos, sin)


def _q_kernel(x_ref, gpre_ref, wdq_ref, gq_ref, wn_ref, wp_ref, wpr_ref, wuk_ref, cos_ref, sin_ref, q_ref, *, scale):
    h = _rms(x_ref[...], gpre_ref[...]).astype(BF16)
    cq = _rms(jnp.dot(h, wdq_ref[...], preferred_element_type=F32), gq_ref[...]).astype(BF16)
    qn = jnp.dot(cq, wn_ref[...], preferred_element_type=F32).astype(BF16)
    qp = jnp.dot(cq, wp_ref[...], preferred_element_type=F32)
    qpr = jnp.dot(cq, wpr_ref[...], preferred_element_type=F32)
    qpe = ((qp * cos_ref[...] + qpr * sin_ref[...]) * scale).astype(BF16)
    for hd in range(MLA_HEADS):
        ql = jnp.dot(qn[:, hd * QK_NOPE:(hd + 1) * QK_NOPE], wuk_ref[hd], preferred_element_type=F32)
        q_ref[hd, :, :KV_RANK] = (ql * scale).astype(BF16)
        q_ref[hd, :, KV_RANK:] = qpe[:, hd * QK_ROPE:(hd + 1) * QK_ROPE]


def _q_proj(x, gpre, wdq, gq, wn, wp, wpr, wuk, cos, sin, tm):
    b, t, _ = x.shape
    grid = (b, t // tm)
    tab = pl.BlockSpec((tm, MLA_HEADS * QK_ROPE), lambda i, j: (j, 0))
    scale = (QK_NOPE + QK_ROPE) ** -0.5
    return pl.pallas_call(
        functools.partial(_q_kernel, scale=scale),
        grid=grid,
        in_specs=[
            pl.BlockSpec((None, tm, D_MODEL), lambda i, j: (i, j, 0)),
            _const_spec((1, D_MODEL)),
            _const_spec((D_MODEL, Q_RANK)),
            _const_spec((1, Q_RANK)),
            _const_spec((Q_RANK, MLA_HEADS * QK_NOPE)),
            _const_spec((Q_RANK, MLA_HEADS * QK_ROPE)),
            _const_spec((Q_RANK, MLA_HEADS * QK_ROPE)),
            _const_spec((MLA_HEADS, QK_NOPE, KV_RANK)),
            tab,
            tab,
        ],
        out_specs=pl.BlockSpec((None, MLA_HEADS, tm, QK_CAT), lambda i, j: (i, 0, j, 0)),
        out_shape=jax.ShapeDtypeStruct((b, MLA_HEADS, t, QK_CAT), BF16),
        compiler_params=_params("parallel", "parallel"),
        name="q_proj",
    )(x, gpre, wdq, gq, wn, wp, wpr, wuk, cos, sin)


def _attn_kernel(q_ref, k_ref, o_ref, m_ref, l_ref, acc_ref, *, tq, tk, past, nkv):
    qi = pl.program_id(1)
    rows = MLA_HEADS * tq
    q = q_ref[...].reshape(rows, QK_CAT)
    m_ref[...] = jnp.full(m_ref.shape, -jnp.inf, F32)
    l_ref[...] = jnp.zeros(l_ref.shape, F32)
    acc_ref[...] = jnp.zeros(acc_ref.shape, F32)
    q_chunk = (past + qi * tq + lax.broadcasted_iota(jnp.int32, (tq, tk), 0)) // CHUNK
    k_iota = lax.broadcasted_iota(jnp.int32, (tq, tk), 1)
    n_vis = jnp.minimum(nkv, (past + (qi + 1) * tq + tk - 1) // tk)

    def body(j, carry):
        k0 = pl.multiple_of(j * tk, tk)
        ks = k_ref[pl.ds(k0, tk), :]
        s = lax.dot_general(q, ks, NT_DIMS, preferred_element_type=F32)
        mask = ((k0 + k_iota) // CHUNK) <= q_chunk
        s3 = jnp.where(mask[None], s.reshape(MLA_HEADS, tq, tk), -jnp.inf)
        m_prev = m_ref[...]
        m_new = jnp.maximum(m_prev, jnp.max(s3, axis=-1, keepdims=True))
        alpha = jnp.exp(m_prev - m_new)
        p = jnp.exp(s3 - m_new)
        l_ref[...] = alpha * l_ref[...] + jnp.sum(p, axis=-1, keepdims=True)
        pv = jnp.dot(p.astype(BF16).reshape(rows, tk), ks[:, :KV_RANK], preferred_element_type=F32)
        acc_ref[...] = alpha * acc_ref[...] + pv.reshape(MLA_HEADS, tq, KV_RANK)
        m_ref[...] = m_new
        return carry

    lax.fori_loop(0, n_vis, body, 0)
    o_ref[...] = (acc_ref[...] / l_ref[...]).astype(BF16)


def _attention(q, kcat, past, tq, tk):
    b, _, t, _ = q.shape
    s_len = kcat.shape[1]
    nkv = s_len // tk
    grid = (b, t // tq)
    return pl.pallas_call(
        functools.partial(_attn_kernel, tq=tq, tk=tk, past=past, nkv=nkv),
        grid=grid,
        in_specs=[
            pl.BlockSpec((None, MLA_HEADS, tq, QK_CAT), lambda i, j: (i, 0, j, 0)),
            pl.BlockSpec((None, s_len, QK_CAT), lambda i, j: (i, 0, 0)),
        ],
        out_specs=pl.BlockSpec((None, MLA_HEADS, tq, KV_RANK), lambda i, j: (i, 0, j, 0)),
        out_shape=jax.ShapeDtypeStruct((b, MLA_HEADS, t, KV_RANK), BF16),
        scratch_shapes=[
            pltpu.VMEM((MLA_HEADS, tq, 1), F32),
            pltpu.VMEM((MLA_HEADS, tq, 1), F32),
            pltpu.VMEM((MLA_HEADS, tq, KV_RANK), F32),
        ],
        compiler_params=_params("parallel", "parallel"),
        name="mla_attention",
    )(q, kcat)


def _rope_tables(past, t):
    half = QK_ROPE // 2
    inv = jnp.exp(-math.log(ROPE_THETA) * jnp.arange(half, dtype=F32) / half)
    ang = (past + jnp.arange(t, dtype=jnp.int32)).astype(F32)[:, None] * inv[None, :]
    cos = jnp.cos(ang)
    sin = jnp.sin(ang)
    return jnp.concatenate([cos, cos], axis=-1), jnp.concatenate([sin, sin], axis=-1)


def _rotate_half_cols(w):
    shp = w.shape
    w4 = w.reshape(shp[:-1] + (shp[-1] // QK_ROPE, 2, QK_ROPE // 2))
    return jnp.concatenate([-w4[..., 1:2, :], w4[..., 0:1, :]], axis=-2).reshape(shp)


def _tile(t, pref):
    return pref if t % pref == 0 else t


def _prep_weights(norm_mix_pre, norm_mix_post, norm_ffn_pre, norm_ffn_post,
                  ssm_w_in, ssm_conv_w, ssm_conv_b, ssm_dt_bias, ssm_a_log, ssm_d, ssm_norm, ssm_w_out,
                  kv_norm_in, kv_w_dkv, kv_norm, kv_w_kr, kv_w_uk, kv_w_uv,
                  mla_w_dq, mla_q_norm, mla_w_uq, mla_w_o,
                  ffn_w_up, ffn_conv_w, ffn_conv_b, ffn_w_down):
    w = {}
    w_in = ssm_w_in[0]
    w["wzx"] = w_in[:, :D_INNER + CONV_DIM].astype(BF16)
    w_dt = w_in[:, D_INNER + CONV_DIM:]
    w["wdt"] = jnp.pad(w_dt, ((0, 0), (0, LANE - SSM_HEADS))).astype(BF16)
    w["wdtt"] = w_dt.T.astype(BF16)
    w["ssm_cw"] = ssm_conv_w[0]
    w["ssm_cb"] = ssm_conv_b[0][None, :]
    w["dtb"] = ssm_dt_bias[0][None, :]
    w["dtbt"] = ssm_dt_bias[0][:, None]
    a = -jnp.exp(ssm_a_log[0].astype(F32))
    w["arow"] = a[None, :]
    w["acol"] = a[:, None]
    w["dskip"] = jnp.repeat(ssm_d[0], SSM_HEADDIM)[None, :]
    w["gn"] = ssm_norm[0][None, :]
    w["w_out"] = ssm_w_out[0].astype(BF16)
    w["g_mix_pre"] = norm_mix_pre[:, None, :]
    w["g_mix_post"] = norm_mix_post[:, None, :]
    w["g_ffn_pre"] = norm_ffn_pre[:, None, :]
    w["g_ffn_post"] = norm_ffn_post[:, None, :]
    w["ffn_up"] = ffn_w_up.astype(BF16)
    w["ffn_cw"] = ffn_conv_w
    w["ffn_cb"] = ffn_conv_b[:, None, :]
    w["ffn_down"] = ffn_w_down.astype(BF16)
    w["kv_gin"] = kv_norm_in[None, :]
    w["kv_w"] = jnp.concatenate([kv_w_dkv, kv_w_kr, _rotate_half_cols(kv_w_kr)], axis=-1).astype(BF16)
    w["kv_g"] = kv_norm[None, :]
    w["wdq"] = mla_w_dq[0].astype(BF16)
    w["gq"] = mla_q_norm[0][None, :]
    wuq = mla_w_uq[0].reshape(Q_RANK, MLA_HEADS, QK_NOPE + QK_ROPE)
    w["wq_nope"] = wuq[:, :, :QK_NOPE].reshape(Q_RANK, MLA_HEADS * QK_NOPE).astype(BF16)
    wq_pe = wuq[:, :, QK_NOPE:].reshape(Q_RANK, MLA_HEADS * QK_ROPE)
    w["wq_pe"] = wq_pe.astype(BF16)
    w["wq_pe_rot"] = _rotate_half_cols(wq_pe).astype(BF16)
    w["wuk_t"] = jnp.transpose(kv_w_uk, (1, 2, 0)).astype(BF16)
    w["wuv"] = jnp.transpose(kv_w_uv, (1, 0, 2)).astype(BF16)
    w["wo"] = mla_w_o[0].astype(BF16)
    return w


def _trunk(x, ssm_conv_buf, ssm_state, ffn_buf, past_kcat, w):
    b, t, _ = x.shape
    past = 0 if past_kcat is None else past_kcat.shape[1]
    assert t % CHUNK == 0 and past % CHUNK == 0
    tm = _tile(t, 256)

    z, xbc, dt, dtt, new_conv = _mamba_in(x, ssm_conv_buf, w["g_mix_pre"][0], w["wzx"], w["wdt"], w["wdtt"],
                                          w["ssm_cw"], w["ssm_cb"], w["dtb"], w["dtbt"], tm)
    nc = _tile(t, 256) // CHUNK
    dtt4 = dtt.reshape(b, SSM_HEADS, t // CHUNK, CHUNK).transpose(0, 2, 1, 3)
    yg, new_state = _ssd(xbc, z, dt, dtt4, w["arow"], w["acol"], w["dskip"], w["gn"], ssm_state, nc)
    x = _proj_res(yg, x, w["w_out"], w["g_mix_post"][0], tm)
    x, new_ffn0 = _ffn(x, ffn_buf[0], w["g_ffn_pre"][0], w["ffn_up"][0], w["ffn_cw"][0], w["ffn_cb"][0],
                       w["ffn_down"][0], w["g_ffn_post"][0], tm)

    cos, sin = _rope_tables(past, t)
    lat, kpe, kcat = _kv_shared(x, w["kv_gin"], w["kv_w"], w["kv_g"], cos, sin, tm)
    if past_kcat is not None:
        kcat = jnp.concatenate([past_kcat, kcat], axis=1)

    cos_h = jnp.tile(cos, (1, MLA_HEADS))
    sin_h = jnp.tile(sin, (1, MLA_HEADS))
    q = _q_proj(x, w["g_mix_pre"][1], w["wdq"], w["gq"], w["wq_nope"], w["wq_pe"], w["wq_pe_rot"], w["wuk_t"],
                cos_h, sin_h, tm)
    tq = _tile(t, 128)
    s_len = past + t
    if s_len % 512 == 0:
        tk = 512
    else:
        tk = -(-s_len // LANE) * LANE
        kcat = jnp.pad(kcat, ((0, 0), (0, tk - s_len), (0, 0)))
    o_lat = _attention(q, kcat, past, tq, tk)
    x = _mla_out(o_lat, x, w["wuv"], w["wo"], w["g_mix_post"][1], tm)
    x, new_ffn1 = _ffn(x, ffn_buf[1], w["g_ffn_pre"][1], w["ffn_up"][1], w["ffn_cw"][1], w["ffn_cb"][1],
                       w["ffn_down"][1], w["g_ffn_post"][1], tm)
    return (x, new_state[None], new_conv[None], jnp.stack([new_ffn0, new_ffn1]), lat, kpe)


def kernel(x_prompt, x_sample, state_ssm, state_ssm_conv, state_ffn_conv, cache_kv_latent, cache_k_rope, norm_mix_pre, norm_mix_post, norm_ffn_pre, norm_ffn_post, ssm_w_in, ssm_conv_w, ssm_conv_b, ssm_dt_bias, ssm_a_log, ssm_d, ssm_norm, ssm_w_out, kv_norm_in, kv_w_dkv, kv_norm, kv_w_kr, kv_w_uk, kv_w_uv, mla_w_dq, mla_q_norm, mla_w_uq, mla_w_o, ffn_w_up, ffn_conv_w, ffn_conv_b, ffn_w_down):
    w = _prep_weights(norm_mix_pre, norm_mix_post, norm_ffn_pre, norm_ffn_post,
                      ssm_w_in, ssm_conv_w, ssm_conv_b, ssm_dt_bias, ssm_a_log, ssm_d, ssm_norm, ssm_w_out,
                      kv_norm_in, kv_w_dkv, kv_norm, kv_w_kr, kv_w_uk, kv_w_uv,
                      mla_w_dq, mla_q_norm, mla_w_uq, mla_w_o,
                      ffn_w_up, ffn_conv_w, ffn_conv_b, ffn_w_down)
    bp = x_prompt.shape[0]
    dtp = x_prompt.dtype
    y_p, p_ssm, p_conv, p_ffn, p_lat, p_kpe = _trunk(
        x_prompt,
        jnp.zeros((bp, SSM_CONV - 1, CONV_DIM), dtp),
        jnp.zeros((bp, SSM_HEADS, SSM_HEADDIM, D_STATE), dtp),
        jnp.zeros((2, bp, FFN_CONV - 1, 2 * D_FF), dtp),
        None, w)
    past_kcat = jnp.concatenate([cache_kv_latent, cache_k_rope], axis=-1).astype(BF16)
    y_s, s_ssm, s_conv, s_ffn, s_lat, s_kpe = _trunk(
        x_sample, state_ssm_conv[0], state_ssm[0], state_ffn_conv, past_kcat, w)
    return (y_p, y_s, p_ssm, p_conv, p_ffn, p_lat, p_kpe, s_ssm, s_conv, s_ffn, s_lat, s_kpe)
```

```python
import functools
import math

import jax
import jax.numpy as jnp
from jax import lax
from jax.experimental import pallas as pl
from jax.experimental.pallas import tpu as pltpu

F32 = jnp.float32
BF16 = jnp.bfloat16

D_MODEL = 1024
CHUNK = 64
EPS = 1e-6
D_INNER = 2048
SSM_HEADDIM = 64
SSM_HEADS = 32
SSM_GROUPS = 4
SSM_HPG = 8
D_STATE = 128
SSM_CONV = 4
GROUP_W = D_INNER // SSM_GROUPS
BC_W = SSM_GROUPS * D_STATE
CONV_DIM = D_INNER + 2 * BC_W
MLA_HEADS = 16
Q_RANK = 384
KV_RANK = 256
QK_NOPE = 128
QK_ROPE = 64
V_DIM = 128
QK_CAT = KV_RANK + QK_ROPE
ROPE_THETA = 10000.0
D_FF = 2816
FFN_CONV = 3
LANE = 128
CARRY_ROWS = 8
COL_CHUNK = 512
FFN_CHUNK = 256
ATT_ROWS = 128
ATT_TK = 256
LOG2E = 1.4426950408889634
VMEM_LIMIT = 56 * 1024 * 1024

NT_DIMS = (((1,), (1,)), ((), ()))
TN_DIMS = (((0,), (0,)), ((), ()))


def _rms(x, g):
    return x * lax.rsqrt(jnp.mean(x * x, axis=-1, keepdims=True) + EPS) * g


def _silu(x):
    return x * (1.0 / (1.0 + jnp.exp(-x)))


def _softplus(x):
    return jnp.maximum(x, 0.0) + jnp.log(1.0 + jnp.exp(-jnp.abs(x)))


def _const_spec(shape):
    zeros = (0,) * len(shape)
    return pl.BlockSpec(shape, lambda *_: zeros, pipeline_mode=pl.Buffered(1))


def _params(*sem):
    return pltpu.CompilerParams(dimension_semantics=sem, vmem_limit_bytes=VMEM_LIMIT)


def _init_carry(carry_ref, hist_ref, first_step):
    hist = hist_ref.shape[0]

    @pl.when(first_step)
    def _():
        carry_ref[...] = jnp.zeros(carry_ref.shape, F32)
        carry_ref[CARRY_ROWS - hist:CARRY_ROWS, :] = hist_ref[...]


def _causal_conv(raw, carry_ref, cols, w_ref, b_ref, taps):
    tm = raw.shape[0]
    hist = taps - 1
    ext = jnp.concatenate([carry_ref[:, cols], raw], axis=0)
    acc = b_ref[:, cols] + raw * w_ref[hist:hist + 1, cols]
    for k in range(1, taps):
        acc = acc + ext[CARRY_ROWS - k:CARRY_ROWS - k + tm, :] * w_ref[hist - k:hist - k + 1, cols]
    carry_ref[:, cols] = raw[tm - CARRY_ROWS:, :]
    return acc


def _mamba_in_kernel(x_ref, buf_ref, g_ref, wzx_ref, wdt_ref, wdtt_ref, cw_ref, cb_ref, dtb_ref, dtbt_ref,
                     z_ref, xbc_ref, dt_ref, dtt_ref, nbuf_ref, ext_ref):
    _init_carry(ext_ref, buf_ref, pl.program_id(1) == 0)
    h = _rms(x_ref[...], g_ref[...]).astype(BF16)
    for j in range(D_INNER // COL_CHUNK):
        cols = slice(j * COL_CHUNK, (j + 1) * COL_CHUNK)
        z_ref[:, cols] = jnp.dot(h, wzx_ref[:, cols], preferred_element_type=F32).astype(BF16)
    for j in range(CONV_DIM // COL_CHUNK):
        cols = slice(j * COL_CHUNK, (j + 1) * COL_CHUNK)
        wcols = slice(D_INNER + j * COL_CHUNK, D_INNER + (j + 1) * COL_CHUNK)
        raw = jnp.dot(h, wzx_ref[:, wcols], preferred_element_type=F32)
        xbc_ref[:, cols] = _silu(_causal_conv(raw, ext_ref, cols, cw_ref, cb_ref, SSM_CONV)).astype(BF16)
    nbuf_ref[...] = ext_ref[CARRY_ROWS - (SSM_CONV - 1):, :]
    dt = jnp.dot(h, wdt_ref[...], preferred_element_type=F32)[:, :SSM_HEADS]
    dt_ref[...] = _softplus(dt + dtb_ref[...])
    dtt = lax.dot_general(wdtt_ref[...], h, NT_DIMS, preferred_element_type=F32)
    dtt_ref[...] = _softplus(dtt + dtbt_ref[...])


def _mamba_in(x, conv_buf, g, wzx, wdt, wdtt, cw, cb, dtb, dtbt, tm):
    b, t, _ = x.shape
    grid = (b, t // tm)
    row = lambda w: pl.BlockSpec((None, tm, w), lambda i, j: (i, j, 0))
    return pl.pallas_call(
        _mamba_in_kernel,
        grid=grid,
        in_specs=[
            row(D_MODEL),
            pl.BlockSpec((None, SSM_CONV - 1, CONV_DIM), lambda i, j: (i, 0, 0)),
            _const_spec((1, D_MODEL)),
            _const_spec((D_MODEL, D_INNER + CONV_DIM)),
            _const_spec((D_MODEL, LANE)),
            _const_spec((SSM_HEADS, D_MODEL)),
            _const_spec((SSM_CONV, CONV_DIM)),
            _const_spec((1, CONV_DIM)),
            _const_spec((1, SSM_HEADS)),
            _const_spec((SSM_HEADS, 1)),
        ],
        out_specs=[
            row(D_INNER),
            row(CONV_DIM),
            row(SSM_HEADS),
            pl.BlockSpec((None, SSM_HEADS, tm), lambda i, j: (i, 0, j)),
            pl.BlockSpec((None, SSM_CONV - 1, CONV_DIM), lambda i, j: (i, 0, 0)),
        ],
        out_shape=[
            jax.ShapeDtypeStruct((b, t, D_INNER), BF16),
            jax.ShapeDtypeStruct((b, t, CONV_DIM), BF16),
            jax.ShapeDtypeStruct((b, t, SSM_HEADS), F32),
            jax.ShapeDtypeStruct((b, SSM_HEADS, t), F32),
            jax.ShapeDtypeStruct((b, SSM_CONV - 1, CONV_DIM), F32),
        ],
        scratch_shapes=[pltpu.VMEM((CARRY_ROWS, CONV_DIM), F32)],
        compiler_params=_params("parallel", "arbitrary"),
        name="mamba_in",
    )(x, conv_buf, g, wzx, wdt, wdtt, cw, cb, dtb, dtbt)


def _ssd_kernel(xs_ref, bm_ref, cm_ref, z_ref, dt_ref, dtt_ref, arow_ref, acol_ref, exp_ref, dskip_ref, gn_ref,
                h0_ref, yg_ref, hout_ref, st_ref, *, nc):
    step = pl.program_id(1)

    @pl.when(step == 0)
    def _():
        st_ref[...] = h0_ref[...]

    li = lax.broadcasted_iota(jnp.int32, (CHUNK, CHUNK), 0)
    si = lax.broadcasted_iota(jnp.int32, (CHUNK, CHUNK), 1)
    tril = (si <= li).astype(F32)
    pi = lax.broadcasted_iota(jnp.int32, (LANE, LANE), 0)
    pj = lax.broadcasted_iota(jnp.int32, (LANE, LANE), 1)
    same_head = (pi // CHUNK) == (pj // CHUNK)
    triu2 = jnp.where(same_head, (pi <= pj).astype(F32), 0.0)
    l2 = lax.broadcasted_iota(jnp.int32, (CHUNK, LANE), 0)
    s2 = lax.broadcasted_iota(jnp.int32, (CHUNK, LANE), 1) % CHUNK
    causal2 = s2 <= l2
    expand = exp_ref[...]

    def chunk(c, carry):
        r0 = pl.multiple_of(c * CHUNK, CHUNK)
        rows = pl.ds(r0, CHUNK)
        dt = dt_ref[rows, :]
        dtt2 = dtt_ref[c]
        acs = jnp.dot(tril, dt * arow_ref[...], precision=lax.Precision.HIGHEST,
                      preferred_element_type=F32)
        acst2 = jnp.dot(dtt2 * acol_ref[...], triu2, precision=lax.Precision.HIGHEST,
                        preferred_element_type=F32)
        last = acs[CHUNK - 1:CHUNK, :]
        e_acs = jnp.exp(acs).astype(BF16)
        wst = (jnp.exp(last - acs) * dt).astype(BF16)
        hi = acs.astype(BF16)
        r1 = acs - hi.astype(F32)
        mid = r1.astype(BF16)
        lo = (r1 - mid.astype(F32)).astype(BF16)
        acs_w = (jnp.dot(hi, expand, preferred_element_type=F32) + jnp.dot(mid, expand, preferred_element_type=F32)
                 + jnp.dot(lo, expand, preferred_element_type=F32))
        e_acs_w = jnp.dot(e_acs, expand, preferred_element_type=F32)
        wst_w = jnp.dot(wst, expand, preferred_element_type=F32)
        decay_w = jnp.exp(acs_w[CHUNK - 1:CHUNK, :])
        for g in range(SSM_GROUPS):
            gsl = slice(g * GROUP_W, (g + 1) * GROUP_W)
            bg = bm_ref[rows, g * D_STATE:(g + 1) * D_STATE]
            cg = cm_ref[rows, g * D_STATE:(g + 1) * D_STATE]
            xg = xs_ref[rows, gsl]
            cb = lax.dot_general(cg, bg, NT_DIMS, preferred_element_type=F32)
            cb2 = jnp.concatenate([cb, cb], axis=1)
            ys = []
            for jj in range(SSM_HPG // 2):
                j = g * (SSM_HPG // 2) + jj
                seg = acs_w[:, j * LANE:(j + 1) * LANE] - acst2[j:j + 1, :]
                wts = cb2 * jnp.exp(jnp.where(causal2, seg, -jnp.inf)) * dtt2[j:j + 1, :]
                xp = xg[:, jj * LANE:(jj + 1) * LANE]
                xbd = jnp.where(same_head, jnp.concatenate([xp, xp], axis=0), jnp.zeros((), BF16))
                ys.append(jnp.dot(wts.astype(BF16), xbd, preferred_element_type=F32))
            y = jnp.concatenate(ys, axis=1)
            stg = st_ref[:, gsl]
            y = y + jnp.dot(cg, stg.astype(BF16), preferred_element_type=F32) * e_acs_w[:, gsl]
            xgf = xg.astype(F32)
            xw = (xgf * wst_w[:, gsl]).astype(BF16)
            st_ref[:, gsl] = stg * decay_w[:, gsl] + lax.dot_general(bg, xw, TN_DIMS, preferred_element_type=F32)
            y = y + xgf * dskip_ref[:, gsl]
            yg = y * _silu(z_ref[rows, gsl].astype(F32))
            yg_ref[rows, gsl] = _rms(yg, gn_ref[:, gsl]).astype(BF16)
        return carry

    lax.fori_loop(0, nc, chunk, 0)

    @pl.when(step == pl.num_programs(1) - 1)
    def _():
        hout_ref[...] = st_ref[...]


def _ssd(xbc, z, dt, dtt2, arow, acol2, expand, dskip, gn, h0t, nc):
    b, t, _ = xbc.shape
    lt = nc * CHUNK
    grid = (b, t // lt)
    state_spec = pl.BlockSpec((None, D_STATE, D_INNER), lambda i, j: (i, 0, 0))
    return pl.pallas_call(
        functools.partial(_ssd_kernel, nc=nc),
        grid=grid,
        in_specs=[
            pl.BlockSpec((None, lt, D_INNER), lambda i, j: (i, j, 0)),
            pl.BlockSpec((None, lt, BC_W), lambda i, j: (i, j, D_INNER // BC_W)),
            pl.BlockSpec((None, lt, BC_W), lambda i, j: (i, j, D_INNER // BC_W + 1)),
            pl.BlockSpec((None, lt, D_INNER), lambda i, j: (i, j, 0)),
            pl.BlockSpec((None, lt, SSM_HEADS), lambda i, j: (i, j, 0)),
            pl.BlockSpec((None, nc, SSM_HEADS // 2, 2 * CHUNK), lambda i, j: (i, j, 0, 0)),
            _const_spec((1, SSM_HEADS)),
            _const_spec((SSM_HEADS // 2, 2 * CHUNK)),
            _const_spec((SSM_HEADS, D_INNER)),
            _const_spec((1, D_INNER)),
            _const_spec((1, D_INNER)),
            state_spec,
        ],
        out_specs=[
            pl.BlockSpec((None, lt, D_INNER), lambda i, j: (i, j, 0)),
            state_spec,
        ],
        out_shape=[
            jax.ShapeDtypeStruct((b, t, D_INNER), BF16),
            jax.ShapeDtypeStruct((b, D_STATE, D_INNER), F32),
        ],
        scratch_shapes=[pltpu.VMEM((D_STATE, D_INNER), F32)],
        compiler_params=_params("parallel", "arbitrary"),
        name="ssd_scan",
    )(xbc, xbc, xbc, z, dt, dtt2, arow, acol2, expand, dskip, gn, h0t)


def _proj_res_kernel(a_ref, x_ref, w_ref, g_ref, o_ref):
    y = jnp.dot(a_ref[...], w_ref[...], preferred_element_type=F32)
    o_ref[...] = x_ref[...] + _rms(y, g_ref[...])


def _proj_res(a, x, w, g, tm):
    b, t, k = a.shape
    grid = (b, t // tm)
    return pl.pallas_call(
        _proj_res_kernel,
        grid=grid,
        in_specs=[
            pl.BlockSpec((None, tm, k), lambda i, j: (i, j, 0)),
            pl.BlockSpec((None, tm, D_MODEL), lambda i, j: (i, j, 0)),
            _const_spec((k, D_MODEL)),
            _const_spec((1, D_MODEL)),
        ],
        out_specs=pl.BlockSpec((None, tm, D_MODEL), lambda i, j: (i, j, 0)),
        out_shape=jax.ShapeDtypeStruct((b, t, D_MODEL), F32),
        compiler_params=_params("parallel", "parallel"),
        name="proj_res",
    )(a, x, w, g)


def _mla_out_kernel(o_ref, x_ref, wuv_ref, wo_ref, g_ref, out_ref):
    parts = []
    for h in range(MLA_HEADS):
        parts.append(jnp.dot(o_ref[h], wuv_ref[h], preferred_element_type=F32).astype(BF16))
    o = jnp.concatenate(parts, axis=-1)
    y = jnp.dot(o, wo_ref[...], preferred_element_type=F32)
    out_ref[...] = x_ref[...] + _rms(y, g_ref[...])


def _mla_out(o_lat, x, wuv, wo, g, tm):
    b, t, _ = x.shape
    grid = (b, t // tm)
    return pl.pallas_call(
        _mla_out_kernel,
        grid=grid,
        in_specs=[
            pl.BlockSpec((None, MLA_HEADS, tm, KV_RANK), lambda i, j: (i, 0, j, 0)),
            pl.BlockSpec((None, tm, D_MODEL), lambda i, j: (i, j, 0)),
            _const_spec((MLA_HEADS, KV_RANK, V_DIM)),
            _const_spec((MLA_HEADS * V_DIM, D_MODEL)),
            _const_spec((1, D_MODEL)),
        ],
        out_specs=pl.BlockSpec((None, tm, D_MODEL), lambda i, j: (i, j, 0)),
        out_shape=jax.ShapeDtypeStruct((b, t, D_MODEL), F32),
        compiler_params=_params("parallel", "parallel"),
        name="mla_out",
    )(o_lat, x, wuv, wo, g)


def _ffn_kernel(x_ref, buf_ref, gpre_ref, wup_ref, cw_ref, cb_ref, wdn_ref, gpost_ref,
                o_ref, nbuf_ref, ext_ref, act_ref):
    _init_carry(ext_ref, buf_ref, pl.program_id(1) == 0)
    x = x_ref[...]
    h = _rms(x, gpre_ref[...]).astype(BF16)
    for j in range(D_FF // FFN_CHUNK):
        vcols = slice(j * FFN_CHUNK, (j + 1) * FFN_CHUNK)
        gcols = slice(D_FF + j * FFN_CHUNK, D_FF + (j + 1) * FFN_CHUNK)
        val = _causal_conv(jnp.dot(h, wup_ref[:, vcols], preferred_element_type=F32),
                           ext_ref, vcols, cw_ref, cb_ref, FFN_CONV)
        gate = _causal_conv(jnp.dot(h, wup_ref[:, gcols], preferred_element_type=F32),
                            ext_ref, gcols, cw_ref, cb_ref, FFN_CONV)
        act_ref[:, vcols] = (jax.nn.gelu(gate, approximate=True) * val).astype(BF16)
    nbuf_ref[...] = ext_ref[CARRY_ROWS - (FFN_CONV - 1):, :]
    f = jnp.dot(act_ref[...], wdn_ref[...], preferred_element_type=F32)
    o_ref[...] = x + _rms(f, gpost_ref[...])


def _ffn(x, buf, gpre, wup, cw, cb, wdn, gpost, tm):
    b, t, _ = x.shape
    grid = (b, t // tm)
    buf_spec = pl.BlockSpec((None, FFN_CONV - 1, 2 * D_FF), lambda i, j: (i, 0, 0))
    return pl.pallas_call(
        _ffn_kernel,
        grid=grid,
        in_specs=[
            pl.BlockSpec((None, tm, D_MODEL), lambda i, j: (i, j, 0)),
            buf_spec,
            _const_spec((1, D_MODEL)),
            _const_spec((D_MODEL, 2 * D_FF)),
            _const_spec((FFN_CONV, 2 * D_FF)),
            _const_spec((1, 2 * D_FF)),
            _const_spec((D_FF, D_MODEL)),
            _const_spec((1, D_MODEL)),
        ],
        out_specs=[pl.BlockSpec((None, tm, D_MODEL), lambda i, j: (i, j, 0)), buf_spec],
        out_shape=[
            jax.ShapeDtypeStruct((b, t, D_MODEL), F32),
            jax.ShapeDtypeStruct((b, FFN_CONV - 1, 2 * D_FF), F32),
        ],
        scratch_shapes=[pltpu.VMEM((CARRY_ROWS, 2 * D_FF), F32), pltpu.VMEM((tm, D_FF), BF16)],
        compiler_params=_params("parallel", "arbitrary"),
        name="conv_ffn",
    )(x, buf, gpre, wup, cw, cb, wdn, gpost)


def _kv_kernel(x_ref, gin_ref, w_ref, gkv_ref, cos_ref, sin_ref, lat_ref, kpe_ref, kcat_ref):
    h = _rms(x_ref[...], gin_ref[...]).astype(BF16)
    hw = jnp.dot(h, w_ref[...], preferred_element_type=F32)
    lat = _rms(hw[:, :KV_RANK], gkv_ref[...])
    kpe = hw[:, KV_RANK:QK_CAT] * cos_ref[...] + hw[:, QK_CAT:] * sin_ref[...]
    lat_ref[...] = lat
    kpe_ref[...] = kpe
    kcat_ref[:, :KV_RANK] = lat.astype(BF16)
    kcat_ref[:, KV_RANK:] = kpe.astype(BF16)


def _kv_shared(x, gin, w, gkv, cos, sin, tm):
    b, t, _ = x.shape
    grid = (b, t // tm)
    row = lambda wd: pl.BlockSpec((None, tm, wd), lambda i, j: (i, j, 0))
    tab = pl.BlockSpec((tm, QK_ROPE), lambda i, j: (j, 0))
    return pl.pallas_call(
        _kv_kernel,
        grid=grid,
        in_specs=[
            row(D_MODEL),
            _const_spec((1, D_MODEL)),
            _const_spec((D_MODEL, KV_RANK + 2 * QK_ROPE)),
            _const_spec((1, KV_RANK)),
            tab,
            tab,
        ],
        out_specs=[row(KV_RANK), row(QK_ROPE), row(QK_CAT)],
        out_shape=[
            jax.ShapeDtypeStruct((b, t, KV_RANK), F32),
            jax.ShapeDtypeStruct((b, t, QK_ROPE), F32),
            jax.ShapeDtypeStruct((b, t, QK_CAT), BF16),
        ],
        compiler_params=_params("parallel", "parallel"),
        name="kv_shared",
    )(x, gin, w, gkv, cos, sin)


def _q_kernel(x_ref, gpre_ref, wdq_ref, gq_ref, wn_ref, wp_ref, wpr_ref, wuk_ref, cos_ref, sin_ref, q_ref, *, scale):
    h = _rms(x_ref[...], gpre_ref[...]).astype(BF16)
    cq = _rms(jnp.dot(h, wdq_ref[...], preferred_element_type=F32), gq_ref[...]).astype(BF16)
    qn = jnp.dot(cq, wn_ref[...], preferred_element_type=F32).astype(BF16)
    qp = jnp.dot(cq, wp_ref[...], preferred_element_type=F32)
    qpr = jnp.dot(cq, wpr_ref[...], preferred_element_type=F32)
    qpe = ((qp * cos_ref[...] + qpr * sin_ref[...]) * scale).astype(BF16)
    for hd in range(MLA_HEADS):
        ql = jnp.dot(qn[:, hd * QK_NOPE:(hd + 1) * QK_NOPE], wuk_ref[hd], preferred_element_type=F32)
        q_ref[hd, :, :KV_RANK] = (ql * scale).astype(BF16)
        q_ref[hd, :, KV_RANK:] = qpe[:, hd * QK_ROPE:(hd + 1) * QK_ROPE]


def _q_proj(x, gpre, wdq, gq, wn, wp, wpr, wuk, cos, sin, tm):
    b, t, _ = x.shape
    grid = (b, t // tm)
    tab = pl.BlockSpec((tm, MLA_HEADS * QK_ROPE), lambda i, j: (j, 0))
    scale = (QK_NOPE + QK_ROPE) ** -0.5 * LOG2E
    return pl.pallas_call(
        functools.partial(_q_kernel, scale=scale),
        grid=grid,
        in_specs=[
            pl.BlockSpec((None, tm, D_MODEL), lambda i, j: (i, j, 0)),
            _const_spec((1, D_MODEL)),
            _const_spec((D_MODEL, Q_RANK)),
            _const_spec((1, Q_RANK)),
            _const_spec((Q_RANK, MLA_HEADS * QK_NOPE)),
            _const_spec((Q_RANK, MLA_HEADS * QK_ROPE)),
            _const_spec((Q_RANK, MLA_HEADS * QK_ROPE)),
            _const_spec((MLA_HEADS, QK_NOPE, KV_RANK)),
            tab,
            tab,
        ],
        out_specs=pl.BlockSpec((None, MLA_HEADS, tm, QK_CAT), lambda i, j: (i, 0, j, 0)),
        out_shape=jax.ShapeDtypeStruct((b, MLA_HEADS, t, QK_CAT), BF16),
        compiler_params=_params("parallel", "parallel"),
        name="q_proj",
    )(x, gpre, wdq, gq, wn, wp, wpr, wuk, cos, sin)


def _attn_kernel(q_ref, k_ref, vt_ref, o_ref, m_ref, l_ref, a_ref, acc_ref, s_ref, p_ref, *, tk, past, nkv):
    qi = pl.program_id(1)
    rows = MLA_HEADS * CHUNK
    nblk = rows // LANE
    m_ref[...] = jnp.full(m_ref.shape, -jnp.inf, F32)
    l_ref[...] = jnp.zeros(l_ref.shape, F32)
    acc_ref[...] = jnp.zeros(acc_ref.shape, F32)
    q_chunk = past // CHUNK + qi
    n_vis = jnp.minimum(nkv, (past + (qi + 1) * CHUNK + tk - 1) // tk)
    key_chunk = lax.broadcasted_iota(jnp.int32, (tk, LANE), 0) // CHUNK

    def scores(j):
        q = q_ref[...].reshape(rows, QK_CAT)
        ks = k_ref[pl.ds(pl.multiple_of(j * tk, tk), tk), :]
        return lax.dot_general(ks, q, NT_DIMS, preferred_element_type=F32)

    def absorb(j):
        slot = j % 2
        bias = jnp.where(key_chunk + j * (tk // CHUNK) <= q_chunk, 0.0, -jnp.inf)
        for b in range(nblk):
            csl = slice(b * LANE, (b + 1) * LANE)
            s = s_ref[slot, :, csl] + bias
            m_prev = m_ref[:, csl]
            m_new = jnp.maximum(m_prev, jnp.max(s, axis=0, keepdims=True))
            alpha = jnp.exp2(m_prev - m_new)
            p = jnp.exp2(s - m_new)
            l_ref[:, csl] = alpha * l_ref[:, csl] + jnp.sum(p, axis=0, keepdims=True)
            m_ref[:, csl] = m_new
            a_ref[:, csl] = alpha
            p_ref[:, csl] = p.astype(BF16)
        pv = jnp.dot(vt_ref[j], p_ref[...], preferred_element_type=F32)
        for b in range(nblk):
            csl = slice(b * LANE, (b + 1) * LANE)
            acc_ref[:, csl] = a_ref[:, csl] * acc_ref[:, csl] + pv[:, csl]

    s_ref[0] = scores(0)

    def body(j, carry):
        s_new = scores(j)
        absorb(j - 1)
        s_ref[j % 2] = s_new
        return carry

    lax.fori_loop(1, n_vis, body, 0)
    absorb(n_vis - 1)
    o = (acc_ref[...] / l_ref[...]).T
    o_ref[...] = o.astype(BF16).reshape(MLA_HEADS, CHUNK, KV_RANK)


def _attention(q, kcat, vt, past, tk):
    b, _, t, _ = q.shape
    s_len = kcat.shape[1]
    nkv = s_len // tk
    tq = CHUNK
    grid = (b, t // tq)
    rows = MLA_HEADS * tq
    return pl.pallas_call(
        functools.partial(_attn_kernel, tk=tk, past=past, nkv=nkv),
        grid=grid,
        in_specs=[
            pl.BlockSpec((None, MLA_HEADS, tq, QK_CAT), lambda i, j: (i, 0, j, 0)),
            pl.BlockSpec((None, s_len, QK_CAT), lambda i, j: (i, 0, 0)),
            pl.BlockSpec((None, nkv, KV_RANK, tk), lambda i, j: (i, 0, 0, 0)),
        ],
        out_specs=pl.BlockSpec((None, MLA_HEADS, tq, KV_RANK), lambda i, j: (i, 0, j, 0)),
        out_shape=jax.ShapeDtypeStruct((b, MLA_HEADS, t, KV_RANK), BF16),
        scratch_shapes=[
            pltpu.VMEM((1, rows), F32),
            pltpu.VMEM((1, rows), F32),
            pltpu.VMEM((1, rows), F32),
            pltpu.VMEM((KV_RANK, rows), F32),
            pltpu.VMEM((2, tk, rows), F32),
            pltpu.VMEM((tk, rows), BF16),
        ],
        compiler_params=_params("parallel", "parallel"),
        name="mla_attention",
    )(q, kcat, vt)


def _rope_tables(past, t):
    half = QK_ROPE // 2
    inv = jnp.exp(-math.log(ROPE_THETA) * jnp.arange(half, dtype=F32) / half)
    ang = (past + jnp.arange(t, dtype=jnp.int32)).astype(F32)[:, None] * inv[None, :]
    cos = jnp.cos(ang)
    sin = jnp.sin(ang)
    return jnp.concatenate([cos, cos], axis=-1), jnp.concatenate([sin, sin], axis=-1)


def _rotate_half_cols(w):
    shp = w.shape
    w4 = w.reshape(shp[:-1] + (shp[-1] // QK_ROPE, 2, QK_ROPE // 2))
    return jnp.concatenate([-w4[..., 1:2, :], w4[..., 0:1, :]], axis=-2).reshape(shp)


def _tile(t, pref):
    return pref if t % pref == 0 else t


def _prep_weights(norm_mix_pre, norm_mix_post, norm_ffn_pre, norm_ffn_post,
                  ssm_w_in, ssm_conv_w, ssm_conv_b, ssm_dt_bias, ssm_a_log, ssm_d, ssm_norm, ssm_w_out,
                  kv_norm_in, kv_w_dkv, kv_norm, kv_w_kr, kv_w_uk, kv_w_uv,
                  mla_w_dq, mla_q_norm, mla_w_uq, mla_w_o,
                  ffn_w_up, ffn_conv_w, ffn_conv_b, ffn_w_down):
    w = {}
    w_in = ssm_w_in[0]
    w["wzx"] = w_in[:, :D_INNER + CONV_DIM].astype(BF16)
    w_dt = w_in[:, D_INNER + CONV_DIM:]
    w["wdt"] = jnp.pad(w_dt, ((0, 0), (0, LANE - SSM_HEADS))).astype(BF16)
    w["wdtt"] = w_dt.T.astype(BF16)
    w["ssm_cw"] = ssm_conv_w[0]
    w["ssm_cb"] = ssm_conv_b[0][None, :]
    w["dtb"] = ssm_dt_bias[0][None, :]
    w["dtbt"] = ssm_dt_bias[0][:, None]
    a = -jnp.exp(ssm_a_log[0].astype(F32))
    w["arow"] = a[None, :]
    w["acol2"] = jnp.broadcast_to(a.reshape(SSM_HEADS // 2, 2, 1),
                                  (SSM_HEADS // 2, 2, CHUNK)).reshape(SSM_HEADS // 2, 2 * CHUNK)
    w["expand"] = jnp.repeat(jnp.eye(SSM_HEADS, dtype=BF16), SSM_HEADDIM, axis=1)
    w["dskip"] = jnp.repeat(ssm_d[0], SSM_HEADDIM)[None, :]
    w["gn"] = ssm_norm[0][None, :]
    w["w_out"] = ssm_w_out[0].astype(BF16)
    w["g_mix_pre"] = norm_mix_pre[:, None, :]
    w["g_mix_post"] = norm_mix_post[:, None, :]
    w["g_ffn_pre"] = norm_ffn_pre[:, None, :]
    w["g_ffn_post"] = norm_ffn_post[:, None, :]
    w["ffn_up"] = ffn_w_up.astype(BF16)
    w["ffn_cw"] = ffn_conv_w
    w["ffn_cb"] = ffn_conv_b[:, None, :]
    w["ffn_down"] = ffn_w_down.astype(BF16)
    w["kv_gin"] = kv_norm_in[None, :]
    w["kv_w"] = jnp.concatenate([kv_w_dkv, kv_w_kr, _rotate_half_cols(kv_w_kr)], axis=-1).astype(BF16)
    w["kv_g"] = kv_norm[None, :]
    w["wdq"] = mla_w_dq[0].astype(BF16)
    w["gq"] = mla_q_norm[0][None, :]
    wuq = mla_w_uq[0].reshape(Q_RANK, MLA_HEADS, QK_NOPE + QK_ROPE)
    w["wq_nope"] = wuq[:, :, :QK_NOPE].reshape(Q_RANK, MLA_HEADS * QK_NOPE).astype(BF16)
    wq_pe = wuq[:, :, QK_NOPE:].reshape(Q_RANK, MLA_HEADS * QK_ROPE)
    w["wq_pe"] = wq_pe.astype(BF16)
    w["wq_pe_rot"] = _rotate_half_cols(wq_pe).astype(BF16)
    w["wuk_t"] = jnp.transpose(kv_w_uk, (1, 2, 0)).astype(BF16)
    w["wuv"] = jnp.transpose(kv_w_uv, (1, 0, 2)).astype(BF16)
    w["wo"] = mla_w_o[0].astype(BF16)
    return w


def _trunk(x, ssm_conv_buf, ssm_state, ffn_buf, past_kcat, w):
    b, t, _ = x.shape
    past = 0 if past_kcat is None else past_kcat.shape[1]
    assert t % CHUNK == 0 and past % CHUNK == 0
    tm = _tile(t, 256)

    z, xbc, dt, dtt, new_conv = _mamba_in(x, ssm_conv_buf, w["g_mix_pre"][0], w["wzx"], w["wdt"], w["wdtt"],
                                          w["ssm_cw"], w["ssm_cb"], w["dtb"], w["dtbt"], tm)
    nc = _tile(t, 256) // CHUNK
    dtt2 = dtt.reshape(b, SSM_HEADS // 2, 2, t // CHUNK, CHUNK).transpose(0, 3, 1, 2, 4)
    dtt2 = dtt2.reshape(b, t // CHUNK, SSM_HEADS // 2, 2 * CHUNK)
    h0t = ssm_state.reshape(b, D_INNER, D_STATE).transpose(0, 2, 1)
    yg, new_state_t = _ssd(xbc, z, dt, dtt2, w["arow"], w["acol2"], w["expand"], w["dskip"], w["gn"], h0t, nc)
    new_state = new_state_t.transpose(0, 2, 1).reshape(b, SSM_HEADS, SSM_HEADDIM, D_STATE)
    x = _proj_res(yg, x, w["w_out"], w["g_mix_post"][0], tm)
    x, new_ffn0 = _ffn(x, ffn_buf[0], w["g_ffn_pre"][0], w["ffn_up"][0], w["ffn_cw"][0], w["ffn_cb"][0],
                       w["ffn_down"][0], w["g_ffn_post"][0], tm)

    cos, sin = _rope_tables(past, t)
    lat, kpe, kcat = _kv_shared(x, w["kv_gin"], w["kv_w"], w["kv_g"], cos, sin, tm)
    if past_kcat is not None:
        kcat = jnp.concatenate([past_kcat, kcat], axis=1)

    cos_h = jnp.tile(cos, (1, MLA_HEADS))
    sin_h = jnp.tile(sin, (1, MLA_HEADS))
    q = _q_proj(x, w["g_mix_pre"][1], w["wdq"], w["gq"], w["wq_nope"], w["wq_pe"], w["wq_pe_rot"], w["wuk_t"],
                cos_h, sin_h, tm)
    s_len = past + t
    s_pad = -(-s_len // ATT_TK) * ATT_TK
    if s_pad != s_len:
        kcat = jnp.pad(kcat, ((0, 0), (0, s_pad - s_len), (0, 0)))
    vt = kcat[:, :, :KV_RANK].reshape(b, s_pad // ATT_TK, ATT_TK, KV_RANK).transpose(0, 1, 3, 2)
    o_lat = _attention(q, kcat, vt, past, ATT_TK)
    x = _mla_out(o_lat, x, w["wuv"], w["wo"], w["g_mix_post"][1], tm)
    x, new_ffn1 = _ffn(x, ffn_buf[1], w["g_ffn_pre"][1], w["ffn_up"][1], w["ffn_cw"][1], w["ffn_cb"][1],
                       w["ffn_down"][1], w["g_ffn_post"][1], tm)
    return (x, new_state[None], new_conv[None], jnp.stack([new_ffn0, new_ffn1]), lat, kpe)


def kernel(x_prompt, x_sample, state_ssm, state_ssm_conv, state_ffn_conv, cache_kv_latent, cache_k_rope, norm_mix_pre, norm_mix_post, norm_ffn_pre, norm_ffn_post, ssm_w_in, ssm_conv_w, ssm_conv_b, ssm_dt_bias, ssm_a_log, ssm_d, ssm_norm, ssm_w_out, kv_norm_in, kv_w_dkv, kv_norm, kv_w_kr, kv_w_uk, kv_w_uv, mla_w_dq, mla_q_norm, mla_w_uq, mla_w_o, ffn_w_up, ffn_conv_w, ffn_conv_b, ffn_w_down):
    w = _prep_weights(norm_mix_pre, norm_mix_post, norm_ffn_pre, norm_ffn_post,
                      ssm_w_in, ssm_conv_w, ssm_conv_b, ssm_dt_bias, ssm_a_log, ssm_d, ssm_norm, ssm_w_out,
                      kv_norm_in, kv_w_dkv, kv_norm, kv_w_kr, kv_w_uk, kv_w_uv,
                      mla_w_dq, mla_q_norm, mla_w_uq, mla_w_o,
                      ffn_w_up, ffn_conv_w, ffn_conv_b, ffn_w_down)
    bp = x_prompt.shape[0]
    dtp = x_prompt.dtype
    y_p, p_ssm, p_conv, p_ffn, p_lat, p_kpe = _trunk(
        x_prompt,
        jnp.zeros((bp, SSM_CONV - 1, CONV_DIM), dtp),
        jnp.zeros((bp, SSM_HEADS, SSM_HEADDIM, D_STATE), dtp),
        jnp.zeros((2, bp, FFN_CONV - 1, 2 * D_FF), dtp),
        None, w)
    past_kcat = jnp.concatenate([cache_kv_latent, cache_k_rope], axis=-1).astype(BF16)
    y_s, s_ssm, s_conv, s_ffn, s_lat, s_kpe = _trunk(
        x_sample, state_ssm_conv[0], state_ssm[0], state_ffn_conv, past_kcat, w)
    return (y_p, y_s, p_ssm, p_conv, p_ffn, p_lat, p_kpe, s_ssm, s_conv, s_ffn, s_lat, s_kpe)
```

```python
import functools
import math

import jax
import jax.numpy as jnp
from jax import lax
from jax.experimental import pallas as pl
from jax.experimental.pallas import tpu as pltpu

F32 = jnp.float32
BF16 = jnp.bfloat16

D_MODEL = 1024
CHUNK = 64
EPS = 1e-6
D_INNER = 2048
SSM_HEADDIM = 64
SSM_HEADS = 32
SSM_GROUPS = 4
SSM_HPG = 8
D_STATE = 128
SSM_CONV = 4
GROUP_W = D_INNER // SSM_GROUPS
BC_W = SSM_GROUPS * D_STATE
CONV_DIM = D_INNER + 2 * BC_W
MLA_HEADS = 16
Q_RANK = 384
KV_RANK = 256
QK_NOPE = 128
QK_ROPE = 64
V_DIM = 128
QK_CAT = KV_RANK + QK_ROPE
ROPE_THETA = 10000.0
D_FF = 2816
FFN_CONV = 3
LANE = 128
CARRY_ROWS = 8
COL_CHUNK = 512
FFN_CHUNK = 256
ATT_ROWS = 128
ATT_TK = 256
LOG2E = 1.4426950408889634
VMEM_LIMIT = 56 * 1024 * 1024

NT_DIMS = (((1,), (1,)), ((), ()))
TN_DIMS = (((0,), (0,)), ((), ()))


def _rms(x, g):
    return x * lax.rsqrt(jnp.mean(x * x, axis=-1, keepdims=True) + EPS) * g


def _silu(x):
    return x * (1.0 / (1.0 + jnp.exp(-x)))


def _softplus(x):
    return jnp.maximum(x, 0.0) + jnp.log(1.0 + jnp.exp(-jnp.abs(x)))


def _const_spec(shape):
    zeros = (0,) * len(shape)
    return pl.BlockSpec(shape, lambda *_: zeros, pipeline_mode=pl.Buffered(1))


def _params(*sem):
    return pltpu.CompilerParams(dimension_semantics=sem, vmem_limit_bytes=VMEM_LIMIT)


def _init_carry(carry_ref, hist_ref, first_step):
    hist = hist_ref.shape[0]

    @pl.when(first_step)
    def _():
        carry_ref[...] = jnp.zeros(carry_ref.shape, F32)
        carry_ref[CARRY_ROWS - hist:CARRY_ROWS, :] = hist_ref[...]


def _causal_conv(raw, carry_ref, cols, w_ref, b_ref, taps):
    tm = raw.shape[0]
    hist = taps - 1
    ext = jnp.concatenate([carry_ref[:, cols], raw], axis=0)
    acc = b_ref[:, cols] + raw * w_ref[hist:hist + 1, cols]
    for k in range(1, taps):
        acc = acc + ext[CARRY_ROWS - k:CARRY_ROWS - k + tm, :] * w_ref[hist - k:hist - k + 1, cols]
    carry_ref[:, cols] = raw[tm - CARRY_ROWS:, :]
    return acc


def _mamba_in_kernel(x_ref, buf_ref, g_ref, wzx_ref, wdt_ref, wdtt_ref, cw_ref, cb_ref, dtb_ref, dtbt_ref,
                     z_ref, xbc_ref, dt_ref, dtt_ref, nbuf_ref, ext_ref):
    _init_carry(ext_ref, buf_ref, pl.program_id(1) == 0)
    h = _rms(x_ref[...], g_ref[...]).astype(BF16)
    for j in range(D_INNER // COL_CHUNK):
        cols = slice(j * COL_CHUNK, (j + 1) * COL_CHUNK)
        z_ref[:, cols] = jnp.dot(h, wzx_ref[:, cols], preferred_element_type=F32).astype(BF16)
    for j in range(CONV_DIM // COL_CHUNK):
        cols = slice(j * COL_CHUNK, (j + 1) * COL_CHUNK)
        wcols = slice(D_INNER + j * COL_CHUNK, D_INNER + (j + 1) * COL_CHUNK)
        raw = jnp.dot(h, wzx_ref[:, wcols], preferred_element_type=F32)
        xbc_ref[:, cols] = _silu(_causal_conv(raw, ext_ref, cols, cw_ref, cb_ref, SSM_CONV)).astype(BF16)
    nbuf_ref[...] = ext_ref[CARRY_ROWS - (SSM_CONV - 1):, :]
    dt = jnp.dot(h, wdt_ref[...], preferred_element_type=F32)[:, :SSM_HEADS]
    dt_ref[...] = _softplus(dt + dtb_ref[...])
    dtt = lax.dot_general(wdtt_ref[...], h, NT_DIMS, preferred_element_type=F32)
    dtt_ref[...] = _softplus(dtt + dtbt_ref[...])


def _mamba_in(x, conv_buf, g, wzx, wdt, wdtt, cw, cb, dtb, dtbt, tm):
    b, t, _ = x.shape
    grid = (b, t // tm)
    row = lambda w: pl.BlockSpec((None, tm, w), lambda i, j: (i, j, 0))
    return pl.pallas_call(
        _mamba_in_kernel,
        grid=grid,
        in_specs=[
            row(D_MODEL),
            pl.BlockSpec((None, SSM_CONV - 1, CONV_DIM), lambda i, j: (i, 0, 0)),
            _const_spec((1, D_MODEL)),
            _const_spec((D_MODEL, D_INNER + CONV_DIM)),
            _const_spec((D_MODEL, LANE)),
            _const_spec((SSM_HEADS, D_MODEL)),
            _const_spec((SSM_CONV, CONV_DIM)),
            _const_spec((1, CONV_DIM)),
            _const_spec((1, SSM_HEADS)),
            _const_spec((SSM_HEADS, 1)),
        ],
        out_specs=[
            row(D_INNER),
            row(CONV_DIM),
            row(SSM_HEADS),
            pl.BlockSpec((None, SSM_HEADS, tm), lambda i, j: (i, 0, j)),
            pl.BlockSpec((None, SSM_CONV - 1, CONV_DIM), lambda i, j: (i, 0, 0)),
        ],
        out_shape=[
            jax.ShapeDtypeStruct((b, t, D_INNER), BF16),
            jax.ShapeDtypeStruct((b, t, CONV_DIM), BF16),
            jax.ShapeDtypeStruct((b, t, SSM_HEADS), F32),
            jax.ShapeDtypeStruct((b, SSM_HEADS, t), F32),
            jax.ShapeDtypeStruct((b, SSM_CONV - 1, CONV_DIM), F32),
        ],
        scratch_shapes=[pltpu.VMEM((CARRY_ROWS, CONV_DIM), F32)],
        compiler_params=_params("parallel", "arbitrary"),
        name="mamba_in",
    )(x, conv_buf, g, wzx, wdt, wdtt, cw, cb, dtb, dtbt)


def _ssd_kernel(xs_ref, bm_ref, cm_ref, z_ref, dt_ref, dtt_ref, arow_ref, acol_ref, exp_ref, dskip_ref, gn_ref,
                h0_ref, yg_ref, hout_ref, st_ref, *, nc):
    step = pl.program_id(1)

    @pl.when(step == 0)
    def _():
        st_ref[...] = h0_ref[...]

    li = lax.broadcasted_iota(jnp.int32, (CHUNK, CHUNK), 0)
    si = lax.broadcasted_iota(jnp.int32, (CHUNK, CHUNK), 1)
    tril = (si <= li).astype(F32)
    pi = lax.broadcasted_iota(jnp.int32, (LANE, LANE), 0)
    pj = lax.broadcasted_iota(jnp.int32, (LANE, LANE), 1)
    same_head = (pi // CHUNK) == (pj // CHUNK)
    triu2 = jnp.where(same_head, (pi <= pj).astype(F32), 0.0)
    l2 = lax.broadcasted_iota(jnp.int32, (CHUNK, LANE), 0)
    s2 = lax.broadcasted_iota(jnp.int32, (CHUNK, LANE), 1) % CHUNK
    causal2 = s2 <= l2
    expand = exp_ref[...]

    def chunk(c, carry):
        r0 = pl.multiple_of(c * CHUNK, CHUNK)
        rows = pl.ds(r0, CHUNK)
        dt = dt_ref[rows, :]
        dtt2 = dtt_ref[c]
        acs = jnp.dot(tril, dt * arow_ref[...], precision=lax.Precision.HIGHEST,
                      preferred_element_type=F32)
        acst2 = jnp.dot(dtt2 * acol_ref[...], triu2, precision=lax.Precision.HIGHEST,
                        preferred_element_type=F32)
        last = acs[CHUNK - 1:CHUNK, :]
        e_acs = jnp.exp(acs).astype(BF16)
        wst = (jnp.exp(last - acs) * dt).astype(BF16)
        hi = acs.astype(BF16)
        r1 = acs - hi.astype(F32)
        mid = r1.astype(BF16)
        lo = (r1 - mid.astype(F32)).astype(BF16)
        acs_w = (jnp.dot(hi, expand, preferred_element_type=F32) + jnp.dot(mid, expand, preferred_element_type=F32)
                 + jnp.dot(lo, expand, preferred_element_type=F32))
        e_acs_w = jnp.dot(e_acs, expand, preferred_element_type=F32)
        wst_w = jnp.dot(wst, expand, preferred_element_type=F32)
        decay_w = jnp.exp(acs_w[CHUNK - 1:CHUNK, :])
        for g in range(SSM_GROUPS):
            gsl = slice(g * GROUP_W, (g + 1) * GROUP_W)
            bg = bm_ref[rows, g * D_STATE:(g + 1) * D_STATE]
            cg = cm_ref[rows, g * D_STATE:(g + 1) * D_STATE]
            xg = xs_ref[rows, gsl]
            cb = lax.dot_general(cg, bg, NT_DIMS, preferred_element_type=F32)
            cb2 = jnp.concatenate([cb, cb], axis=1)
            ys = []
            for jj in range(SSM_HPG // 2):
                j = g * (SSM_HPG // 2) + jj
                seg = acs_w[:, j * LANE:(j + 1) * LANE] - acst2[j:j + 1, :]
                wts = cb2 * jnp.exp(jnp.where(causal2, seg, -jnp.inf)) * dtt2[j:j + 1, :]
                xp = xg[:, jj * LANE:(jj + 1) * LANE]
                xbd = jnp.where(same_head, jnp.concatenate([xp, xp], axis=0), jnp.zeros((), BF16))
                ys.append(jnp.dot(wts.astype(BF16), xbd, preferred_element_type=F32))
            y = jnp.concatenate(ys, axis=1)
            stg = st_ref[:, gsl]
            y = y + jnp.dot(cg, stg.astype(BF16), preferred_element_type=F32) * e_acs_w[:, gsl]
            xgf = xg.astype(F32)
            xw = (xgf * wst_w[:, gsl]).astype(BF16)
            st_ref[:, gsl] = stg * decay_w[:, gsl] + lax.dot_general(bg, xw, TN_DIMS, preferred_element_type=F32)
            y = y + xgf * dskip_ref[:, gsl]
            yg = y * _silu(z_ref[rows, gsl].astype(F32))
            yg_ref[rows, gsl] = _rms(yg, gn_ref[:, gsl]).astype(BF16)
        return carry

    lax.fori_loop(0, nc, chunk, 0)

    @pl.when(step == pl.num_programs(1) - 1)
    def _():
        hout_ref[...] = st_ref[...]


def _ssd(xbc, z, dt, dtt2, arow, acol2, expand, dskip, gn, h0t, nc):
    b, t, _ = xbc.shape
    lt = nc * CHUNK
    grid = (b, t // lt)
    state_spec = pl.BlockSpec((None, D_STATE, D_INNER), lambda i, j: (i, 0, 0))
    return pl.pallas_call(
        functools.partial(_ssd_kernel, nc=nc),
        grid=grid,
        in_specs=[
            pl.BlockSpec((None, lt, D_INNER), lambda i, j: (i, j, 0)),
            pl.BlockSpec((None, lt, BC_W), lambda i, j: (i, j, D_INNER // BC_W)),
            pl.BlockSpec((None, lt, BC_W), lambda i, j: (i, j, D_INNER // BC_W + 1)),
            pl.BlockSpec((None, lt, D_INNER), lambda i, j: (i, j, 0)),
            pl.BlockSpec((None, lt, SSM_HEADS), lambda i, j: (i, j, 0)),
            pl.BlockSpec((None, nc, SSM_HEADS // 2, 2 * CHUNK), lambda i, j: (i, j, 0, 0)),
            _const_spec((1, SSM_HEADS)),
            _const_spec((SSM_HEADS // 2, 2 * CHUNK)),
            _const_spec((SSM_HEADS, D_INNER)),
            _const_spec((1, D_INNER)),
            _const_spec((1, D_INNER)),
            state_spec,
        ],
        out_specs=[
            pl.BlockSpec((None, lt, D_INNER), lambda i, j: (i, j, 0)),
            state_spec,
        ],
        out_shape=[
            jax.ShapeDtypeStruct((b, t, D_INNER), BF16),
            jax.ShapeDtypeStruct((b, D_STATE, D_INNER), F32),
        ],
        scratch_shapes=[pltpu.VMEM((D_STATE, D_INNER), F32)],
        compiler_params=_params("parallel", "arbitrary"),
        name="ssd_scan",
    )(xbc, xbc, xbc, z, dt, dtt2, arow, acol2, expand, dskip, gn, h0t)


def _mixer_residual(mix_bf16, x, w_ref, g_ref):
    return x + _rms(jnp.dot(mix_bf16, w_ref[...], preferred_element_type=F32), g_ref[...])


def _mla_values(o_ref, wuv_ref):
    parts = []
    for h in range(MLA_HEADS):
        parts.append(jnp.dot(o_ref[h], wuv_ref[h], preferred_element_type=F32).astype(BF16))
    return jnp.concatenate(parts, axis=-1)


FFN_WEIGHT_SPECS = (
    (1, D_MODEL), (D_MODEL, 2 * D_FF), (FFN_CONV, 2 * D_FF), (1, 2 * D_FF), (D_FF, D_MODEL), (1, D_MODEL))


def _ffn_body(x, buf_ref, gpre_ref, wup_ref, cw_ref, cb_ref, wdn_ref, gpost_ref, nbuf_ref, ext_ref, act_ref):
    _init_carry(ext_ref, buf_ref, pl.program_id(1) == 0)
    h = _rms(x, gpre_ref[...]).astype(BF16)
    for j in range(D_FF // FFN_CHUNK):
        vcols = slice(j * FFN_CHUNK, (j + 1) * FFN_CHUNK)
        gcols = slice(D_FF + j * FFN_CHUNK, D_FF + (j + 1) * FFN_CHUNK)
        val = _causal_conv(jnp.dot(h, wup_ref[:, vcols], preferred_element_type=F32),
                           ext_ref, vcols, cw_ref, cb_ref, FFN_CONV)
        gate = _causal_conv(jnp.dot(h, wup_ref[:, gcols], preferred_element_type=F32),
                            ext_ref, gcols, cw_ref, cb_ref, FFN_CONV)
        act_ref[:, vcols] = (jax.nn.gelu(gate, approximate=True) * val).astype(BF16)
    nbuf_ref[...] = ext_ref[CARRY_ROWS - (FFN_CONV - 1):, :]
    f = jnp.dot(act_ref[...], wdn_ref[...], preferred_element_type=F32)
    return x + _rms(f, gpost_ref[...])


def _ffn_scratch(tm):
    return [pltpu.VMEM((CARRY_ROWS, 2 * D_FF), F32), pltpu.VMEM((tm, D_FF), BF16)]


KV_WEIGHT_SPECS = ((1, D_MODEL), (D_MODEL, KV_RANK + 2 * QK_ROPE), (1, KV_RANK))


def _kv_body(x, gin_ref, w_ref, gkv_ref, cos_ref, sin_ref, lat_ref, kpe_ref, kcat_ref):
    h = _rms(x, gin_ref[...]).astype(BF16)
    hw = jnp.dot(h, w_ref[...], preferred_element_type=F32)
    lat = _rms(hw[:, :KV_RANK], gkv_ref[...])
    kpe = hw[:, KV_RANK:QK_CAT] * cos_ref[:, :QK_ROPE] + hw[:, QK_CAT:] * sin_ref[:, :QK_ROPE]
    lat_ref[...] = lat
    kpe_ref[...] = kpe
    kcat_ref[:, :KV_RANK] = lat.astype(BF16)
    kcat_ref[:, KV_RANK:] = kpe.astype(BF16)


Q_SCALE = (QK_NOPE + QK_ROPE) ** -0.5 * LOG2E
Q_WEIGHT_SPECS = (
    (1, D_MODEL), (D_MODEL, Q_RANK), (1, Q_RANK), (Q_RANK, MLA_HEADS * QK_NOPE), (Q_RANK, MLA_HEADS * QK_ROPE),
    (Q_RANK, MLA_HEADS * QK_ROPE), (MLA_HEADS, QK_NOPE, KV_RANK))


def _q_body(x, gpre_ref, wdq_ref, gq_ref, wn_ref, wp_ref, wpr_ref, wuk_ref, cos_ref, sin_ref, q_ref):
    scale = Q_SCALE
    h = _rms(x, gpre_ref[...]).astype(BF16)
    cq = _rms(jnp.dot(h, wdq_ref[...], preferred_element_type=F32), gq_ref[...]).astype(BF16)
    qn = jnp.dot(cq, wn_ref[...], preferred_element_type=F32).astype(BF16)
    qp = jnp.dot(cq, wp_ref[...], preferred_element_type=F32)
    qpr = jnp.dot(cq, wpr_ref[...], preferred_element_type=F32)
    reps = MLA_HEADS * QK_ROPE // LANE
    cos = jnp.concatenate([cos_ref[...]] * reps, axis=1)
    sin = jnp.concatenate([sin_ref[...]] * reps, axis=1)
    qpe = ((qp * cos + qpr * sin) * scale).astype(BF16)
    for hd in range(MLA_HEADS):
        ql = jnp.dot(qn[:, hd * QK_NOPE:(hd + 1) * QK_NOPE], wuk_ref[hd], preferred_element_type=F32)
        q_ref[hd, :, :KV_RANK] = (ql * scale).astype(BF16)
        q_ref[hd, :, KV_RANK:] = qpe[:, hd * QK_ROPE:(hd + 1) * QK_ROPE]


def _layer0_tail_kernel(*refs):
    yg_ref, x_ref, buf_ref, cos_ref, sin_ref, wout_ref, gmix_ref = refs[:7]
    ffn_w = refs[7:13]
    kv_w = refs[13:16]
    q_w = refs[16:23]
    x_out_ref, nbuf_ref, lat_ref, kpe_ref, kcat_ref, q_ref, ext_ref, act_ref = refs[23:]
    x = _mixer_residual(yg_ref[...], x_ref[...], wout_ref, gmix_ref)
    x = _ffn_body(x, buf_ref, *ffn_w, nbuf_ref, ext_ref, act_ref)
    x_out_ref[...] = x
    _kv_body(x, *kv_w, cos_ref, sin_ref, lat_ref, kpe_ref, kcat_ref)
    _q_body(x, *q_w, cos_ref, sin_ref, q_ref)


def _layer0_tail(yg, x, ffn_buf, cos, sin, w_out, g_mix, ffn_w, kv_w, q_w, tm):
    b, t, _ = x.shape
    grid = (b, t // tm)
    row = lambda wd: pl.BlockSpec((None, tm, wd), lambda i, j: (i, j, 0))
    buf_spec = pl.BlockSpec((None, FFN_CONV - 1, 2 * D_FF), lambda i, j: (i, 0, 0))
    tab = pl.BlockSpec((tm, LANE), lambda i, j: (j, 0))
    weight_shapes = ((D_INNER, D_MODEL), (1, D_MODEL)) + FFN_WEIGHT_SPECS + KV_WEIGHT_SPECS + Q_WEIGHT_SPECS
    return pl.pallas_call(
        _layer0_tail_kernel,
        grid=grid,
        in_specs=[row(D_INNER), row(D_MODEL), buf_spec, tab, tab] + [_const_spec(s) for s in weight_shapes],
        out_specs=[
            row(D_MODEL), buf_spec, row(KV_RANK), row(QK_ROPE), row(QK_CAT),
            pl.BlockSpec((None, MLA_HEADS, tm, QK_CAT), lambda i, j: (i, 0, j, 0)),
        ],
        out_shape=[
            jax.ShapeDtypeStruct((b, t, D_MODEL), F32),
            jax.ShapeDtypeStruct((b, FFN_CONV - 1, 2 * D_FF), F32),
            jax.ShapeDtypeStruct((b, t, KV_RANK), F32),
            jax.ShapeDtypeStruct((b, t, QK_ROPE), F32),
            jax.ShapeDtypeStruct((b, t, QK_CAT), BF16),
            jax.ShapeDtypeStruct((b, MLA_HEADS, t, QK_CAT), BF16),
        ],
        scratch_shapes=_ffn_scratch(tm),
        compiler_params=_params("parallel", "arbitrary"),
        name="layer0_tail",
    )(yg, x, ffn_buf, cos, sin, w_out, g_mix, *ffn_w, *kv_w, *q_w)


def _layer1_tail_kernel(*refs):
    o_ref, x_ref, buf_ref, wuv_ref, wo_ref, gmix_ref = refs[:6]
    ffn_w = refs[6:12]
    x_out_ref, nbuf_ref, ext_ref, act_ref = refs[12:]
    x = _mixer_residual(_mla_values(o_ref, wuv_ref), x_ref[...], wo_ref, gmix_ref)
    x_out_ref[...] = _ffn_body(x, buf_ref, *ffn_w, nbuf_ref, ext_ref, act_ref)


def _layer1_tail(o_lat, x, ffn_buf, wuv, wo, g_mix, ffn_w, tm):
    b, t, _ = x.shape
    grid = (b, t // tm)
    row = lambda wd: pl.BlockSpec((None, tm, wd), lambda i, j: (i, j, 0))
    buf_spec = pl.BlockSpec((None, FFN_CONV - 1, 2 * D_FF), lambda i, j: (i, 0, 0))
    weight_shapes = ((MLA_HEADS, KV_RANK, V_DIM), (MLA_HEADS * V_DIM, D_MODEL), (1, D_MODEL)) + FFN_WEIGHT_SPECS
    return pl.pallas_call(
        _layer1_tail_kernel,
        grid=grid,
        in_specs=[pl.BlockSpec((None, MLA_HEADS, tm, KV_RANK), lambda i, j: (i, 0, j, 0)), row(D_MODEL), buf_spec]
        + [_const_spec(s) for s in weight_shapes],
        out_specs=[row(D_MODEL), buf_spec],
        out_shape=[
            jax.ShapeDtypeStruct((b, t, D_MODEL), F32),
            jax.ShapeDtypeStruct((b, FFN_CONV - 1, 2 * D_FF), F32),
        ],
        scratch_shapes=_ffn_scratch(tm),
        compiler_params=_params("parallel", "arbitrary"),
        name="layer1_tail",
    )(o_lat, x, ffn_buf, wuv, wo, g_mix, *ffn_w)


def _attn_kernel(q_ref, k_ref, vt_ref, o_ref, m_ref, l_ref, a_ref, acc_ref, s_ref, p_ref, *, tq, tk, past, nkv):
    qi = pl.program_id(1)
    rows = MLA_HEADS * tq
    nblk = rows // LANE
    m_ref[...] = jnp.full(m_ref.shape, -jnp.inf, F32)
    l_ref[...] = jnp.zeros(l_ref.shape, F32)
    acc_ref[...] = jnp.zeros(acc_ref.shape, F32)
    n_vis = jnp.minimum(nkv, (past + (qi + 1) * tq + tk - 1) // tk)

    def scores(j):
        q = q_ref[...].reshape(rows, QK_CAT)
        ks = k_ref[pl.ds(pl.multiple_of(j * tk, tk), tk), :]
        return lax.dot_general(ks, q, NT_DIMS, preferred_element_type=F32)

    def hidden_bias(j):
        key_chunk = lax.broadcasted_iota(jnp.int32, (tk, LANE), 0) // CHUNK + j * (tk // CHUNK)
        lane = lax.broadcasted_iota(jnp.int32, (tk, LANE), 1)
        pats = []
        for ph in range(max(1, tq // LANE)):
            q_chunk = (past + qi * tq) // CHUNK + ((ph * LANE + lane) % tq) // CHUNK
            pats.append(jnp.where(key_chunk <= q_chunk, 0.0, -jnp.inf))
        return pats

    def absorb(j, masked):
        slot = j % 2
        pats = hidden_bias(j) if masked else None
        for b in range(nblk):
            csl = slice(b * LANE, (b + 1) * LANE)
            s = s_ref[slot, :, csl]
            if masked:
                s = s + pats[b % len(pats)]
            m_prev = m_ref[:, csl]
            m_new = jnp.maximum(m_prev, jnp.max(s, axis=0, keepdims=True))
            alpha = jnp.exp2(m_prev - m_new)
            p = jnp.exp2(s - m_new)
            l_ref[:, csl] = alpha * l_ref[:, csl] + jnp.sum(p, axis=0, keepdims=True)
            m_ref[:, csl] = m_new
            a_ref[:, csl] = alpha
            p_ref[:, csl] = p.astype(BF16)
        pv = jnp.dot(vt_ref[j], p_ref[...], preferred_element_type=F32)
        for b in range(nblk):
            csl = slice(b * LANE, (b + 1) * LANE)
            acc_ref[:, csl] = a_ref[:, csl] * acc_ref[:, csl] + pv[:, csl]

    s_ref[0] = scores(0)

    def body(j, carry):
        s_new = scores(j)
        absorb(j - 1, masked=False)
        s_ref[j % 2] = s_new
        return carry

    lax.fori_loop(1, n_vis, body, 0)
    absorb(n_vis - 1, masked=True)
    o = (acc_ref[...] / l_ref[...]).T
    o_ref[...] = o.astype(BF16).reshape(MLA_HEADS, tq, KV_RANK)


def _attention(q, kcat, vt, past, tq, tk):
    b, _, t, _ = q.shape
    s_len = kcat.shape[1]
    nkv = s_len // tk
    assert tk % tq == 0 and past % tk == 0 and t % tq == 0 and tq % CHUNK == 0
    grid = (b, t // tq)
    rows = MLA_HEADS * tq
    return pl.pallas_call(
        functools.partial(_attn_kernel, tq=tq, tk=tk, past=past, nkv=nkv),
        grid=grid,
        in_specs=[
            pl.BlockSpec((None, MLA_HEADS, tq, QK_CAT), lambda i, j: (i, 0, j, 0)),
            pl.BlockSpec((None, s_len, QK_CAT), lambda i, j: (i, 0, 0)),
            pl.BlockSpec((None, nkv, KV_RANK, tk), lambda i, j: (i, 0, 0, 0)),
        ],
        out_specs=pl.BlockSpec((None, MLA_HEADS, tq, KV_RANK), lambda i, j: (i, 0, j, 0)),
        out_shape=jax.ShapeDtypeStruct((b, MLA_HEADS, t, KV_RANK), BF16),
        scratch_shapes=[
            pltpu.VMEM((1, rows), F32),
            pltpu.VMEM((1, rows), F32),
            pltpu.VMEM((1, rows), F32),
            pltpu.VMEM((KV_RANK, rows), F32),
            pltpu.VMEM((2, tk, rows), F32),
            pltpu.VMEM((tk, rows), BF16),
        ],
        compiler_params=_params("parallel", "parallel"),
        name="mla_attention",
    )(q, kcat, vt)


def _rope_tables(past, t):
    half = QK_ROPE // 2
    inv = jnp.exp(-math.log(ROPE_THETA) * jnp.arange(half, dtype=F32) / half)
    ang = (past + jnp.arange(t, dtype=jnp.int32)).astype(F32)[:, None] * inv[None, :]
    reps = 2 * LANE // QK_ROPE
    return jnp.tile(jnp.cos(ang), (1, reps)), jnp.tile(jnp.sin(ang), (1, reps))


def _rotate_half_cols(w):
    shp = w.shape
    w4 = w.reshape(shp[:-1] + (shp[-1] // QK_ROPE, 2, QK_ROPE // 2))
    return jnp.concatenate([-w4[..., 1:2, :], w4[..., 0:1, :]], axis=-2).reshape(shp)


def _tile(t, pref):
    return pref if t % pref == 0 else t


def _prep_weights(norm_mix_pre, norm_mix_post, norm_ffn_pre, norm_ffn_post,
                  ssm_w_in, ssm_conv_w, ssm_conv_b, ssm_dt_bias, ssm_a_log, ssm_d, ssm_norm, ssm_w_out,
                  kv_norm_in, kv_w_dkv, kv_norm, kv_w_kr, kv_w_uk, kv_w_uv,
                  mla_w_dq, mla_q_norm, mla_w_uq, mla_w_o,
                  ffn_w_up, ffn_conv_w, ffn_conv_b, ffn_w_down):
    w = {}
    w_in = ssm_w_in[0]
    w["wzx"] = w_in[:, :D_INNER + CONV_DIM].astype(BF16)
    w_dt = w_in[:, D_INNER + CONV_DIM:]
    w["wdt"] = jnp.pad(w_dt, ((0, 0), (0, LANE - SSM_HEADS))).astype(BF16)
    w["wdtt"] = w_dt.T.astype(BF16)
    w["ssm_cw"] = ssm_conv_w[0]
    w["ssm_cb"] = ssm_conv_b[0][None, :]
    w["dtb"] = ssm_dt_bias[0][None, :]
    w["dtbt"] = ssm_dt_bias[0][:, None]
    a = -jnp.exp(ssm_a_log[0].astype(F32))
    w["arow"] = a[None, :]
    w["acol2"] = jnp.broadcast_to(a.reshape(SSM_HEADS // 2, 2, 1),
                                  (SSM_HEADS // 2, 2, CHUNK)).reshape(SSM_HEADS // 2, 2 * CHUNK)
    w["expand"] = jnp.repeat(jnp.eye(SSM_HEADS, dtype=BF16), SSM_HEADDIM, axis=1)
    w["dskip"] = jnp.repeat(ssm_d[0], SSM_HEADDIM)[None, :]
    w["gn"] = ssm_norm[0][None, :]
    w["w_out"] = ssm_w_out[0].astype(BF16)
    w["g_mix_pre"] = norm_mix_pre[:, None, :]
    w["g_mix_post"] = norm_mix_post[:, None, :]
    w["g_ffn_pre"] = norm_ffn_pre[:, None, :]
    w["g_ffn_post"] = norm_ffn_post[:, None, :]
    w["ffn_up"] = ffn_w_up.astype(BF16)
    w["ffn_cw"] = ffn_conv_w
    w["ffn_cb"] = ffn_conv_b[:, None, :]
    w["ffn_down"] = ffn_w_down.astype(BF16)
    w["kv_gin"] = kv_norm_in[None, :]
    w["kv_w"] = jnp.concatenate([kv_w_dkv, kv_w_kr, _rotate_half_cols(kv_w_kr)], axis=-1).astype(BF16)
    w["kv_g"] = kv_norm[None, :]
    w["wdq"] = mla_w_dq[0].astype(BF16)
    w["gq"] = mla_q_norm[0][None, :]
    wuq = mla_w_uq[0].reshape(Q_RANK, MLA_HEADS, QK_NOPE + QK_ROPE)
    w["wq_nope"] = wuq[:, :, :QK_NOPE].reshape(Q_RANK, MLA_HEADS * QK_NOPE).astype(BF16)
    wq_pe = wuq[:, :, QK_NOPE:].reshape(Q_RANK, MLA_HEADS * QK_ROPE)
    w["wq_pe"] = wq_pe.astype(BF16)
    w["wq_pe_rot"] = _rotate_half_cols(wq_pe).astype(BF16)
    w["wuk_t"] = jnp.transpose(kv_w_uk, (1, 2, 0)).astype(BF16)
    w["wuv"] = jnp.transpose(kv_w_uv, (1, 0, 2)).astype(BF16)
    w["wo"] = mla_w_o[0].astype(BF16)
    return w


def _trunk(x, ssm_conv_buf, ssm_state, ffn_buf, past_kcat, w):
    b, t, _ = x.shape
    past = 0 if past_kcat is None else past_kcat.shape[1]
    assert t % CHUNK == 0 and past % CHUNK == 0
    tm = _tile(t, 256)

    z, xbc, dt, dtt, new_conv = _mamba_in(x, ssm_conv_buf, w["g_mix_pre"][0], w["wzx"], w["wdt"], w["wdtt"],
                                          w["ssm_cw"], w["ssm_cb"], w["dtb"], w["dtbt"], tm)
    nc = _tile(t, 256) // CHUNK
    dtt2 = dtt.reshape(b, SSM_HEADS // 2, 2, t // CHUNK, CHUNK).transpose(0, 3, 1, 2, 4)
    dtt2 = dtt2.reshape(b, t // CHUNK, SSM_HEADS // 2, 2 * CHUNK)
    h0t = ssm_state.reshape(b, D_INNER, D_STATE).transpose(0, 2, 1)
    yg, new_state_t = _ssd(xbc, z, dt, dtt2, w["arow"], w["acol2"], w["expand"], w["dskip"], w["gn"], h0t, nc)
    new_state = new_state_t.transpose(0, 2, 1).reshape(b, SSM_HEADS, SSM_HEADDIM, D_STATE)

    ffn_w = lambda i: (w["g_ffn_pre"][i], w["ffn_up"][i], w["ffn_cw"][i], w["ffn_cb"][i], w["ffn_down"][i],
                       w["g_ffn_post"][i])
    cos, sin = _rope_tables(past, t)
    x, new_ffn0, lat, kpe, kcat, q = _layer0_tail(
        yg, x, ffn_buf[0], cos, sin, w["w_out"], w["g_mix_post"][0], ffn_w(0),
        (w["kv_gin"], w["kv_w"], w["kv_g"]),
        (w["g_mix_pre"][1], w["wdq"], w["gq"], w["wq_nope"], w["wq_pe"], w["wq_pe_rot"], w["wuk_t"]), tm)
    if past_kcat is not None:
        kcat = jnp.concatenate([past_kcat, kcat], axis=1)

    s_len = past + t
    s_pad = -(-s_len // ATT_TK) * ATT_TK
    if s_pad != s_len:
        kcat = jnp.pad(kcat, ((0, 0), (0, s_pad - s_len), (0, 0)))
    vt = kcat[:, :, :KV_RANK].reshape(b, s_pad // ATT_TK, ATT_TK, KV_RANK).transpose(0, 1, 3, 2)
    o_lat = _attention(q, kcat, vt, past, _tile(t, ATT_TK), ATT_TK)
    x, new_ffn1 = _layer1_tail(o_lat, x, ffn_buf[1], w["wuv"], w["wo"], w["g_mix_post"][1], ffn_w(1), tm)
    return (x, new_state[None], new_conv[None], jnp.stack([new_ffn0, new_ffn1]), lat, kpe)


def kernel(x_prompt, x_sample, state_ssm, state_ssm_conv, state_ffn_conv, cache_kv_latent, cache_k_rope, norm_mix_pre, norm_mix_post, norm_ffn_pre, norm_ffn_post, ssm_w_in, ssm_conv_w, ssm_conv_b, ssm_dt_bias, ssm_a_log, ssm_d, ssm_norm, ssm_w_out, kv_norm_in, kv_w_dkv, kv_norm, kv_w_kr, kv_w_uk, kv_w_uv, mla_w_dq, mla_q_norm, mla_w_uq, mla_w_o, ffn_w_up, ffn_conv_w, ffn_conv_b, ffn_w_down):
    w = _prep_weights(norm_mix_pre, norm_mix_post, norm_ffn_pre, norm_ffn_post,
                      ssm_w_in, ssm_conv_w, ssm_conv_b, ssm_dt_bias, ssm_a_log, ssm_d, ssm_norm, ssm_w_out,
                      kv_norm_in, kv_w_dkv, kv_norm, kv_w_kr, kv_w_uk, kv_w_uv,
                      mla_w_dq, mla_q_norm, mla_w_uq, mla_w_o,
                      ffn_w_up, ffn_conv_w, ffn_conv_b, ffn_w_down)
    bp = x_prompt.shape[0]
    dtp = x_prompt.dtype
    y_p, p_ssm, p_conv, p_ffn, p_lat, p_kpe = _trunk(
        x_prompt,
        jnp.zeros((bp, SSM_CONV - 1, CONV_DIM), dtp),
        jnp.zeros((bp, SSM_HEADS, SSM_HEADDIM, D_STATE), dtp),
        jnp.zeros((2, bp, FFN_CONV - 1, 2 * D_FF), dtp),
        None, w)
    past_kcat = jnp.concatenate([cache_kv_latent, cache_k_rope], axis=-1).astype(BF16)
    y_s, s_ssm, s_conv, s_ffn, s_lat, s_kpe = _trunk(
        x_sample, state_ssm_conv[0], state_ssm[0], state_ffn_conv, past_kcat, w)
    return (y_p, y_s, p_ssm, p_conv, p_ffn, p_lat, p_kpe, s_ssm, s_conv, s_ffn, s_lat, s_kpe)
```

```python
import functools
import math

import jax
import jax.numpy as jnp
from jax import lax
from jax.experimental import pallas as pl
from jax.experimental.pallas import tpu as pltpu

F32 = jnp.float32
BF16 = jnp.bfloat16

D_MODEL = 1024
CHUNK = 64
EPS = 1e-6
D_INNER = 2048
SSM_HEADDIM = 64
SSM_HEADS = 32
SSM_GROUPS = 4
SSM_HPG = 8
D_STATE = 128
SSM_CONV = 4
GROUP_W = D_INNER // SSM_GROUPS
BC_W = SSM_GROUPS * D_STATE
CONV_DIM = D_INNER + 2 * BC_W
MLA_HEADS = 16
Q_RANK = 384
KV_RANK = 256
QK_NOPE = 128
QK_ROPE = 64
V_DIM = 128
QK_CAT = KV_RANK + QK_ROPE
ROPE_THETA = 10000.0
D_FF = 2816
FFN_CONV = 3
LANE = 128
CARRY_ROWS = 8
TOKEN_TILE = 256
COL_CHUNK = 512
FFN_CHUNK = 256
SSD_UNROLL = 4
ATT_TK = 256
LOG2E = 1.4426950408889634
VMEM_LIMIT = 56 * 1024 * 1024

NT_DIMS = (((1,), (1,)), ((), ()))
TN_DIMS = (((0,), (0,)), ((), ()))


def _rms(x, g):
    return x * lax.rsqrt(jnp.mean(x * x, axis=-1, keepdims=True) + EPS) * g


def _silu(x):
    return x * (1.0 / (1.0 + jnp.exp(-x)))


def _softplus(x):
    return jnp.maximum(x, 0.0) + jnp.log(1.0 + jnp.exp(-jnp.abs(x)))


def _const_spec(shape):
    zeros = (0,) * len(shape)
    return pl.BlockSpec(shape, lambda *_: zeros, pipeline_mode=pl.Buffered(1))


def _params(*sem):
    return pltpu.CompilerParams(dimension_semantics=sem, vmem_limit_bytes=VMEM_LIMIT)


def _init_carry(carry_ref, hist_ref, first_step):
    nseg, hist, _ = hist_ref.shape

    @pl.when(first_step)
    def _():
        carry_ref[...] = jnp.zeros(carry_ref.shape, F32)
        for s in range(nseg):
            carry_ref[(s + 1) * CARRY_ROWS - hist:(s + 1) * CARRY_ROWS, :] = hist_ref[s]


def _store_history(nbuf_ref, carry_ref):
    nseg, hist, _ = nbuf_ref.shape
    for s in range(nseg):
        nbuf_ref[s] = carry_ref[(s + 1) * CARRY_ROWS - hist:(s + 1) * CARRY_ROWS, :]


def _causal_conv(raw, carry_ref, cols, w_ref, b_ref, taps):
    nseg = carry_ref.shape[0] // CARRY_ROWS
    seg_len = raw.shape[0] // nseg
    hist = taps - 1
    outs = []
    for s in range(nseg):
        cur = raw[s * seg_len:(s + 1) * seg_len, :]
        prev = carry_ref[s * CARRY_ROWS:(s + 1) * CARRY_ROWS, cols]
        ext = jnp.concatenate([prev, cur], axis=0)
        acc = b_ref[:, cols] + cur * w_ref[hist:hist + 1, cols]
        for k in range(1, taps):
            acc = acc + ext[CARRY_ROWS - k:CARRY_ROWS - k + seg_len, :] * w_ref[hist - k:hist - k + 1, cols]
        carry_ref[s * CARRY_ROWS:(s + 1) * CARRY_ROWS, cols] = cur[seg_len - CARRY_ROWS:, :]
        outs.append(acc)
    return outs[0] if nseg == 1 else jnp.concatenate(outs, axis=0)


def _mamba_in_kernel(x_ref, buf_ref, g_ref, wzx_ref, wdt_ref, wdtt_ref, cw_ref, cb_ref, dtb_ref, dtbt_ref,
                     z_ref, xbc_ref, dt_ref, dtt_ref, nbuf_ref, ext_ref):
    _init_carry(ext_ref, buf_ref, pl.program_id(1) == 0)
    h = _rms(x_ref[...], g_ref[...]).astype(BF16)
    for j in range(D_INNER // COL_CHUNK):
        cols = slice(j * COL_CHUNK, (j + 1) * COL_CHUNK)
        z_ref[:, cols] = jnp.dot(h, wzx_ref[:, cols], preferred_element_type=F32).astype(BF16)
    for j in range(CONV_DIM // COL_CHUNK):
        cols = slice(j * COL_CHUNK, (j + 1) * COL_CHUNK)
        wcols = slice(D_INNER + j * COL_CHUNK, D_INNER + (j + 1) * COL_CHUNK)
        raw = jnp.dot(h, wzx_ref[:, wcols], preferred_element_type=F32)
        xbc_ref[:, cols] = _silu(_causal_conv(raw, ext_ref, cols, cw_ref, cb_ref, SSM_CONV)).astype(BF16)
    _store_history(nbuf_ref, ext_ref)
    dt = jnp.dot(h, wdt_ref[...], preferred_element_type=F32)[:, :SSM_HEADS]
    dt_ref[...] = _softplus(dt + dtb_ref[...])
    dtt = lax.dot_general(wdtt_ref[...], h, NT_DIMS, preferred_element_type=F32)
    dtt_ref[...] = _softplus(dtt + dtbt_ref[...])


def _mamba_in(x, conv_buf, g, wzx, wdt, wdtt, cw, cb, dtb, dtbt, tm):
    b, t, _ = x.shape
    nseg = conv_buf.shape[1]
    grid = (b, t // tm)
    row = lambda w: pl.BlockSpec((None, tm, w), lambda i, j: (i, j, 0))
    return pl.pallas_call(
        _mamba_in_kernel,
        grid=grid,
        in_specs=[
            row(D_MODEL),
            pl.BlockSpec((None, nseg, SSM_CONV - 1, CONV_DIM), lambda i, j: (i, 0, 0, 0)),
            _const_spec((1, D_MODEL)),
            _const_spec((D_MODEL, D_INNER + CONV_DIM)),
            _const_spec((D_MODEL, LANE)),
            _const_spec((SSM_HEADS, D_MODEL)),
            _const_spec((SSM_CONV, CONV_DIM)),
            _const_spec((1, CONV_DIM)),
            _const_spec((1, SSM_HEADS)),
            _const_spec((SSM_HEADS, 1)),
        ],
        out_specs=[
            row(D_INNER),
            row(CONV_DIM),
            row(SSM_HEADS),
            pl.BlockSpec((None, SSM_HEADS, tm), lambda i, j: (i, 0, j)),
            pl.BlockSpec((None, nseg, SSM_CONV - 1, CONV_DIM), lambda i, j: (i, 0, 0, 0)),
        ],
        out_shape=[
            jax.ShapeDtypeStruct((b, t, D_INNER), BF16),
            jax.ShapeDtypeStruct((b, t, CONV_DIM), BF16),
            jax.ShapeDtypeStruct((b, t, SSM_HEADS), F32),
            jax.ShapeDtypeStruct((b, SSM_HEADS, t), F32),
            jax.ShapeDtypeStruct((b, nseg, SSM_CONV - 1, CONV_DIM), F32),
        ],
        scratch_shapes=[pltpu.VMEM((nseg * CARRY_ROWS, CONV_DIM), F32)],
        compiler_params=_params("parallel", "arbitrary"),
        name="mamba_in",
    )(x, conv_buf, g, wzx, wdt, wdtt, cw, cb, dtb, dtbt)


def _ssd_kernel(xs_ref, bm_ref, cm_ref, z_ref, dt_ref, dtt_ref, arow_ref, acol_ref, exp_ref, dskip_ref, gn_ref,
                h0_ref, yg_ref, hout_ref, st_ref, *, nc):
    step = pl.program_id(1)

    @pl.when(step == 0)
    def _():
        st_ref[...] = h0_ref[...].T

    li = lax.broadcasted_iota(jnp.int32, (CHUNK, CHUNK), 0)
    si = lax.broadcasted_iota(jnp.int32, (CHUNK, CHUNK), 1)
    tril = (si <= li).astype(F32)
    pi = lax.broadcasted_iota(jnp.int32, (LANE, LANE), 0)
    pj = lax.broadcasted_iota(jnp.int32, (LANE, LANE), 1)
    same_head = (pi // CHUNK) == (pj // CHUNK)
    triu2 = jnp.where(same_head, (pi <= pj).astype(F32), 0.0)
    l2 = lax.broadcasted_iota(jnp.int32, (CHUNK, LANE), 0)
    s2 = lax.broadcasted_iota(jnp.int32, (CHUNK, LANE), 1) % CHUNK
    causal2 = s2 <= l2
    expand = exp_ref[...]

    gsls = [slice(g * GROUP_W, (g + 1) * GROUP_W) for g in range(SSM_GROUPS)]

    def decays(c):
        rows = pl.ds(pl.multiple_of(c * CHUNK, CHUNK), CHUNK)
        dt = dt_ref[rows, :]
        dtt2 = dtt_ref[c]
        acs = jnp.dot(tril, dt * arow_ref[...], precision=lax.Precision.HIGHEST,
                      preferred_element_type=F32)
        acst2 = jnp.dot(dtt2 * acol_ref[...], triu2, precision=lax.Precision.HIGHEST,
                        preferred_element_type=F32)
        last = acs[CHUNK - 1:CHUNK, :]
        e_acs = jnp.exp(acs).astype(BF16)
        wst = (jnp.exp(last - acs) * dt).astype(BF16)
        hi = acs.astype(BF16)
        r1 = acs - hi.astype(F32)
        mid = r1.astype(BF16)
        lo = (r1 - mid.astype(F32)).astype(BF16)
        acs_w = (jnp.dot(hi, expand, preferred_element_type=F32) + jnp.dot(mid, expand, preferred_element_type=F32)
                 + jnp.dot(lo, expand, preferred_element_type=F32))
        e_acs_w = jnp.dot(e_acs, expand, preferred_element_type=F32)
        wst_w = jnp.dot(wst, expand, preferred_element_type=F32)
        decay_w = jnp.exp(acs_w[CHUNK - 1:CHUNK, :])
        return dtt2, acst2, acs_w, e_acs_w, wst_w, decay_w

    def through_state(c, wst_w, decay_w):
        rows = pl.ds(pl.multiple_of(c * CHUNK, CHUNK), CHUNK)
        cbs, yoffs = [], []
        for g, gsl in enumerate(gsls):
            bg = bm_ref[rows, g * D_STATE:(g + 1) * D_STATE]
            cg = cm_ref[rows, g * D_STATE:(g + 1) * D_STATE]
            cbs.append(lax.dot_general(cg, bg, NT_DIMS, preferred_element_type=F32))
            stg = st_ref[:, gsl]
            yoffs.append(jnp.dot(cg, stg.astype(BF16), preferred_element_type=F32))
            xw = (xs_ref[rows, gsl].astype(F32) * wst_w[:, gsl]).astype(BF16)
            st_ref[:, gsl] = stg * decay_w[:, gsl] + lax.dot_general(bg, xw, TN_DIMS, preferred_element_type=F32)
        return cbs, yoffs

    def within_chunk(c, dtt2, acst2, acs_w, e_acs_w, cbs, yoffs):
        rows = pl.ds(pl.multiple_of(c * CHUNK, CHUNK), CHUNK)
        for g, gsl in enumerate(gsls):
            xg = xs_ref[rows, gsl]
            cb2 = jnp.concatenate([cbs[g], cbs[g]], axis=1)
            ys = []
            for jj in range(SSM_HPG // 2):
                j = g * (SSM_HPG // 2) + jj
                seg = acs_w[:, j * LANE:(j + 1) * LANE] - acst2[j:j + 1, :]
                wts = cb2 * jnp.exp(jnp.where(causal2, seg, -jnp.inf)) * dtt2[j:j + 1, :]
                xp = xg[:, jj * LANE:(jj + 1) * LANE]
                xbd = jnp.where(same_head, jnp.concatenate([xp, xp], axis=0), jnp.zeros((), BF16))
                ys.append(jnp.dot(wts.astype(BF16), xbd, preferred_element_type=F32))
            y = jnp.concatenate(ys, axis=1)
            y = y + yoffs[g] * e_acs_w[:, gsl] + xg.astype(F32) * dskip_ref[:, gsl]
            yg = y * _silu(z_ref[rows, gsl].astype(F32))
            yg_ref[rows, gsl] = _rms(yg, gn_ref[:, gsl]).astype(BF16)

    unroll = min(nc, SSD_UNROLL)

    def body(i, carry):
        cs = [i * unroll + k for k in range(unroll)]
        pre = [decays(c) for c in cs]
        mid = [through_state(c, p[4], p[5]) for c, p in zip(cs, pre)]
        for c, p, m in zip(cs, pre, mid):
            within_chunk(c, p[0], p[1], p[2], p[3], m[0], m[1])
        return carry

    lax.fori_loop(0, nc // unroll, body, 0)

    @pl.when(step == pl.num_programs(1) - 1)
    def _():
        hout_ref[...] = st_ref[...].T


def _ssd(xbc, z, dt, dtt2, arow, acol2, expand, dskip, gn, h0t, nc):
    b, t, _ = xbc.shape
    lt = nc * CHUNK
    grid = (b, t // lt)
    state_spec = pl.BlockSpec((None, D_INNER, D_STATE), lambda i, j: (i, 0, 0))
    return pl.pallas_call(
        functools.partial(_ssd_kernel, nc=nc),
        grid=grid,
        in_specs=[
            pl.BlockSpec((None, lt, D_INNER), lambda i, j: (i, j, 0)),
            pl.BlockSpec((None, lt, BC_W), lambda i, j: (i, j, D_INNER // BC_W)),
            pl.BlockSpec((None, lt, BC_W), lambda i, j: (i, j, D_INNER // BC_W + 1)),
            pl.BlockSpec((None, lt, D_INNER), lambda i, j: (i, j, 0)),
            pl.BlockSpec((None, lt, SSM_HEADS), lambda i, j: (i, j, 0)),
            pl.BlockSpec((None, nc, SSM_HEADS // 2, 2 * CHUNK), lambda i, j: (i, j, 0, 0)),
            _const_spec((1, SSM_HEADS)),
            _const_spec((SSM_HEADS // 2, 2 * CHUNK)),
            _const_spec((SSM_HEADS, D_INNER)),
            _const_spec((1, D_INNER)),
            _const_spec((1, D_INNER)),
            state_spec,
        ],
        out_specs=[
            pl.BlockSpec((None, lt, D_INNER), lambda i, j: (i, j, 0)),
            state_spec,
        ],
        out_shape=[
            jax.ShapeDtypeStruct((b, t, D_INNER), BF16),
            jax.ShapeDtypeStruct((b, D_INNER, D_STATE), F32),
        ],
        scratch_shapes=[pltpu.VMEM((D_STATE, D_INNER), F32)],
        compiler_params=_params("parallel", "arbitrary"),
        name="ssd_scan",
    )(xbc, xbc, xbc, z, dt, dtt2, arow, acol2, expand, dskip, gn, h0t)


def _mixer_residual(mix_bf16, x, w_ref, g_ref):
    return x + _rms(jnp.dot(mix_bf16, w_ref[...], preferred_element_type=F32), g_ref[...])


def _mla_values(o_ref, wuv_ref):
    parts = []
    for h in range(MLA_HEADS):
        parts.append(jnp.dot(o_ref[h], wuv_ref[h], preferred_element_type=F32).astype(BF16))
    return jnp.concatenate(parts, axis=-1)


FFN_WEIGHT_SPECS = (
    (1, D_MODEL), (D_MODEL, 2 * D_FF), (FFN_CONV, 2 * D_FF), (1, 2 * D_FF), (D_FF, D_MODEL), (1, D_MODEL))


def _ffn_body(x, buf_ref, gpre_ref, wup_ref, cw_ref, cb_ref, wdn_ref, gpost_ref, nbuf_ref, ext_ref, act_ref):
    _init_carry(ext_ref, buf_ref, pl.program_id(1) == 0)
    h = _rms(x, gpre_ref[...]).astype(BF16)
    for j in range(D_FF // FFN_CHUNK):
        vcols = slice(j * FFN_CHUNK, (j + 1) * FFN_CHUNK)
        gcols = slice(D_FF + j * FFN_CHUNK, D_FF + (j + 1) * FFN_CHUNK)
        val = _causal_conv(jnp.dot(h, wup_ref[:, vcols], preferred_element_type=F32),
                           ext_ref, vcols, cw_ref, cb_ref, FFN_CONV)
        gate = _causal_conv(jnp.dot(h, wup_ref[:, gcols], preferred_element_type=F32),
                            ext_ref, gcols, cw_ref, cb_ref, FFN_CONV)
        act_ref[:, vcols] = (jax.nn.gelu(gate, approximate=True) * val).astype(BF16)
    _store_history(nbuf_ref, ext_ref)
    f = jnp.dot(act_ref[...], wdn_ref[...], preferred_element_type=F32)
    return x + _rms(f, gpost_ref[...])


def _ffn_scratch(tm, nseg):
    return [pltpu.VMEM((nseg * CARRY_ROWS, 2 * D_FF), F32), pltpu.VMEM((tm, D_FF), BF16)]


KV_WEIGHT_SPECS = ((1, D_MODEL), (D_MODEL, KV_RANK + 2 * QK_ROPE), (1, KV_RANK))


def _kv_body(x, gin_ref, w_ref, gkv_ref, cos_ref, sin_ref, lat_ref, kpe_ref, kcat_ref):
    h = _rms(x, gin_ref[...]).astype(BF16)
    hw = jnp.dot(h, w_ref[...], preferred_element_type=F32)
    lat = _rms(hw[:, :KV_RANK], gkv_ref[...])
    kpe = hw[:, KV_RANK:QK_CAT] * cos_ref[:, :QK_ROPE] + hw[:, QK_CAT:] * sin_ref[:, :QK_ROPE]
    lat_ref[...] = lat
    kpe_ref[...] = kpe
    kcat_ref[:, :KV_RANK] = lat.astype(BF16)
    kcat_ref[:, KV_RANK:] = kpe.astype(BF16)


Q_SCALE = (QK_NOPE + QK_ROPE) ** -0.5 * LOG2E
Q_WEIGHT_SPECS = (
    (1, D_MODEL), (D_MODEL, Q_RANK), (1, Q_RANK), (Q_RANK, MLA_HEADS * QK_NOPE), (Q_RANK, MLA_HEADS * QK_ROPE),
    (Q_RANK, MLA_HEADS * QK_ROPE), (MLA_HEADS, QK_NOPE, KV_RANK))


def _q_body(x, gpre_ref, wdq_ref, gq_ref, wn_ref, wp_ref, wpr_ref, wuk_ref, cos_ref, sin_ref, q_ref):
    scale = Q_SCALE
    h = _rms(x, gpre_ref[...]).astype(BF16)
    cq = _rms(jnp.dot(h, wdq_ref[...], preferred_element_type=F32), gq_ref[...]).astype(BF16)
    qn = jnp.dot(cq, wn_ref[...], preferred_element_type=F32).astype(BF16)
    qp = jnp.dot(cq, wp_ref[...], preferred_element_type=F32)
    qpr = jnp.dot(cq, wpr_ref[...], preferred_element_type=F32)
    reps = MLA_HEADS * QK_ROPE // LANE
    cos = jnp.concatenate([cos_ref[...]] * reps, axis=1)
    sin = jnp.concatenate([sin_ref[...]] * reps, axis=1)
    qpe = ((qp * cos + qpr * sin) * scale).astype(BF16)
    for hd in range(MLA_HEADS):
        ql = jnp.dot(qn[:, hd * QK_NOPE:(hd + 1) * QK_NOPE], wuk_ref[hd], preferred_element_type=F32)
        q_ref[hd, :, :KV_RANK] = (ql * scale).astype(BF16)
        q_ref[hd, :, KV_RANK:] = qpe[:, hd * QK_ROPE:(hd + 1) * QK_ROPE]


def _layer0_tail_kernel(*refs):
    yg_ref, x_ref, buf_ref, cos_ref, sin_ref, wout_ref, gmix_ref = refs[:7]
    ffn_w = refs[7:13]
    kv_w = refs[13:16]
    q_w = refs[16:23]
    x_out_ref, nbuf_ref, lat_ref, kpe_ref, kcat_ref, q_ref, ext_ref, act_ref = refs[23:]
    x = _mixer_residual(yg_ref[...], x_ref[...], wout_ref, gmix_ref)
    x = _ffn_body(x, buf_ref, *ffn_w, nbuf_ref, ext_ref, act_ref)
    x_out_ref[...] = x
    _kv_body(x, *kv_w, cos_ref, sin_ref, lat_ref, kpe_ref, kcat_ref)
    _q_body(x, *q_w, cos_ref, sin_ref, q_ref)


def _layer0_tail(yg, x, ffn_buf, cos, sin, w_out, g_mix, ffn_w, kv_w, q_w, tm):
    b, t, _ = x.shape
    grid = (b, t // tm)
    row = lambda wd: pl.BlockSpec((None, tm, wd), lambda i, j: (i, j, 0))
    nseg = ffn_buf.shape[1]
    buf_spec = pl.BlockSpec((None, nseg, FFN_CONV - 1, 2 * D_FF), lambda i, j: (i, 0, 0, 0))
    tab = pl.BlockSpec((tm, LANE), lambda i, j: (j, 0))
    weight_shapes = ((D_INNER, D_MODEL), (1, D_MODEL)) + FFN_WEIGHT_SPECS + KV_WEIGHT_SPECS + Q_WEIGHT_SPECS
    return pl.pallas_call(
        _layer0_tail_kernel,
        grid=grid,
        in_specs=[row(D_INNER), row(D_MODEL), buf_spec, tab, tab] + [_const_spec(s) for s in weight_shapes],
        out_specs=[
            row(D_MODEL), buf_spec, row(KV_RANK), row(QK_ROPE), row(QK_CAT),
            pl.BlockSpec((None, MLA_HEADS, tm, QK_CAT), lambda i, j: (i, 0, j, 0)),
        ],
        out_shape=[
            jax.ShapeDtypeStruct((b, t, D_MODEL), F32),
            jax.ShapeDtypeStruct((b, nseg, FFN_CONV - 1, 2 * D_FF), F32),
            jax.ShapeDtypeStruct((b, t, KV_RANK), F32),
            jax.ShapeDtypeStruct((b, t, QK_ROPE), F32),
            jax.ShapeDtypeStruct((b, t, QK_CAT), BF16),
            jax.ShapeDtypeStruct((b, MLA_HEADS, t, QK_CAT), BF16),
        ],
        scratch_shapes=_ffn_scratch(tm, nseg),
        compiler_params=_params("parallel", "arbitrary"),
        name="layer0_tail",
    )(yg, x, ffn_buf, cos, sin, w_out, g_mix, *ffn_w, *kv_w, *q_w)


def _layer1_tail_kernel(*refs):
    o_ref, x_ref, buf_ref, wuv_ref, wo_ref, gmix_ref = refs[:6]
    ffn_w = refs[6:12]
    x_out_ref, nbuf_ref, ext_ref, act_ref = refs[12:]
    x = _mixer_residual(_mla_values(o_ref, wuv_ref), x_ref[...], wo_ref, gmix_ref)
    x_out_ref[...] = _ffn_body(x, buf_ref, *ffn_w, nbuf_ref, ext_ref, act_ref)


def _layer1_tail(o_lat, x, ffn_buf, wuv, wo, g_mix, ffn_w, tm):
    b, t, _ = x.shape
    grid = (b, t // tm)
    row = lambda wd: pl.BlockSpec((None, tm, wd), lambda i, j: (i, j, 0))
    nseg = ffn_buf.shape[1]
    buf_spec = pl.BlockSpec((None, nseg, FFN_CONV - 1, 2 * D_FF), lambda i, j: (i, 0, 0, 0))
    weight_shapes = ((MLA_HEADS, KV_RANK, V_DIM), (MLA_HEADS * V_DIM, D_MODEL), (1, D_MODEL)) + FFN_WEIGHT_SPECS
    return pl.pallas_call(
        _layer1_tail_kernel,
        grid=grid,
        in_specs=[pl.BlockSpec((None, MLA_HEADS, tm, KV_RANK), lambda i, j: (i, 0, j, 0)), row(D_MODEL), buf_spec]
        + [_const_spec(s) for s in weight_shapes],
        out_specs=[row(D_MODEL), buf_spec],
        out_shape=[
            jax.ShapeDtypeStruct((b, t, D_MODEL), F32),
            jax.ShapeDtypeStruct((b, nseg, FFN_CONV - 1, 2 * D_FF), F32),
        ],
        scratch_shapes=_ffn_scratch(tm, nseg),
        compiler_params=_params("parallel", "arbitrary"),
        name="layer1_tail",
    )(o_lat, x, ffn_buf, wuv, wo, g_mix, *ffn_w)


def _attn_kernel(q_ref, k_ref, vt_ref, o_ref, m_ref, l_ref, a_ref, acc_ref, s_ref, p_ref, qt_ref, *, tq, tk, past,
                 nkv):
    qi = pl.program_id(1)
    rows = MLA_HEADS * tq
    nblk = rows // LANE
    m_ref[...] = jnp.full(m_ref.shape, -jnp.inf, F32)
    l_ref[...] = jnp.zeros(l_ref.shape, F32)
    acc_ref[...] = jnp.zeros(acc_ref.shape, F32)
    n_vis = jnp.minimum(nkv, (past + (qi + 1) * tq + tk - 1) // tk)

    qt_ref[...] = q_ref[...].reshape(rows, QK_CAT).astype(F32).T.astype(BF16)

    def scores(j):
        ks = k_ref[pl.ds(pl.multiple_of(j * tk, tk), tk), :]
        return jnp.dot(ks, qt_ref[...], preferred_element_type=F32)

    def hidden_bias(j):
        key_chunk = lax.broadcasted_iota(jnp.int32, (tk, LANE), 0) // CHUNK + j * (tk // CHUNK)
        lane = lax.broadcasted_iota(jnp.int32, (tk, LANE), 1)
        pats = []
        for ph in range(max(1, tq // LANE)):
            q_chunk = (past + qi * tq) // CHUNK + ((ph * LANE + lane) % tq) // CHUNK
            pats.append(jnp.where(key_chunk <= q_chunk, 0.0, -jnp.inf))
        return pats

    def absorb(j, masked):
        slot = j % 2
        pats = hidden_bias(j) if masked else None
        for b in range(nblk):
            csl = slice(b * LANE, (b + 1) * LANE)
            s = s_ref[slot, :, csl]
            if masked:
                s = s + pats[b % len(pats)]
            m_prev = m_ref[:, csl]
            m_new = jnp.maximum(m_prev, jnp.max(s, axis=0, keepdims=True))
            alpha = jnp.exp2(m_prev - m_new)
            p = jnp.exp2(s - m_new)
            l_ref[:, csl] = alpha * l_ref[:, csl] + jnp.sum(p, axis=0, keepdims=True)
            m_ref[:, csl] = m_new
            a_ref[:, csl] = alpha
            p_ref[:, csl] = p.astype(BF16)
        pv = jnp.dot(vt_ref[j], p_ref[...], preferred_element_type=F32)
        for b in range(nblk):
            csl = slice(b * LANE, (b + 1) * LANE)
            acc_ref[:, csl] = a_ref[:, csl] * acc_ref[:, csl] + pv[:, csl]

    s_ref[0] = scores(0)

    def body(j, carry):
        s_new = scores(j)
        absorb(j - 1, masked=False)
        s_ref[j % 2] = s_new
        return carry

    lax.fori_loop(1, n_vis, body, 0)
    absorb(n_vis - 1, masked=True)
    o = (acc_ref[...] / l_ref[...]).T
    o_ref[...] = o.astype(BF16).reshape(MLA_HEADS, tq, KV_RANK)


def _attention(q, kcat, vt, past, tq, tk):
    b, _, t, _ = q.shape
    s_len = kcat.shape[1]
    nkv = s_len // tk
    assert tk % tq == 0 and past % tk == 0 and t % tq == 0 and tq % CHUNK == 0
    grid = (b, t // tq)
    rows = MLA_HEADS * tq
    return pl.pallas_call(
        functools.partial(_attn_kernel, tq=tq, tk=tk, past=past, nkv=nkv),
        grid=grid,
        in_specs=[
            pl.BlockSpec((None, MLA_HEADS, tq, QK_CAT), lambda i, j: (i, 0, j, 0)),
            pl.BlockSpec((None, s_len, QK_CAT), lambda i, j: (i, 0, 0)),
            pl.BlockSpec((None, nkv, KV_RANK, tk), lambda i, j: (i, 0, 0, 0)),
        ],
        out_specs=pl.BlockSpec((None, MLA_HEADS, tq, KV_RANK), lambda i, j: (i, 0, j, 0)),
        out_shape=jax.ShapeDtypeStruct((b, MLA_HEADS, t, KV_RANK), BF16),
        scratch_shapes=[
            pltpu.VMEM((1, rows), F32),
            pltpu.VMEM((1, rows), F32),
            pltpu.VMEM((1, rows), F32),
            pltpu.VMEM((KV_RANK, rows), F32),
            pltpu.VMEM((2, tk, rows), F32),
            pltpu.VMEM((tk, rows), BF16),
            pltpu.VMEM((QK_CAT, rows), BF16),
        ],
        compiler_params=_params("parallel", "parallel"),
        name="mla_attention",
    )(q, kcat, vt)


def _rope_tables(past, t):
    half = QK_ROPE // 2
    inv = jnp.exp(-math.log(ROPE_THETA) * jnp.arange(half, dtype=F32) / half)
    ang = (past + jnp.arange(t, dtype=jnp.int32)).astype(F32)[:, None] * inv[None, :]
    reps = 2 * LANE // QK_ROPE
    return jnp.tile(jnp.cos(ang), (1, reps)), jnp.tile(jnp.sin(ang), (1, reps))


def _rotate_half_cols(w):
    shp = w.shape
    w4 = w.reshape(shp[:-1] + (shp[-1] // QK_ROPE, 2, QK_ROPE // 2))
    return jnp.concatenate([-w4[..., 1:2, :], w4[..., 0:1, :]], axis=-2).reshape(shp)


def _tile(t, pref):
    return pref if t % pref == 0 else t


def _prep_weights(norm_mix_pre, norm_mix_post, norm_ffn_pre, norm_ffn_post,
                  ssm_w_in, ssm_conv_w, ssm_conv_b, ssm_dt_bias, ssm_a_log, ssm_d, ssm_norm, ssm_w_out,
                  kv_norm_in, kv_w_dkv, kv_norm, kv_w_kr, kv_w_uk, kv_w_uv,
                  mla_w_dq, mla_q_norm, mla_w_uq, mla_w_o,
                  ffn_w_up, ffn_conv_w, ffn_conv_b, ffn_w_down):
    w = {}
    w_in = ssm_w_in[0]
    w["wzx"] = w_in[:, :D_INNER + CONV_DIM].astype(BF16)
    w_dt = w_in[:, D_INNER + CONV_DIM:]
    w["wdt"] = jnp.pad(w_dt, ((0, 0), (0, LANE - SSM_HEADS))).astype(BF16)
    w["wdtt"] = w_dt.T.astype(BF16)
    w["ssm_cw"] = ssm_conv_w[0]
    w["ssm_cb"] = ssm_conv_b[0][None, :]
    w["dtb"] = ssm_dt_bias[0][None, :]
    w["dtbt"] = ssm_dt_bias[0][:, None]
    a = -jnp.exp(ssm_a_log[0].astype(F32))
    w["arow"] = a[None, :]
    w["acol2"] = jnp.broadcast_to(a.reshape(SSM_HEADS // 2, 2, 1),
                                  (SSM_HEADS // 2, 2, CHUNK)).reshape(SSM_HEADS // 2, 2 * CHUNK)
    w["expand"] = jnp.repeat(jnp.eye(SSM_HEADS, dtype=BF16), SSM_HEADDIM, axis=1)
    w["dskip"] = jnp.repeat(ssm_d[0], SSM_HEADDIM)[None, :]
    w["gn"] = ssm_norm[0][None, :]
    w["w_out"] = ssm_w_out[0].astype(BF16)
    w["g_mix_pre"] = norm_mix_pre[:, None, :]
    w["g_mix_post"] = norm_mix_post[:, None, :]
    w["g_ffn_pre"] = norm_ffn_pre[:, None, :]
    w["g_ffn_post"] = norm_ffn_post[:, None, :]
    w["ffn_up"] = [ffn_w_up[i].astype(BF16) for i in range(2)]
    w["ffn_cw"] = ffn_conv_w
    w["ffn_cb"] = ffn_conv_b[:, None, :]
    w["ffn_down"] = [ffn_w_down[i].astype(BF16) for i in range(2)]
    w["kv_gin"] = kv_norm_in[None, :]
    w["kv_w"] = jnp.concatenate([kv_w_dkv, kv_w_kr, _rotate_half_cols(kv_w_kr)], axis=-1).astype(BF16)
    w["kv_g"] = kv_norm[None, :]
    w["wdq"] = mla_w_dq[0].astype(BF16)
    w["gq"] = mla_q_norm[0][None, :]
    wuq = mla_w_uq[0].reshape(Q_RANK, MLA_HEADS, QK_NOPE + QK_ROPE)
    w["wq_nope"] = wuq[:, :, :QK_NOPE].reshape(Q_RANK, MLA_HEADS * QK_NOPE).astype(BF16)
    wq_pe = wuq[:, :, QK_NOPE:].reshape(Q_RANK, MLA_HEADS * QK_ROPE)
    w["wq_pe"] = wq_pe.astype(BF16)
    w["wq_pe_rot"] = _rotate_half_cols(wq_pe).astype(BF16)
    w["wuk_t"] = jnp.transpose(kv_w_uk, (1, 2, 0)).astype(BF16)
    w["wuv"] = jnp.transpose(kv_w_uv, (1, 0, 2)).astype(BF16)
    w["wo"] = mla_w_o[0].astype(BF16)
    return w


def _trunk(x, ssm_conv_buf, ssm_state, ffn_buf, past_kcat, w):
    b, t, _ = x.shape
    past = 0 if past_kcat is None else past_kcat.shape[1]
    assert t % CHUNK == 0 and past % CHUNK == 0
    tm = TOKEN_TILE
    nseq = tm // t if t < tm and tm % t == 0 and b % (tm // t) == 0 else 1
    if t % tm != 0 and nseq == 1:
        tm = t
    bb, tt = b // nseq, nseq * t
    pack = lambda a: a.reshape((bb, tt) + a.shape[2:])
    unpack = lambda a: a.reshape((b, t) + a.shape[2:])
    pack_hist = lambda a: a.reshape((bb, nseq) + a.shape[1:])
    unpack_hist = lambda a: a.reshape((b,) + a.shape[2:])

    z, xbc, dt, dtt, new_conv = _mamba_in(pack(x), pack_hist(ssm_conv_buf), w["g_mix_pre"][0], w["wzx"], w["wdt"],
                                          w["wdtt"], w["ssm_cw"], w["ssm_cb"], w["dtb"], w["dtbt"], tm)
    nc = _tile(t, 256) // CHUNK
    dtt2 = dtt.reshape(bb, SSM_HEADS // 2, 2, tt // CHUNK, CHUNK).transpose(0, 3, 1, 2, 4)
    dtt2 = dtt2.reshape(b, t // CHUNK, SSM_HEADS // 2, 2 * CHUNK)
    yg, new_state = _ssd(unpack(xbc), unpack(z), unpack(dt), dtt2, w["arow"], w["acol2"], w["expand"], w["dskip"],
                         w["gn"], ssm_state.reshape(b, D_INNER, D_STATE), nc)
    new_state = new_state.reshape(b, SSM_HEADS, SSM_HEADDIM, D_STATE)

    ffn_w = lambda i: (w["g_ffn_pre"][i], w["ffn_up"][i], w["ffn_cw"][i], w["ffn_cb"][i], w["ffn_down"][i],
                       w["g_ffn_post"][i])
    cos, sin = _rope_tables(past, t)
    x, new_ffn0, lat, kpe, kcat, q = _layer0_tail(
        pack(yg), pack(x), pack_hist(ffn_buf[0]), jnp.tile(cos, (nseq, 1)), jnp.tile(sin, (nseq, 1)),
        w["w_out"], w["g_mix_post"][0], ffn_w(0), (w["kv_gin"], w["kv_w"], w["kv_g"]),
        (w["g_mix_pre"][1], w["wdq"], w["gq"], w["wq_nope"], w["wq_pe"], w["wq_pe_rot"], w["wuk_t"]), tm)
    lat, kpe, kcat = unpack(lat), unpack(kpe), unpack(kcat)
    q = q.reshape(bb, MLA_HEADS, nseq, t, QK_CAT).transpose(0, 2, 1, 3, 4).reshape(b, MLA_HEADS, t, QK_CAT)
    if past_kcat is not None:
        kcat = jnp.concatenate([past_kcat, kcat], axis=1)

    s_len = past + t
    s_pad = -(-s_len // ATT_TK) * ATT_TK
    if s_pad != s_len:
        kcat = jnp.pad(kcat, ((0, 0), (0, s_pad - s_len), (0, 0)))
    vt = kcat[:, :, :KV_RANK].reshape(b, s_pad // ATT_TK, ATT_TK, KV_RANK).transpose(0, 1, 3, 2)
    o_lat = _attention(q, kcat, vt, past, _tile(t, ATT_TK), ATT_TK)
    o_lat = o_lat.reshape(bb, nseq, MLA_HEADS, t, KV_RANK).transpose(0, 2, 1, 3, 4).reshape(bb, MLA_HEADS, tt, KV_RANK)
    x, new_ffn1 = _layer1_tail(o_lat, x, pack_hist(ffn_buf[1]), w["wuv"], w["wo"], w["g_mix_post"][1], ffn_w(1), tm)
    return (unpack(x), new_state[None], unpack_hist(new_conv)[None],
            jnp.stack([unpack_hist(new_ffn0), unpack_hist(new_ffn1)]), lat, kpe)


def kernel(x_prompt, x_sample, state_ssm, state_ssm_conv, state_ffn_conv, cache_kv_latent, cache_k_rope, norm_mix_pre, norm_mix_post, norm_ffn_pre, norm_ffn_post, ssm_w_in, ssm_conv_w, ssm_conv_b, ssm_dt_bias, ssm_a_log, ssm_d, ssm_norm, ssm_w_out, kv_norm_in, kv_w_dkv, kv_norm, kv_w_kr, kv_w_uk, kv_w_uv, mla_w_dq, mla_q_norm, mla_w_uq, mla_w_o, ffn_w_up, ffn_conv_w, ffn_conv_b, ffn_w_down):
    w = _prep_weights(norm_mix_pre, norm_mix_post, norm_ffn_pre, norm_ffn_post,
                      ssm_w_in, ssm_conv_w, ssm_conv_b, ssm_dt_bias, ssm_a_log, ssm_d, ssm_norm, ssm_w_out,
                      kv_norm_in, kv_w_dkv, kv_norm, kv_w_kr, kv_w_uk, kv_w_uv,
                      mla_w_dq, mla_q_norm, mla_w_uq, mla_w_o,
                      ffn_w_up, ffn_conv_w, ffn_conv_b, ffn_w_down)
    bp = x_prompt.shape[0]
    dtp = x_prompt.dtype
    y_p, p_ssm, p_conv, p_ffn, p_lat, p_kpe = _trunk(
        x_prompt,
        jnp.zeros((bp, SSM_CONV - 1, CONV_DIM), dtp),
        jnp.zeros((bp, SSM_HEADS, SSM_HEADDIM, D_STATE), dtp),
        jnp.zeros((2, bp, FFN_CONV - 1, 2 * D_FF), dtp),
        None, w)
    past_kcat = jnp.concatenate([cache_kv_latent, cache_k_rope], axis=-1).astype(BF16)
    y_s, s_ssm, s_conv, s_ffn, s_lat, s_kpe = _trunk(
        x_sample, state_ssm_conv[0], state_ssm[0], state_ffn_conv, past_kcat, w)
    return (y_p, y_s, p_ssm, p_conv, p_ffn, p_lat, p_kpe, s_ssm, s_conv, s_ffn, s_lat, s_kpe)
```

```python
import functools
import math

import jax
import jax.numpy as jnp
from jax import lax
from jax.experimental import pallas as pl
from jax.experimental.pallas import tpu as pltpu

F32 = jnp.float32
BF16 = jnp.bfloat16

D_MODEL = 1024
CHUNK = 64
EPS = 1e-6
D_INNER = 2048
SSM_HEADDIM = 64
SSM_HEADS = 32
SSM_GROUPS = 4
SSM_HPG = 8
D_STATE = 128
SSM_CONV = 4
GROUP_W = D_INNER // SSM_GROUPS
BC_W = SSM_GROUPS * D_STATE
CONV_DIM = D_INNER + 2 * BC_W
MLA_HEADS = 16
Q_RANK = 384
KV_RANK = 256
QK_NOPE = 128
QK_ROPE = 64
V_DIM = 128
QK_CAT = KV_RANK + QK_ROPE
ROPE_THETA = 10000.0
D_FF = 2816
FFN_CONV = 3
LANE = 128
CARRY_ROWS = 8
TOKEN_TILE = 256
COL_CHUNK = 512
FFN_CHUNK = 256
SSD_UNROLL = 4
ATT_TK = 256
LOG2E = 1.4426950408889634
VMEM_LIMIT = 56 * 1024 * 1024

NT_DIMS = (((1,), (1,)), ((), ()))
TN_DIMS = (((0,), (0,)), ((), ()))


def _rms(x, g):
    return x * lax.rsqrt(jnp.mean(x * x, axis=-1, keepdims=True) + EPS) * g


def _silu(x):
    return x * (1.0 / (1.0 + jnp.exp(-x)))


def _softplus(x):
    return jnp.maximum(x, 0.0) + jnp.log(1.0 + jnp.exp(-jnp.abs(x)))


def _const_spec(shape):
    zeros = (0,) * len(shape)
    return pl.BlockSpec(shape, lambda *_: zeros, pipeline_mode=pl.Buffered(1))


def _params(*sem):
    return pltpu.CompilerParams(dimension_semantics=sem, vmem_limit_bytes=VMEM_LIMIT)


def _init_carry(carry_ref, hist_ref, first_step):
    nseg, hist, _ = hist_ref.shape

    @pl.when(first_step)
    def _():
        carry_ref[...] = jnp.zeros(carry_ref.shape, F32)
        for s in range(nseg):
            carry_ref[(s + 1) * CARRY_ROWS - hist:(s + 1) * CARRY_ROWS, :] = hist_ref[s]


def _store_history(nbuf_ref, carry_ref):
    nseg, hist, _ = nbuf_ref.shape
    for s in range(nseg):
        nbuf_ref[s] = carry_ref[(s + 1) * CARRY_ROWS - hist:(s + 1) * CARRY_ROWS, :]


def _causal_conv(raw, carry_ref, cols, w_ref, b_ref, taps):
    nseg = carry_ref.shape[0] // CARRY_ROWS
    seg_len = raw.shape[0] // nseg
    hist = taps - 1
    outs = []
    for s in range(nseg):
        cur = raw[s * seg_len:(s + 1) * seg_len, :]
        prev = carry_ref[s * CARRY_ROWS:(s + 1) * CARRY_ROWS, cols]
        ext = jnp.concatenate([prev, cur], axis=0)
        acc = b_ref[:, cols] + cur * w_ref[hist:hist + 1, cols]
        for k in range(1, taps):
            acc = acc + ext[CARRY_ROWS - k:CARRY_ROWS - k + seg_len, :] * w_ref[hist - k:hist - k + 1, cols]
        carry_ref[s * CARRY_ROWS:(s + 1) * CARRY_ROWS, cols] = cur[seg_len - CARRY_ROWS:, :]
        outs.append(acc)
    return outs[0] if nseg == 1 else jnp.concatenate(outs, axis=0)


def _mamba_in_kernel(x_ref, buf_ref, g_ref, wzx_ref, wdt_ref, wdtt_ref, cw_ref, cb_ref, dtb_ref, dtbt_ref,
                     z_ref, xbc_ref, dt_ref, dtt_ref, nbuf_ref, ext_ref):
    _init_carry(ext_ref, buf_ref, pl.program_id(1) == 0)
    h = _rms(x_ref[...], g_ref[...]).astype(BF16)
    for j in range(D_INNER // COL_CHUNK):
        cols = slice(j * COL_CHUNK, (j + 1) * COL_CHUNK)
        z_ref[:, cols] = jnp.dot(h, wzx_ref[:, cols], preferred_element_type=F32).astype(BF16)
    nchunk = CONV_DIM // COL_CHUNK
    proj = lambda j: jnp.dot(h, wzx_ref[:, D_INNER + j * COL_CHUNK:D_INNER + (j + 1) * COL_CHUNK],
                             preferred_element_type=F32)
    raw = proj(0)
    for j in range(nchunk):
        nxt = proj(j + 1) if j + 1 < nchunk else None
        cols = slice(j * COL_CHUNK, (j + 1) * COL_CHUNK)
        xbc_ref[:, cols] = _silu(_causal_conv(raw, ext_ref, cols, cw_ref, cb_ref, SSM_CONV)).astype(BF16)
        raw = nxt
    _store_history(nbuf_ref, ext_ref)
    dt = jnp.dot(h, wdt_ref[...], preferred_element_type=F32)[:, :SSM_HEADS]
    dt_ref[...] = _softplus(dt + dtb_ref[...])
    dtt = lax.dot_general(wdtt_ref[...], h, NT_DIMS, preferred_element_type=F32)
    dtt_ref[...] = _softplus(dtt + dtbt_ref[...])


def _mamba_in(x, conv_buf, g, wzx, wdt, wdtt, cw, cb, dtb, dtbt, tm):
    b, t, _ = x.shape
    nseg = conv_buf.shape[1]
    grid = (b, t // tm)
    row = lambda w: pl.BlockSpec((None, tm, w), lambda i, j: (i, j, 0))
    return pl.pallas_call(
        _mamba_in_kernel,
        grid=grid,
        in_specs=[
            row(D_MODEL),
            pl.BlockSpec((None, nseg, SSM_CONV - 1, CONV_DIM), lambda i, j: (i, 0, 0, 0)),
            _const_spec((1, D_MODEL)),
            _const_spec((D_MODEL, D_INNER + CONV_DIM)),
            _const_spec((D_MODEL, LANE)),
            _const_spec((SSM_HEADS, D_MODEL)),
            _const_spec((SSM_CONV, CONV_DIM)),
            _const_spec((1, CONV_DIM)),
            _const_spec((1, SSM_HEADS)),
            _const_spec((SSM_HEADS, 1)),
        ],
        out_specs=[
            row(D_INNER),
            row(CONV_DIM),
            row(SSM_HEADS),
            pl.BlockSpec((None, SSM_HEADS, tm), lambda i, j: (i, 0, j)),
            pl.BlockSpec((None, nseg, SSM_CONV - 1, CONV_DIM), lambda i, j: (i, 0, 0, 0)),
        ],
        out_shape=[
            jax.ShapeDtypeStruct((b, t, D_INNER), BF16),
            jax.ShapeDtypeStruct((b, t, CONV_DIM), BF16),
            jax.ShapeDtypeStruct((b, t, SSM_HEADS), F32),
            jax.ShapeDtypeStruct((b, SSM_HEADS, t), F32),
            jax.ShapeDtypeStruct((b, nseg, SSM_CONV - 1, CONV_DIM), F32),
        ],
        scratch_shapes=[pltpu.VMEM((nseg * CARRY_ROWS, CONV_DIM), F32)],
        compiler_params=_params("parallel", "arbitrary"),
        name="mamba_in",
    )(x, conv_buf, g, wzx, wdt, wdtt, cw, cb, dtb, dtbt)


def _ssd_kernel(xs_ref, bm_ref, cm_ref, z_ref, dt_ref, dtt_ref, arow_ref, acol_ref, exp_ref, dskip_ref, gn_ref,
                h0_ref, yg_ref, hout_ref, st_ref, *, nc):
    step = pl.program_id(1)

    @pl.when(step == 0)
    def _():
        st_ref[...] = h0_ref[...].T

    li = lax.broadcasted_iota(jnp.int32, (CHUNK, CHUNK), 0)
    si = lax.broadcasted_iota(jnp.int32, (CHUNK, CHUNK), 1)
    tril = (si <= li).astype(F32)
    pi = lax.broadcasted_iota(jnp.int32, (LANE, LANE), 0)
    pj = lax.broadcasted_iota(jnp.int32, (LANE, LANE), 1)
    same_head = (pi // CHUNK) == (pj // CHUNK)
    triu2 = jnp.where(same_head, (pi <= pj).astype(F32), 0.0)
    l2 = lax.broadcasted_iota(jnp.int32, (CHUNK, LANE), 0)
    s2 = lax.broadcasted_iota(jnp.int32, (CHUNK, LANE), 1) % CHUNK
    causal2 = s2 <= l2
    expand = exp_ref[...]

    gsls = [slice(g * GROUP_W, (g + 1) * GROUP_W) for g in range(SSM_GROUPS)]

    def decays(c):
        rows = pl.ds(pl.multiple_of(c * CHUNK, CHUNK), CHUNK)
        dt = dt_ref[rows, :]
        dtt2 = dtt_ref[c]
        acs = jnp.dot(tril, dt * arow_ref[...], precision=lax.Precision.HIGHEST,
                      preferred_element_type=F32)
        acst2 = jnp.dot(dtt2 * acol_ref[...], triu2, precision=lax.Precision.HIGHEST,
                        preferred_element_type=F32)
        last = acs[CHUNK - 1:CHUNK, :]
        e_acs = jnp.exp(acs).astype(BF16)
        wst = (jnp.exp(last - acs) * dt).astype(BF16)
        hi = acs.astype(BF16)
        r1 = acs - hi.astype(F32)
        mid = r1.astype(BF16)
        lo = (r1 - mid.astype(F32)).astype(BF16)
        acs_w = (jnp.dot(hi, expand, preferred_element_type=F32) + jnp.dot(mid, expand, preferred_element_type=F32)
                 + jnp.dot(lo, expand, preferred_element_type=F32))
        e_acs_w = jnp.dot(e_acs, expand, preferred_element_type=F32)
        wst_w = jnp.dot(wst, expand, preferred_element_type=F32)
        decay_w = jnp.exp(acs_w[CHUNK - 1:CHUNK, :])
        return dtt2, acst2, acs_w, e_acs_w, wst_w, decay_w

    def through_state(c, wst_w, decay_w):
        rows = pl.ds(pl.multiple_of(c * CHUNK, CHUNK), CHUNK)
        cbs, yoffs = [], []
        for g, gsl in enumerate(gsls):
            bg = bm_ref[rows, g * D_STATE:(g + 1) * D_STATE]
            cg = cm_ref[rows, g * D_STATE:(g + 1) * D_STATE]
            cbs.append(lax.dot_general(cg, bg, NT_DIMS, preferred_element_type=F32))
            stg = st_ref[:, gsl]
            yoffs.append(jnp.dot(cg, stg.astype(BF16), preferred_element_type=F32))
            xw = (xs_ref[rows, gsl].astype(F32) * wst_w[:, gsl]).astype(BF16)
            st_ref[:, gsl] = stg * decay_w[:, gsl] + lax.dot_general(bg, xw, TN_DIMS, preferred_element_type=F32)
        return cbs, yoffs

    def within_chunk(c, dtt2, acst2, acs_w, e_acs_w, cbs, yoffs):
        rows = pl.ds(pl.multiple_of(c * CHUNK, CHUNK), CHUNK)
        for g, gsl in enumerate(gsls):
            xg = xs_ref[rows, gsl]
            cb2 = jnp.concatenate([cbs[g], cbs[g]], axis=1)
            ys = []
            for jj in range(SSM_HPG // 2):
                j = g * (SSM_HPG // 2) + jj
                seg = acs_w[:, j * LANE:(j + 1) * LANE] - acst2[j:j + 1, :]
                wts = cb2 * jnp.exp(jnp.where(causal2, seg, -jnp.inf)) * dtt2[j:j + 1, :]
                xp = xg[:, jj * LANE:(jj + 1) * LANE]
                xbd = jnp.where(same_head, jnp.concatenate([xp, xp], axis=0), jnp.zeros((), BF16))
                ys.append(jnp.dot(wts.astype(BF16), xbd, preferred_element_type=F32))
            y = jnp.concatenate(ys, axis=1)
            y = y + yoffs[g] * e_acs_w[:, gsl] + xg.astype(F32) * dskip_ref[:, gsl]
            yg = y * _silu(z_ref[rows, gsl].astype(F32))
            yg_ref[rows, gsl] = _rms(yg, gn_ref[:, gsl]).astype(BF16)

    unroll = min(nc, SSD_UNROLL)

    def body(i, carry):
        cs = [i * unroll + k for k in range(unroll)]
        pre = [decays(c) for c in cs]
        mid = [through_state(c, p[4], p[5]) for c, p in zip(cs, pre)]
        for c, p, m in zip(cs, pre, mid):
            within_chunk(c, p[0], p[1], p[2], p[3], m[0], m[1])
        return carry

    lax.fori_loop(0, nc // unroll, body, 0)

    @pl.when(step == pl.num_programs(1) - 1)
    def _():
        hout_ref[...] = st_ref[...].T


def _ssd(xbc, z, dt, dtt2, arow, acol2, expand, dskip, gn, h0t, nc):
    b, t, _ = xbc.shape
    lt = nc * CHUNK
    grid = (b, t // lt)
    state_spec = pl.BlockSpec((None, D_INNER, D_STATE), lambda i, j: (i, 0, 0))
    return pl.pallas_call(
        functools.partial(_ssd_kernel, nc=nc),
        grid=grid,
        in_specs=[
            pl.BlockSpec((None, lt, D_INNER), lambda i, j: (i, j, 0)),
            pl.BlockSpec((None, lt, BC_W), lambda i, j: (i, j, D_INNER // BC_W)),
            pl.BlockSpec((None, lt, BC_W), lambda i, j: (i, j, D_INNER // BC_W + 1)),
            pl.BlockSpec((None, lt, D_INNER), lambda i, j: (i, j, 0)),
            pl.BlockSpec((None, lt, SSM_HEADS), lambda i, j: (i, j, 0)),
            pl.BlockSpec((None, nc, SSM_HEADS // 2, 2 * CHUNK), lambda i, j: (i, j, 0, 0)),
            _const_spec((1, SSM_HEADS)),
            _const_spec((SSM_HEADS // 2, 2 * CHUNK)),
            _const_spec((SSM_HEADS, D_INNER)),
            _const_spec((1, D_INNER)),
            _const_spec((1, D_INNER)),
            state_spec,
        ],
        out_specs=[
            pl.BlockSpec((None, lt, D_INNER), lambda i, j: (i, j, 0)),
            state_spec,
        ],
        out_shape=[
            jax.ShapeDtypeStruct((b, t, D_INNER), BF16),
            jax.ShapeDtypeStruct((b, D_INNER, D_STATE), F32),
        ],
        scratch_shapes=[pltpu.VMEM((D_STATE, D_INNER), F32)],
        compiler_params=_params("parallel", "arbitrary"),
        name="ssd_scan",
    )(xbc, xbc, xbc, z, dt, dtt2, arow, acol2, expand, dskip, gn, h0t)


def _mixer_residual(mix_bf16, x, w_ref, g_ref):
    return x + _rms(jnp.dot(mix_bf16, w_ref[...], preferred_element_type=F32), g_ref[...])


def _mla_values(o_ref, wuv_ref):
    parts = []
    for h in range(MLA_HEADS):
        parts.append(jnp.dot(o_ref[h], wuv_ref[h], preferred_element_type=F32).astype(BF16))
    return jnp.concatenate(parts, axis=-1)


FFN_WEIGHT_SPECS = (
    (1, D_MODEL), (D_MODEL, 2 * D_FF), (FFN_CONV, 2 * D_FF), (1, 2 * D_FF), (D_FF, D_MODEL), (1, D_MODEL))


def _ffn_body(x, buf_ref, gpre_ref, wup_ref, cw_ref, cb_ref, wdn_ref, gpost_ref, nbuf_ref, ext_ref, act_ref):
    _init_carry(ext_ref, buf_ref, pl.program_id(1) == 0)
    h = _rms(x, gpre_ref[...]).astype(BF16)
    nchunk = D_FF // FFN_CHUNK
    vcols = [slice(j * FFN_CHUNK, (j + 1) * FFN_CHUNK) for j in range(nchunk)]
    gcols = [slice(D_FF + j * FFN_CHUNK, D_FF + (j + 1) * FFN_CHUNK) for j in range(nchunk)]
    up = lambda j: (jnp.dot(h, wup_ref[:, vcols[j]], preferred_element_type=F32),
                    jnp.dot(h, wup_ref[:, gcols[j]], preferred_element_type=F32))
    raw = up(0)
    for j in range(nchunk):
        nxt = up(j + 1) if j + 1 < nchunk else None
        val = _causal_conv(raw[0], ext_ref, vcols[j], cw_ref, cb_ref, FFN_CONV)
        gate = _causal_conv(raw[1], ext_ref, gcols[j], cw_ref, cb_ref, FFN_CONV)
        act_ref[:, vcols[j]] = (jax.nn.gelu(gate, approximate=True) * val).astype(BF16)
        raw = nxt
    _store_history(nbuf_ref, ext_ref)
    f = jnp.dot(act_ref[...], wdn_ref[...], preferred_element_type=F32)
    return x + _rms(f, gpost_ref[...])


def _ffn_scratch(tm, nseg):
    return [pltpu.VMEM((nseg * CARRY_ROWS, 2 * D_FF), F32), pltpu.VMEM((tm, D_FF), BF16)]


KV_WEIGHT_SPECS = ((1, D_MODEL), (D_MODEL, KV_RANK + 2 * QK_ROPE), (1, KV_RANK))


def _kv_body(x, gin_ref, w_ref, gkv_ref, cos_ref, sin_ref, lat_ref, kpe_ref, kcat_ref):
    h = _rms(x, gin_ref[...]).astype(BF16)
    hw = jnp.dot(h, w_ref[...], preferred_element_type=F32)
    lat = _rms(hw[:, :KV_RANK], gkv_ref[...])
    kpe = hw[:, KV_RANK:QK_CAT] * cos_ref[:, :QK_ROPE] + hw[:, QK_CAT:] * sin_ref[:, :QK_ROPE]
    lat_ref[...] = lat
    kpe_ref[...] = kpe
    kcat_ref[:, :KV_RANK] = lat.astype(BF16)
    kcat_ref[:, KV_RANK:] = kpe.astype(BF16)


Q_SCALE = (QK_NOPE + QK_ROPE) ** -0.5 * LOG2E
Q_WEIGHT_SPECS = (
    (1, D_MODEL), (D_MODEL, Q_RANK), (1, Q_RANK), (Q_RANK, MLA_HEADS * QK_NOPE), (MLA_HEADS * QK_ROPE, Q_RANK),
    (MLA_HEADS * QK_ROPE, Q_RANK), (MLA_HEADS, KV_RANK, QK_NOPE))


def _q_body(x, gpre_ref, wdq_ref, gq_ref, wn_ref, wpt_ref, wprt_ref, wuk_ref, cost_ref, sint_ref, qt_ref):
    tm = x.shape[0]
    scale = Q_SCALE
    h = _rms(x, gpre_ref[...]).astype(BF16)
    cq = _rms(jnp.dot(h, wdq_ref[...], preferred_element_type=F32), gq_ref[...]).astype(BF16)
    qn = jnp.dot(cq, wn_ref[...], preferred_element_type=F32).astype(BF16)
    qpt = lax.dot_general(wpt_ref[...], cq, NT_DIMS, preferred_element_type=F32)
    qprt = lax.dot_general(wprt_ref[...], cq, NT_DIMS, preferred_element_type=F32)
    reps = MLA_HEADS * QK_ROPE // LANE
    cost = jnp.concatenate([cost_ref[...]] * reps, axis=0)
    sint = jnp.concatenate([sint_ref[...]] * reps, axis=0)
    qpet = ((qpt * cost + qprt * sint) * scale).astype(BF16)
    for hd in range(MLA_HEADS):
        qlt = lax.dot_general(wuk_ref[hd], qn[:, hd * QK_NOPE:(hd + 1) * QK_NOPE], NT_DIMS,
                              preferred_element_type=F32)
        qt_ref[:KV_RANK, hd * tm:(hd + 1) * tm] = (qlt * scale).astype(BF16)
        qt_ref[KV_RANK:, hd * tm:(hd + 1) * tm] = qpet[hd * QK_ROPE:(hd + 1) * QK_ROPE, :]


def _layer0_tail_kernel(*refs):
    yg_ref, x_ref, buf_ref, cos_ref, sin_ref, cost_ref, sint_ref, wout_ref, gmix_ref = refs[:9]
    ffn_w = refs[9:15]
    kv_w = refs[15:18]
    q_w = refs[18:25]
    x_out_ref, nbuf_ref, lat_ref, kpe_ref, kcat_ref, qt_ref, ext_ref, act_ref = refs[25:]
    x = _mixer_residual(yg_ref[...], x_ref[...], wout_ref, gmix_ref)
    x = _ffn_body(x, buf_ref, *ffn_w, nbuf_ref, ext_ref, act_ref)
    x_out_ref[...] = x
    _kv_body(x, *kv_w, cos_ref, sin_ref, lat_ref, kpe_ref, kcat_ref)
    _q_body(x, *q_w, cost_ref, sint_ref, qt_ref)


def _layer0_tail(yg, x, ffn_buf, cos, sin, w_out, g_mix, ffn_w, kv_w, q_w, tm):
    b, t, _ = x.shape
    grid = (b, t // tm)
    row = lambda wd: pl.BlockSpec((None, tm, wd), lambda i, j: (i, j, 0))
    nseg = ffn_buf.shape[1]
    buf_spec = pl.BlockSpec((None, nseg, FFN_CONV - 1, 2 * D_FF), lambda i, j: (i, 0, 0, 0))
    tab = pl.BlockSpec((tm, LANE), lambda i, j: (j, 0))
    tab_t = pl.BlockSpec((LANE, tm), lambda i, j: (0, j))
    weight_shapes = ((D_INNER, D_MODEL), (1, D_MODEL)) + FFN_WEIGHT_SPECS + KV_WEIGHT_SPECS + Q_WEIGHT_SPECS
    return pl.pallas_call(
        _layer0_tail_kernel,
        grid=grid,
        in_specs=[row(D_INNER), row(D_MODEL), buf_spec, tab, tab, tab_t, tab_t]
        + [_const_spec(s) for s in weight_shapes],
        out_specs=[
            row(D_MODEL), buf_spec, row(KV_RANK), row(QK_ROPE), row(QK_CAT),
            pl.BlockSpec((None, None, QK_CAT, MLA_HEADS * tm), lambda i, j: (i, j, 0, 0)),
        ],
        out_shape=[
            jax.ShapeDtypeStruct((b, t, D_MODEL), F32),
            jax.ShapeDtypeStruct((b, nseg, FFN_CONV - 1, 2 * D_FF), F32),
            jax.ShapeDtypeStruct((b, t, KV_RANK), F32),
            jax.ShapeDtypeStruct((b, t, QK_ROPE), F32),
            jax.ShapeDtypeStruct((b, t, QK_CAT), BF16),
            jax.ShapeDtypeStruct((b, t // tm, QK_CAT, MLA_HEADS * tm), BF16),
        ],
        scratch_shapes=_ffn_scratch(tm, nseg),
        compiler_params=_params("parallel", "arbitrary"),
        name="layer0_tail",
    )(yg, x, ffn_buf, cos, sin, cos.T, sin.T, w_out, g_mix, *ffn_w, *kv_w, *q_w)


def _layer1_tail_kernel(*refs):
    o_ref, x_ref, buf_ref, wuv_ref, wo_ref, gmix_ref = refs[:6]
    ffn_w = refs[6:12]
    x_out_ref, nbuf_ref, ext_ref, act_ref = refs[12:]
    x = _mixer_residual(_mla_values(o_ref, wuv_ref), x_ref[...], wo_ref, gmix_ref)
    x_out_ref[...] = _ffn_body(x, buf_ref, *ffn_w, nbuf_ref, ext_ref, act_ref)


def _layer1_tail(o_lat, x, ffn_buf, wuv, wo, g_mix, ffn_w, tm):
    b, t, _ = x.shape
    grid = (b, t // tm)
    row = lambda wd: pl.BlockSpec((None, tm, wd), lambda i, j: (i, j, 0))
    nseg = ffn_buf.shape[1]
    buf_spec = pl.BlockSpec((None, nseg, FFN_CONV - 1, 2 * D_FF), lambda i, j: (i, 0, 0, 0))
    weight_shapes = ((MLA_HEADS, KV_RANK, V_DIM), (MLA_HEADS * V_DIM, D_MODEL), (1, D_MODEL)) + FFN_WEIGHT_SPECS
    return pl.pallas_call(
        _layer1_tail_kernel,
        grid=grid,
        in_specs=[pl.BlockSpec((None, MLA_HEADS, tm, KV_RANK), lambda i, j: (i, 0, j, 0)), row(D_MODEL), buf_spec]
        + [_const_spec(s) for s in weight_shapes],
        out_specs=[row(D_MODEL), buf_spec],
        out_shape=[
            jax.ShapeDtypeStruct((b, t, D_MODEL), F32),
            jax.ShapeDtypeStruct((b, nseg, FFN_CONV - 1, 2 * D_FF), F32),
        ],
        scratch_shapes=_ffn_scratch(tm, nseg),
        compiler_params=_params("parallel", "arbitrary"),
        name="layer1_tail",
    )(o_lat, x, ffn_buf, wuv, wo, g_mix, *ffn_w)


def _attn_kernel(qt_ref, k_ref, vt_ref, o_ref, m_ref, l_ref, a_ref, acc_ref, s_ref, p_ref, *, tq, tk, past, nkv):
    qi = pl.program_id(1)
    rows = MLA_HEADS * tq
    nblk = rows // LANE
    blocks = [slice(b * LANE, (b + 1) * LANE) for b in range(nblk)]
    m_ref[...] = jnp.full(m_ref.shape, -jnp.inf, F32)
    l_ref[...] = jnp.zeros(l_ref.shape, F32)
    acc_ref[...] = jnp.zeros(acc_ref.shape, F32)
    n_vis = jnp.minimum(nkv, (past + (qi + 1) * tq + tk - 1) // tk)

    def scores(j):
        ks = k_ref[pl.ds(pl.multiple_of(j * tk, tk), tk), :]
        return jnp.dot(ks, qt_ref[...], preferred_element_type=F32)

    def store_scores(slot, s):
        for b, csl in enumerate(blocks):
            s_ref[slot, b] = s[:, csl]

    def hidden_bias(j):
        key_chunk = lax.broadcasted_iota(jnp.int32, (tk, LANE), 0) // CHUNK + j * (tk // CHUNK)
        lane = lax.broadcasted_iota(jnp.int32, (tk, LANE), 1)
        pats = []
        for ph in range(max(1, tq // LANE)):
            q_chunk = (past + qi * tq) // CHUNK + ((ph * LANE + lane) % tq) // CHUNK
            pats.append(jnp.where(key_chunk <= q_chunk, 0.0, -jnp.inf))
        return pats

    def absorb(j, masked):
        slot = j % 2
        pats = hidden_bias(j) if masked else None
        for b, csl in enumerate(blocks):
            s = s_ref[slot, b]
            if masked:
                s = s + pats[b % len(pats)]
            m_prev = m_ref[:, csl]
            m_new = jnp.maximum(m_prev, jnp.max(s, axis=0, keepdims=True))
            alpha = jnp.exp2(m_prev - m_new)
            p = jnp.exp2(s - m_new)
            l_ref[:, csl] = alpha * l_ref[:, csl] + jnp.sum(p, axis=0, keepdims=True)
            m_ref[:, csl] = m_new
            a_ref[:, csl] = alpha
            p_ref[b] = p.astype(BF16)
        p_all = jnp.concatenate([p_ref[b] for b in range(nblk)], axis=1)
        pv = jnp.dot(vt_ref[j], p_all, preferred_element_type=F32)
        for b, csl in enumerate(blocks):
            acc_ref[b] = a_ref[:, csl] * acc_ref[b] + pv[:, csl]

    store_scores(0, scores(0))

    def body(j, carry):
        s_new = scores(j)
        absorb(j - 1, masked=False)
        store_scores(j % 2, s_new)
        return carry

    lax.fori_loop(1, n_vis, body, 0)
    absorb(n_vis - 1, masked=True)
    o = jnp.concatenate([acc_ref[b] / l_ref[:, csl] for b, csl in enumerate(blocks)], axis=1).T
    o_ref[...] = o.astype(BF16).reshape(MLA_HEADS, tq, KV_RANK)


def _attention(qt, kcat, vt, past, tq, tk):
    b, ng, _, rows = qt.shape
    t = ng * tq
    s_len = kcat.shape[1]
    nkv = s_len // tk
    assert tk % tq == 0 and past % tk == 0 and rows == MLA_HEADS * tq and tq % CHUNK == 0
    grid = (b, ng)
    nblk = rows // LANE
    return pl.pallas_call(
        functools.partial(_attn_kernel, tq=tq, tk=tk, past=past, nkv=nkv),
        grid=grid,
        in_specs=[
            pl.BlockSpec((None, None, QK_CAT, rows), lambda i, j: (i, j, 0, 0)),
            pl.BlockSpec((None, s_len, QK_CAT), lambda i, j: (i, 0, 0)),
            pl.BlockSpec((None, nkv, KV_RANK, tk), lambda i, j: (i, 0, 0, 0)),
        ],
        out_specs=pl.BlockSpec((None, MLA_HEADS, tq, KV_RANK), lambda i, j: (i, 0, j, 0)),
        out_shape=jax.ShapeDtypeStruct((b, MLA_HEADS, t, KV_RANK), BF16),
        scratch_shapes=[
            pltpu.VMEM((1, rows), F32),
            pltpu.VMEM((1, rows), F32),
            pltpu.VMEM((1, rows), F32),
            pltpu.VMEM((nblk, KV_RANK, LANE), F32),
            pltpu.VMEM((2, nblk, tk, LANE), F32),
            pltpu.VMEM((nblk, tk, LANE), BF16),
        ],
        compiler_params=_params("parallel", "parallel"),
        name="mla_attention",
    )(qt, kcat, vt)


def _rope_tables(past, t):
    half = QK_ROPE // 2
    inv = jnp.exp(-math.log(ROPE_THETA) * jnp.arange(half, dtype=F32) / half)
    ang = (past + jnp.arange(t, dtype=jnp.int32)).astype(F32)[:, None] * inv[None, :]
    reps = 2 * LANE // QK_ROPE
    return jnp.tile(jnp.cos(ang), (1, reps)), jnp.tile(jnp.sin(ang), (1, reps))


def _rotate_half_cols(w):
    shp = w.shape
    w4 = w.reshape(shp[:-1] + (shp[-1] // QK_ROPE, 2, QK_ROPE // 2))
    return jnp.concatenate([-w4[..., 1:2, :], w4[..., 0:1, :]], axis=-2).reshape(shp)


def _tile(t, pref):
    return pref if t % pref == 0 else t


def _prep_weights(norm_mix_pre, norm_mix_post, norm_ffn_pre, norm_ffn_post,
                  ssm_w_in, ssm_conv_w, ssm_conv_b, ssm_dt_bias, ssm_a_log, ssm_d, ssm_norm, ssm_w_out,
                  kv_norm_in, kv_w_dkv, kv_norm, kv_w_kr, kv_w_uk, kv_w_uv,
                  mla_w_dq, mla_q_norm, mla_w_uq, mla_w_o,
                  ffn_w_up, ffn_conv_w, ffn_conv_b, ffn_w_down):
    w = {}
    w_in = ssm_w_in[0]
    w["wzx"] = w_in[:, :D_INNER + CONV_DIM].astype(BF16)
    w_dt = w_in[:, D_INNER + CONV_DIM:]
    w["wdt"] = jnp.pad(w_dt, ((0, 0), (0, LANE - SSM_HEADS))).astype(BF16)
    w["wdtt"] = w_dt.T.astype(BF16)
    w["ssm_cw"] = ssm_conv_w[0]
    w["ssm_cb"] = ssm_conv_b[0][None, :]
    w["dtb"] = ssm_dt_bias[0][None, :]
    w["dtbt"] = ssm_dt_bias[0][:, None]
    a = -jnp.exp(ssm_a_log[0].astype(F32))
    w["arow"] = a[None, :]
    w["acol2"] = jnp.broadcast_to(a.reshape(SSM_HEADS // 2, 2, 1),
                                  (SSM_HEADS // 2, 2, CHUNK)).reshape(SSM_HEADS // 2, 2 * CHUNK)
    w["expand"] = jnp.repeat(jnp.eye(SSM_HEADS, dtype=BF16), SSM_HEADDIM, axis=1)
    w["dskip"] = jnp.repeat(ssm_d[0], SSM_HEADDIM)[None, :]
    w["gn"] = ssm_norm[0][None, :]
    w["w_out"] = ssm_w_out[0].astype(BF16)
    w["g_mix_pre"] = norm_mix_pre[:, None, :]
    w["g_mix_post"] = norm_mix_post[:, None, :]
    w["g_ffn_pre"] = norm_ffn_pre[:, None, :]
    w["g_ffn_post"] = norm_ffn_post[:, None, :]
    w["ffn_up"] = [ffn_w_up[i].astype(BF16) for i in range(2)]
    w["ffn_cw"] = ffn_conv_w
    w["ffn_cb"] = ffn_conv_b[:, None, :]
    w["ffn_down"] = [ffn_w_down[i].astype(BF16) for i in range(2)]
    w["kv_gin"] = kv_norm_in[None, :]
    w["kv_w"] = jnp.concatenate([kv_w_dkv, kv_w_kr, _rotate_half_cols(kv_w_kr)], axis=-1).astype(BF16)
    w["kv_g"] = kv_norm[None, :]
    w["wdq"] = mla_w_dq[0].astype(BF16)
    w["gq"] = mla_q_norm[0][None, :]
    wuq = mla_w_uq[0].reshape(Q_RANK, MLA_HEADS, QK_NOPE + QK_ROPE)
    w["wq_nope"] = wuq[:, :, :QK_NOPE].reshape(Q_RANK, MLA_HEADS * QK_NOPE).astype(BF16)
    wq_pe = wuq[:, :, QK_NOPE:].reshape(Q_RANK, MLA_HEADS * QK_ROPE)
    w["wq_pe_t"] = wq_pe.T.astype(BF16)
    w["wq_pe_rot_t"] = _rotate_half_cols(wq_pe).T.astype(BF16)
    w["wuk"] = jnp.transpose(kv_w_uk, (1, 0, 2)).astype(BF16)
    w["wuv"] = jnp.transpose(kv_w_uv, (1, 0, 2)).astype(BF16)
    w["wo"] = mla_w_o[0].astype(BF16)
    return w


def _trunk(x, ssm_conv_buf, ssm_state, ffn_buf, past_kcat, w):
    b, t, _ = x.shape
    past = 0 if past_kcat is None else past_kcat.shape[1]
    assert t % CHUNK == 0 and past % CHUNK == 0
    tm = TOKEN_TILE
    nseq = tm // t if t < tm and tm % t == 0 and b % (tm // t) == 0 else 1
    if t % tm != 0 and nseq == 1:
        tm = t
    bb, tt = b // nseq, nseq * t
    pack = lambda a: a.reshape((bb, tt) + a.shape[2:])
    unpack = lambda a: a.reshape((b, t) + a.shape[2:])
    pack_hist = lambda a: a.reshape((bb, nseq) + a.shape[1:])
    unpack_hist = lambda a: a.reshape((b,) + a.shape[2:])

    z, xbc, dt, dtt, new_conv = _mamba_in(pack(x), pack_hist(ssm_conv_buf), w["g_mix_pre"][0], w["wzx"], w["wdt"],
                                          w["wdtt"], w["ssm_cw"], w["ssm_cb"], w["dtb"], w["dtbt"], tm)
    nc = _tile(t, 256) // CHUNK
    dtt2 = dtt.reshape(bb, SSM_HEADS // 2, 2, tt // CHUNK, CHUNK).transpose(0, 3, 1, 2, 4)
    dtt2 = dtt2.reshape(b, t // CHUNK, SSM_HEADS // 2, 2 * CHUNK)
    yg, new_state = _ssd(unpack(xbc), unpack(z), unpack(dt), dtt2, w["arow"], w["acol2"], w["expand"], w["dskip"],
                         w["gn"], ssm_state.reshape(b, D_INNER, D_STATE), nc)
    new_state = new_state.reshape(b, SSM_HEADS, SSM_HEADDIM, D_STATE)

    ffn_w = lambda i: (w["g_ffn_pre"][i], w["ffn_up"][i], w["ffn_cw"][i], w["ffn_cb"][i], w["ffn_down"][i],
                       w["g_ffn_post"][i])
    cos, sin = _rope_tables(past, t)
    x, new_ffn0, lat, kpe, kcat, q = _layer0_tail(
        pack(yg), pack(x), pack_hist(ffn_buf[0]), jnp.tile(cos, (nseq, 1)), jnp.tile(sin, (nseq, 1)),
        w["w_out"], w["g_mix_post"][0], ffn_w(0), (w["kv_gin"], w["kv_w"], w["kv_g"]),
        (w["g_mix_pre"][1], w["wdq"], w["gq"], w["wq_nope"], w["wq_pe_t"], w["wq_pe_rot_t"], w["wuk"]), tm)
    lat, kpe, kcat = unpack(lat), unpack(kpe), unpack(kcat)
    tq = _tile(t, ATT_TK)
    assert tq == (t if nseq > 1 else tm)
    if nseq > 1:
        q = q.reshape(bb, QK_CAT, MLA_HEADS, nseq, t).transpose(0, 3, 1, 2, 4).reshape(b, 1, QK_CAT, MLA_HEADS * t)
    if past_kcat is not None:
        kcat = jnp.concatenate([past_kcat, kcat], axis=1)

    s_len = past + t
    s_pad = -(-s_len // ATT_TK) * ATT_TK
    if s_pad != s_len:
        kcat = jnp.pad(kcat, ((0, 0), (0, s_pad - s_len), (0, 0)))
    vt = kcat[:, :, :KV_RANK].reshape(b, s_pad // ATT_TK, ATT_TK, KV_RANK).transpose(0, 1, 3, 2)
    o_lat = _attention(q, kcat, vt, past, tq, ATT_TK)
    o_lat = o_lat.reshape(bb, nseq, MLA_HEADS, t, KV_RANK).transpose(0, 2, 1, 3, 4).reshape(bb, MLA_HEADS, tt, KV_RANK)
    x, new_ffn1 = _layer1_tail(o_lat, x, pack_hist(ffn_buf[1]), w["wuv"], w["wo"], w["g_mix_post"][1], ffn_w(1), tm)
    return (unpack(x), new_state[None], unpack_hist(new_conv)[None],
            jnp.stack([unpack_hist(new_ffn0), unpack_hist(new_ffn1)]), lat, kpe)


def kernel(x_prompt, x_sample, state_ssm, state_ssm_conv, state_ffn_conv, cache_kv_latent, cache_k_rope, norm_mix_pre, norm_mix_post, norm_ffn_pre, norm_ffn_post, ssm_w_in, ssm_conv_w, ssm_conv_b, ssm_dt_bias, ssm_a_log, ssm_d, ssm_norm, ssm_w_out, kv_norm_in, kv_w_dkv, kv_norm, kv_w_kr, kv_w_uk, kv_w_uv, mla_w_dq, mla_q_norm, mla_w_uq, mla_w_o, ffn_w_up, ffn_conv_w, ffn_conv_b, ffn_w_down):
    w = _prep_weights(norm_mix_pre, norm_mix_post, norm_ffn_pre, norm_ffn_post,
                      ssm_w_in, ssm_conv_w, ssm_conv_b, ssm_dt_bias, ssm_a_log, ssm_d, ssm_norm, ssm_w_out,
                      kv_norm_in, kv_w_dkv, kv_norm, kv_w_kr, kv_w_uk, kv_w_uv,
                      mla_w_dq, mla_q_norm, mla_w_uq, mla_w_o,
                      ffn_w_up, ffn_conv_w, ffn_conv_b, ffn_w_down)
    bp = x_prompt.shape[0]
    dtp = x_prompt.dtype
    y_p, p_ssm, p_conv, p_ffn, p_lat, p_kpe = _trunk(
        x_prompt,
        jnp.zeros((bp, SSM_CONV - 1, CONV_DIM), dtp),
        jnp.zeros((bp, SSM_HEADS, SSM_HEADDIM, D_STATE), dtp),
        jnp.zeros((2, bp, FFN_CONV - 1, 2 * D_FF), dtp),
        None, w)
    past_kcat = jnp.concatenate([cache_kv_latent, cache_k_rope], axis=-1).astype(BF16)
    y_s, s_ssm, s_conv, s_ffn, s_lat, s_kpe = _trunk(
        x_sample, state_ssm_conv[0], state_ssm[0], state_ffn_conv, past_kcat, w)
    return (y_p, y_s, p_ssm, p_conv, p_ffn, p_lat, p_kpe, s_ssm, s_conv, s_ffn, s_lat, s_kpe)
```

```python
import functools
import math

import jax
import jax.numpy as jnp
from jax import lax
from jax.experimental import pallas as pl
from jax.experimental.pallas import tpu as pltpu

F32 = jnp.float32
BF16 = jnp.bfloat16

D_MODEL = 1024
CHUNK = 64
EPS = 1e-6
D_INNER = 2048
SSM_HEADDIM = 64
SSM_HEADS = 32
SSM_GROUPS = 4
SSM_HPG = 8
D_STATE = 128
SSM_CONV = 4
GROUP_W = D_INNER // SSM_GROUPS
BC_W = SSM_GROUPS * D_STATE
CONV_DIM = D_INNER + 2 * BC_W
MLA_HEADS = 16
Q_RANK = 384
KV_RANK = 256
QK_NOPE = 128
QK_ROPE = 64
V_DIM = 128
QK_CAT = KV_RANK + QK_ROPE
ROPE_THETA = 10000.0
D_FF = 2816
FFN_CONV = 3
LANE = 128
CARRY_ROWS = 8
TOKEN_TILE = 256
COL_CHUNK = 512
FFN_CHUNK = 256
SSD_UNROLL = 4
ATT_TK = 256
LOG2E = 1.4426950408889634
VMEM_LIMIT = 56 * 1024 * 1024

NT_DIMS = (((1,), (1,)), ((), ()))
TN_DIMS = (((0,), (0,)), ((), ()))


def _rms(x, g):
    return x * lax.rsqrt(jnp.mean(x * x, axis=-1, keepdims=True) + EPS) * g


def _silu(x):
    return x * (1.0 / (1.0 + jnp.exp(-x)))


def _softplus(x):
    return jnp.maximum(x, 0.0) + jnp.log(1.0 + jnp.exp(-jnp.abs(x)))


def _const_spec(shape):
    zeros = (0,) * len(shape)
    return pl.BlockSpec(shape, lambda *_: zeros, pipeline_mode=pl.Buffered(1))


def _layer_spec(shape, layer):
    zeros = (0,) * len(shape)
    return pl.BlockSpec((None,) + tuple(shape), lambda *_: (layer,) + zeros, pipeline_mode=pl.Buffered(1))


def _params(*sem):
    return pltpu.CompilerParams(dimension_semantics=sem, vmem_limit_bytes=VMEM_LIMIT)


def _init_carry(carry_ref, hist_ref, first_step):
    nseg, hist, _ = hist_ref.shape

    @pl.when(first_step)
    def _():
        carry_ref[...] = jnp.zeros(carry_ref.shape, F32)
        for s in range(nseg):
            carry_ref[(s + 1) * CARRY_ROWS - hist:(s + 1) * CARRY_ROWS, :] = hist_ref[s]


def _store_history(nbuf_ref, carry_ref):
    nseg, hist, _ = nbuf_ref.shape
    for s in range(nseg):
        nbuf_ref[s] = carry_ref[(s + 1) * CARRY_ROWS - hist:(s + 1) * CARRY_ROWS, :]


def _causal_conv(raw, carry_ref, cols, w_ref, b_ref, taps):
    nseg = carry_ref.shape[0] // CARRY_ROWS
    seg_len = raw.shape[0] // nseg
    hist = taps - 1
    outs = []
    for s in range(nseg):
        cur = raw[s * seg_len:(s + 1) * seg_len, :]
        prev = carry_ref[s * CARRY_ROWS:(s + 1) * CARRY_ROWS, cols]
        ext = jnp.concatenate([prev, cur], axis=0)
        acc = b_ref[:, cols] + cur * w_ref[hist:hist + 1, cols]
        for k in range(1, taps):
            acc = acc + ext[CARRY_ROWS - k:CARRY_ROWS - k + seg_len, :] * w_ref[hist - k:hist - k + 1, cols]
        carry_ref[s * CARRY_ROWS:(s + 1) * CARRY_ROWS, cols] = cur[seg_len - CARRY_ROWS:, :]
        outs.append(acc)
    return outs[0] if nseg == 1 else jnp.concatenate(outs, axis=0)


def _mamba_in_kernel(x_ref, buf_ref, g_ref, wzx_ref, wdt_ref, wdtt_ref, cw_ref, cb_ref, dtb_ref, dtbt_ref,
                     z_ref, xbc_ref, dt_ref, dtt_ref, nbuf_ref, ext_ref):
    _init_carry(ext_ref, buf_ref, pl.program_id(1) == 0)
    h = _rms(x_ref[...], g_ref[...]).astype(BF16)
    for j in range(D_INNER // COL_CHUNK):
        cols = slice(j * COL_CHUNK, (j + 1) * COL_CHUNK)
        z_ref[:, cols] = jnp.dot(h, wzx_ref[:, cols], preferred_element_type=F32).astype(BF16)
    nchunk = CONV_DIM // COL_CHUNK
    proj = lambda j: jnp.dot(h, wzx_ref[:, D_INNER + j * COL_CHUNK:D_INNER + (j + 1) * COL_CHUNK],
                             preferred_element_type=F32)
    raw = proj(0)
    for j in range(nchunk):
        nxt = proj(j + 1) if j + 1 < nchunk else None
        cols = slice(j * COL_CHUNK, (j + 1) * COL_CHUNK)
        xbc_ref[:, cols] = _silu(_causal_conv(raw, ext_ref, cols, cw_ref, cb_ref, SSM_CONV)).astype(BF16)
        raw = nxt
    _store_history(nbuf_ref, ext_ref)
    dt = jnp.dot(h, wdt_ref[...], preferred_element_type=F32)[:, :SSM_HEADS]
    dt_ref[...] = _softplus(dt + dtb_ref[...])
    dtt = _softplus(lax.dot_general(wdtt_ref[...], h, NT_DIMS, preferred_element_type=F32) + dtbt_ref[...])
    for c in range(h.shape[0] // CHUNK):
        for i in range(2):
            dtt_ref[c, :, i * CHUNK:(i + 1) * CHUNK] = dtt[i * (SSM_HEADS // 2):(i + 1) * (SSM_HEADS // 2),
                                                          c * CHUNK:(c + 1) * CHUNK]


def _mamba_in(x, conv_buf, g, wzx, wdt, wdtt, cw, cb, dtb, dtbt, tm):
    b, t, _ = x.shape
    nseg = conv_buf.shape[1]
    grid = (b, t // tm)
    row = lambda w: pl.BlockSpec((None, tm, w), lambda i, j: (i, j, 0))
    return pl.pallas_call(
        _mamba_in_kernel,
        grid=grid,
        in_specs=[
            row(D_MODEL),
            pl.BlockSpec((None, nseg, SSM_CONV - 1, CONV_DIM), lambda i, j: (i, 0, 0, 0)),
            _const_spec((1, D_MODEL)),
            _const_spec((D_MODEL, D_INNER + CONV_DIM + SSM_HEADS)),
            _const_spec((D_MODEL, LANE)),
            _const_spec((SSM_HEADS, D_MODEL)),
            _const_spec((SSM_CONV, CONV_DIM)),
            _const_spec((1, CONV_DIM)),
            _const_spec((1, SSM_HEADS)),
            _const_spec((SSM_HEADS, 1)),
        ],
        out_specs=[
            row(D_INNER),
            row(CONV_DIM),
            row(SSM_HEADS),
            pl.BlockSpec((None, tm // CHUNK, SSM_HEADS // 2, 2 * CHUNK), lambda i, j: (i, j, 0, 0)),
            pl.BlockSpec((None, nseg, SSM_CONV - 1, CONV_DIM), lambda i, j: (i, 0, 0, 0)),
        ],
        out_shape=[
            jax.ShapeDtypeStruct((b, t, D_INNER), BF16),
            jax.ShapeDtypeStruct((b, t, CONV_DIM), BF16),
            jax.ShapeDtypeStruct((b, t, SSM_HEADS), F32),
            jax.ShapeDtypeStruct((b, t // CHUNK, SSM_HEADS // 2, 2 * CHUNK), F32),
            jax.ShapeDtypeStruct((b, nseg, SSM_CONV - 1, CONV_DIM), F32),
        ],
        scratch_shapes=[pltpu.VMEM((nseg * CARRY_ROWS, CONV_DIM), F32)],
        compiler_params=_params("parallel", "arbitrary"),
        name="mamba_in",
    )(x, conv_buf, g, wzx, wdt, wdtt, cw, cb, dtb, dtbt)


def _ssd_kernel(xs_ref, bm_ref, cm_ref, z_ref, dt_ref, dtt_ref, arow_ref, acol_ref, exp_ref, dskip_ref, gn_ref,
                h0_ref, yg_ref, hout_ref, st_ref, *, nc):
    step = pl.program_id(1)

    @pl.when(step == 0)
    def _():
        st_ref[...] = h0_ref[...].T

    li = lax.broadcasted_iota(jnp.int32, (CHUNK, CHUNK), 0)
    si = lax.broadcasted_iota(jnp.int32, (CHUNK, CHUNK), 1)
    tril = (si <= li).astype(F32)
    pi = lax.broadcasted_iota(jnp.int32, (LANE, LANE), 0)
    pj = lax.broadcasted_iota(jnp.int32, (LANE, LANE), 1)
    same_head = (pi // CHUNK) == (pj // CHUNK)
    triu2 = jnp.where(same_head, (pi <= pj).astype(F32), 0.0)
    l2 = lax.broadcasted_iota(jnp.int32, (CHUNK, LANE), 0)
    s2 = lax.broadcasted_iota(jnp.int32, (CHUNK, LANE), 1) % CHUNK
    causal2 = s2 <= l2
    expand = exp_ref[...]

    gsls = [slice(g * GROUP_W, (g + 1) * GROUP_W) for g in range(SSM_GROUPS)]

    def decays(c):
        rows = pl.ds(pl.multiple_of(c * CHUNK, CHUNK), CHUNK)
        dt = dt_ref[rows, :]
        dtt2 = dtt_ref[c]
        acs = jnp.dot(tril, dt * arow_ref[...], precision=lax.Precision.HIGHEST,
                      preferred_element_type=F32)
        acst2 = jnp.dot(dtt2 * acol_ref[...], triu2, precision=lax.Precision.HIGHEST,
                        preferred_element_type=F32)
        last = acs[CHUNK - 1:CHUNK, :]
        e_acs = jnp.exp(acs).astype(BF16)
        wst = (jnp.exp(last - acs) * dt).astype(BF16)
        hi = acs.astype(BF16)
        r1 = acs - hi.astype(F32)
        mid = r1.astype(BF16)
        lo = (r1 - mid.astype(F32)).astype(BF16)
        acs_w = (jnp.dot(hi, expand, preferred_element_type=F32) + jnp.dot(mid, expand, preferred_element_type=F32)
                 + jnp.dot(lo, expand, preferred_element_type=F32))
        e_acs_w = jnp.dot(e_acs, expand, preferred_element_type=F32)
        wst_w = jnp.dot(wst, expand, preferred_element_type=F32)
        decay_w = jnp.exp(acs_w[CHUNK - 1:CHUNK, :])
        return dtt2, acst2, acs_w, e_acs_w, wst_w, decay_w

    def through_state(c, wst_w, decay_w):
        rows = pl.ds(pl.multiple_of(c * CHUNK, CHUNK), CHUNK)
        cbs, yoffs = [], []
        for g, gsl in enumerate(gsls):
            bg = bm_ref[rows, g * D_STATE:(g + 1) * D_STATE]
            cg = cm_ref[rows, g * D_STATE:(g + 1) * D_STATE]
            cbs.append(lax.dot_general(cg, bg, NT_DIMS, preferred_element_type=F32))
            stg = st_ref[:, gsl]
            yoffs.append(jnp.dot(cg, stg.astype(BF16), preferred_element_type=F32))
            xw = (xs_ref[rows, gsl].astype(F32) * wst_w[:, gsl]).astype(BF16)
            st_ref[:, gsl] = stg * decay_w[:, gsl] + lax.dot_general(bg, xw, TN_DIMS, preferred_element_type=F32)
        return cbs, yoffs

    def within_chunk(c, dtt2, acst2, acs_w, e_acs_w, cbs, yoffs):
        rows = pl.ds(pl.multiple_of(c * CHUNK, CHUNK), CHUNK)
        for g, gsl in enumerate(gsls):
            xg = xs_ref[rows, gsl]
            cb2 = jnp.concatenate([cbs[g], cbs[g]], axis=1)
            ys = []
            for jj in range(SSM_HPG // 2):
                j = g * (SSM_HPG // 2) + jj
                seg = acs_w[:, j * LANE:(j + 1) * LANE] - acst2[j:j + 1, :]
                wts = cb2 * jnp.exp(jnp.where(causal2, seg, -jnp.inf)) * dtt2[j:j + 1, :]
                xp = xg[:, jj * LANE:(jj + 1) * LANE]
                xbd = jnp.where(same_head, jnp.concatenate([xp, xp], axis=0), jnp.zeros((), BF16))
                ys.append(jnp.dot(wts.astype(BF16), xbd, preferred_element_type=F32))
            y = jnp.concatenate(ys, axis=1)
            y = y + yoffs[g] * e_acs_w[:, gsl] + xg.astype(F32) * dskip_ref[:, gsl]
            yg = y * _silu(z_ref[rows, gsl].astype(F32))
            yg_ref[rows, gsl] = _rms(yg, gn_ref[:, gsl]).astype(BF16)

    unroll = min(nc, SSD_UNROLL)

    def body(i, carry):
        cs = [i * unroll + k for k in range(unroll)]
        pre = [decays(c) for c in cs]
        mid = [through_state(c, p[4], p[5]) for c, p in zip(cs, pre)]
        for c, p, m in zip(cs, pre, mid):
            within_chunk(c, p[0], p[1], p[2], p[3], m[0], m[1])
        return carry

    lax.fori_loop(0, nc // unroll, body, 0)

    @pl.when(step == pl.num_programs(1) - 1)
    def _():
        hout_ref[...] = st_ref[...].T


def _ssd(xbc, z, dt, dtt2, arow, acol2, expand, dskip, gn, h0t, nc):
    b, t, _ = xbc.shape
    lt = nc * CHUNK
    grid = (b, t // lt)
    state_spec = pl.BlockSpec((None, D_INNER, D_STATE), lambda i, j: (i, 0, 0))
    return pl.pallas_call(
        functools.partial(_ssd_kernel, nc=nc),
        grid=grid,
        in_specs=[
            pl.BlockSpec((None, lt, D_INNER), lambda i, j: (i, j, 0)),
            pl.BlockSpec((None, lt, BC_W), lambda i, j: (i, j, D_INNER // BC_W)),
            pl.BlockSpec((None, lt, BC_W), lambda i, j: (i, j, D_INNER // BC_W + 1)),
            pl.BlockSpec((None, lt, D_INNER), lambda i, j: (i, j, 0)),
            pl.BlockSpec((None, lt, SSM_HEADS), lambda i, j: (i, j, 0)),
            pl.BlockSpec((None, nc, SSM_HEADS // 2, 2 * CHUNK), lambda i, j: (i, j, 0, 0)),
            _const_spec((1, SSM_HEADS)),
            _const_spec((SSM_HEADS // 2, 2 * CHUNK)),
            _const_spec((SSM_HEADS, D_INNER)),
            _const_spec((1, D_INNER)),
            _const_spec((1, D_INNER)),
            state_spec,
        ],
        out_specs=[
            pl.BlockSpec((None, lt, D_INNER), lambda i, j: (i, j, 0)),
            state_spec,
        ],
        out_shape=[
            jax.ShapeDtypeStruct((b, t, D_INNER), BF16),
            jax.ShapeDtypeStruct((b, D_INNER, D_STATE), F32),
        ],
        scratch_shapes=[pltpu.VMEM((D_STATE, D_INNER), F32)],
        compiler_params=_params("parallel", "arbitrary"),
        name="ssd_scan",
    )(xbc, xbc, xbc, z, dt, dtt2, arow, acol2, expand, dskip, gn, h0t)


def _mixer_residual(mix_bf16, x, w_ref, g_ref):
    return x + _rms(jnp.dot(mix_bf16, w_ref[...], preferred_element_type=F32), g_ref[...])


def _mla_values(o_ref, wuv_ref):
    nseg = o_ref.shape[0]
    parts = []
    for h in range(MLA_HEADS):
        o_h = o_ref[0, h] if nseg == 1 else jnp.concatenate([o_ref[s, h] for s in range(nseg)], axis=0)
        parts.append(jnp.dot(o_h, wuv_ref[h], preferred_element_type=F32).astype(BF16))
    return jnp.concatenate(parts, axis=-1)


FFN_WEIGHT_SPECS = (
    (1, D_MODEL), (D_MODEL, 2 * D_FF), (FFN_CONV, 2 * D_FF), (1, 2 * D_FF), (D_FF, D_MODEL), (1, D_MODEL))


def _ffn_body(x, buf_ref, gpre_ref, wup_ref, cw_ref, cb_ref, wdn_ref, gpost_ref, nbuf_ref, ext_ref, act_ref):
    _init_carry(ext_ref, buf_ref, pl.program_id(1) == 0)
    h = _rms(x, gpre_ref[...]).astype(BF16)
    nchunk = D_FF // FFN_CHUNK
    vcols = [slice(j * FFN_CHUNK, (j + 1) * FFN_CHUNK) for j in range(nchunk)]
    gcols = [slice(D_FF + j * FFN_CHUNK, D_FF + (j + 1) * FFN_CHUNK) for j in range(nchunk)]
    up = lambda j: (jnp.dot(h, wup_ref[:, vcols[j]], preferred_element_type=F32),
                    jnp.dot(h, wup_ref[:, gcols[j]], preferred_element_type=F32))
    raw = up(0)
    for j in range(nchunk):
        nxt = up(j + 1) if j + 1 < nchunk else None
        val = _causal_conv(raw[0], ext_ref, vcols[j], cw_ref, cb_ref, FFN_CONV)
        gate = _causal_conv(raw[1], ext_ref, gcols[j], cw_ref, cb_ref, FFN_CONV)
        act_ref[:, vcols[j]] = (jax.nn.gelu(gate, approximate=True) * val).astype(BF16)
        raw = nxt
    _store_history(nbuf_ref, ext_ref)
    f = jnp.dot(act_ref[...], wdn_ref[...], preferred_element_type=F32)
    return x + _rms(f, gpost_ref[...])


def _ffn_scratch(tm, nseg):
    return [pltpu.VMEM((nseg * CARRY_ROWS, 2 * D_FF), F32), pltpu.VMEM((tm, D_FF), BF16)]


KV_WEIGHT_SPECS = ((1, D_MODEL), (D_MODEL, KV_RANK + 2 * QK_ROPE), (1, KV_RANK))


def _kv_body(x, gin_ref, w_ref, gkv_ref, cos_ref, sin_ref, lat_ref, kpe_ref, kcat_ref, latt_ref):
    h = _rms(x, gin_ref[...]).astype(BF16)
    hw = jnp.dot(h, w_ref[...], preferred_element_type=F32)
    lat = _rms(hw[:, :KV_RANK], gkv_ref[...])
    kpe = hw[:, KV_RANK:QK_CAT] * cos_ref[:, :QK_ROPE] + hw[:, QK_CAT:] * sin_ref[:, :QK_ROPE]
    lat_ref[...] = lat
    kpe_ref[...] = kpe
    kcat_ref[:, :KV_RANK] = lat.astype(BF16)
    kcat_ref[:, KV_RANK:] = kpe.astype(BF16)
    latt_ref[...] = lat.T.astype(BF16)


Q_SCALE = (QK_NOPE + QK_ROPE) ** -0.5 * LOG2E
Q_WEIGHT_SPECS = (
    (1, D_MODEL), (D_MODEL, Q_RANK), (1, Q_RANK), (Q_RANK, MLA_HEADS * QK_NOPE), (MLA_HEADS * QK_ROPE, Q_RANK),
    (MLA_HEADS * QK_ROPE, Q_RANK), (MLA_HEADS, KV_RANK, QK_NOPE))


def _q_body(x, gpre_ref, wdq_ref, gq_ref, wn_ref, wpt_ref, wprt_ref, wuk_ref, cost_ref, sint_ref, qt_ref):
    tm = x.shape[0]
    nseg = qt_ref.shape[0]
    seg_len = tm // nseg
    scale = Q_SCALE
    h = _rms(x, gpre_ref[...]).astype(BF16)
    cq = _rms(jnp.dot(h, wdq_ref[...], preferred_element_type=F32), gq_ref[...]).astype(BF16)
    qn = jnp.dot(cq, wn_ref[...], preferred_element_type=F32).astype(BF16)
    qpt = lax.dot_general(wpt_ref[...], cq, NT_DIMS, preferred_element_type=F32)
    qprt = lax.dot_general(wprt_ref[...], cq, NT_DIMS, preferred_element_type=F32)
    reps = MLA_HEADS * QK_ROPE // LANE
    cost = jnp.concatenate([cost_ref[...]] * reps, axis=0)
    sint = jnp.concatenate([sint_ref[...]] * reps, axis=0)
    qpet = ((qpt * cost + qprt * sint) * scale).astype(BF16)
    for hd in range(MLA_HEADS):
        qlt = lax.dot_general(wuk_ref[hd], qn[:, hd * QK_NOPE:(hd + 1) * QK_NOPE], NT_DIMS,
                              preferred_element_type=F32)
        qlt = (qlt * scale).astype(BF16)
        for s in range(nseg):
            src = slice(s * seg_len, (s + 1) * seg_len)
            dst = slice(hd * seg_len, (hd + 1) * seg_len)
            qt_ref[s, :KV_RANK, dst] = qlt[:, src]
            qt_ref[s, KV_RANK:, dst] = qpet[hd * QK_ROPE:(hd + 1) * QK_ROPE, src]


def _layer0_tail_kernel(*refs):
    yg_ref, x_ref, buf_ref, cos_ref, sin_ref, cost_ref, sint_ref, wout_ref, gmix_ref = refs[:9]
    ffn_w = refs[9:15]
    kv_w = refs[15:18]
    q_w = refs[18:25]
    x_out_ref, nbuf_ref, lat_ref, kpe_ref, kcat_ref, qt_ref, latt_ref, ext_ref, act_ref = refs[25:]
    x = _mixer_residual(yg_ref[...], x_ref[...], wout_ref, gmix_ref)
    x = _ffn_body(x, buf_ref, *ffn_w, nbuf_ref, ext_ref, act_ref)
    x_out_ref[...] = x
    _kv_body(x, *kv_w, cos_ref, sin_ref, lat_ref, kpe_ref, kcat_ref, latt_ref)
    _q_body(x, *q_w, cost_ref, sint_ref, qt_ref)


def _layer0_tail(yg, x, ffn_buf, cos, sin, w_out, g_mix, ffn_w, kv_w, q_w, tm):
    b, t, _ = x.shape
    grid = (b, t // tm)
    row = lambda wd: pl.BlockSpec((None, tm, wd), lambda i, j: (i, j, 0))
    nseg = ffn_buf.shape[1]
    buf_spec = pl.BlockSpec((None, nseg, FFN_CONV - 1, 2 * D_FF), lambda i, j: (i, 0, 0, 0))
    tab = pl.BlockSpec((tm, LANE), lambda i, j: (j, 0))
    tab_t = pl.BlockSpec((LANE, tm), lambda i, j: (0, j))
    weight_specs = ([_const_spec(s) for s in ((D_INNER, D_MODEL), (1, D_MODEL))]
                    + [_layer_spec(s, 0) for s in FFN_WEIGHT_SPECS]
                    + [_const_spec(s) for s in KV_WEIGHT_SPECS + Q_WEIGHT_SPECS])
    return pl.pallas_call(
        _layer0_tail_kernel,
        grid=grid,
        in_specs=[row(D_INNER), row(D_MODEL), buf_spec, tab, tab, tab_t, tab_t] + weight_specs,
        out_specs=[
            row(D_MODEL), buf_spec, row(KV_RANK), row(QK_ROPE), row(QK_CAT),
            pl.BlockSpec((None, None, nseg, QK_CAT, MLA_HEADS * tm // nseg), lambda i, j: (i, j, 0, 0, 0)),
            pl.BlockSpec((None, None, KV_RANK, tm), lambda i, j: (i, j, 0, 0)),
        ],
        out_shape=[
            jax.ShapeDtypeStruct((b, t, D_MODEL), F32),
            jax.ShapeDtypeStruct((b, nseg, FFN_CONV - 1, 2 * D_FF), F32),
            jax.ShapeDtypeStruct((b, t, KV_RANK), F32),
            jax.ShapeDtypeStruct((b, t, QK_ROPE), F32),
            jax.ShapeDtypeStruct((b, t, QK_CAT), BF16),
            jax.ShapeDtypeStruct((b, t // tm, nseg, QK_CAT, MLA_HEADS * tm // nseg), BF16),
            jax.ShapeDtypeStruct((b, t // tm, KV_RANK, tm), BF16),
        ],
        scratch_shapes=_ffn_scratch(tm, nseg),
        compiler_params=_params("parallel", "arbitrary"),
        name="layer0_tail",
    )(yg, x, ffn_buf, cos, sin, cos.T, sin.T, w_out, g_mix, *ffn_w, *kv_w, *q_w)


def _layer1_tail_kernel(*refs):
    o_ref, x_ref, buf_ref, wuv_ref, wo_ref, gmix_ref = refs[:6]
    ffn_w = refs[6:12]
    x_out_ref, nbuf_ref, ext_ref, act_ref = refs[12:]
    x = _mixer_residual(_mla_values(o_ref, wuv_ref), x_ref[...], wo_ref, gmix_ref)
    x_out_ref[...] = _ffn_body(x, buf_ref, *ffn_w, nbuf_ref, ext_ref, act_ref)


def _layer1_tail(o_lat, x, ffn_buf, wuv, wo, g_mix, ffn_w, tm):
    b, t, _ = x.shape
    grid = (b, t // tm)
    row = lambda wd: pl.BlockSpec((None, tm, wd), lambda i, j: (i, j, 0))
    nseg = ffn_buf.shape[1]
    buf_spec = pl.BlockSpec((None, nseg, FFN_CONV - 1, 2 * D_FF), lambda i, j: (i, 0, 0, 0))
    weight_specs = ([_const_spec(s) for s in ((MLA_HEADS, KV_RANK, V_DIM), (MLA_HEADS * V_DIM, D_MODEL), (1, D_MODEL))]
                    + [_layer_spec(s, 1) for s in FFN_WEIGHT_SPECS])
    return pl.pallas_call(
        _layer1_tail_kernel,
        grid=grid,
        in_specs=[pl.BlockSpec((None, nseg, MLA_HEADS, tm // nseg, KV_RANK), lambda i, j: (i, 0, 0, j, 0)),
                  row(D_MODEL), buf_spec] + weight_specs,
        out_specs=[row(D_MODEL), buf_spec],
        out_shape=[
            jax.ShapeDtypeStruct((b, t, D_MODEL), F32),
            jax.ShapeDtypeStruct((b, nseg, FFN_CONV - 1, 2 * D_FF), F32),
        ],
        scratch_shapes=_ffn_scratch(tm, nseg),
        compiler_params=_params("parallel", "arbitrary"),
        name="layer1_tail",
    )(o_lat, x, ffn_buf, wuv, wo, g_mix, *ffn_w)


def _attn_kernel(qt_ref, k_ref, vt_ref, o_ref, m_ref, l_ref, a_ref, acc_ref, s_ref, p_ref, *, tq, tk, past, nkv):
    qi = pl.program_id(1)
    rows = MLA_HEADS * tq
    nblk = rows // LANE
    blocks = [slice(b * LANE, (b + 1) * LANE) for b in range(nblk)]
    m_ref[...] = jnp.full(m_ref.shape, -jnp.inf, F32)
    l_ref[...] = jnp.zeros(l_ref.shape, F32)
    acc_ref[...] = jnp.zeros(acc_ref.shape, F32)
    n_vis = jnp.minimum(nkv, (past + (qi + 1) * tq + tk - 1) // tk)

    def scores(j):
        ks = k_ref[pl.ds(pl.multiple_of(j * tk, tk), tk), :]
        return jnp.dot(ks, qt_ref[...], preferred_element_type=F32)

    def store_scores(slot, s):
        for b, csl in enumerate(blocks):
            s_ref[slot, b] = s[:, csl]

    def hidden_bias(j):
        key_chunk = lax.broadcasted_iota(jnp.int32, (tk, LANE), 0) // CHUNK + j * (tk // CHUNK)
        lane = lax.broadcasted_iota(jnp.int32, (tk, LANE), 1)
        pats = []
        for ph in range(max(1, tq // LANE)):
            q_chunk = (past + qi * tq) // CHUNK + ((ph * LANE + lane) % tq) // CHUNK
            pats.append(jnp.where(key_chunk <= q_chunk, 0.0, -jnp.inf))
        return pats

    def absorb(j, masked):
        slot = j % 2
        pats = hidden_bias(j) if masked else None
        for b, csl in enumerate(blocks):
            s = s_ref[slot, b]
            if masked:
                s = s + pats[b % len(pats)]
            m_prev = m_ref[:, csl]
            m_new = jnp.maximum(m_prev, jnp.max(s, axis=0, keepdims=True))
            alpha = jnp.exp2(m_prev - m_new)
            p = jnp.exp2(s - m_new)
            l_ref[:, csl] = alpha * l_ref[:, csl] + jnp.sum(p, axis=0, keepdims=True)
            m_ref[:, csl] = m_new
            a_ref[:, csl] = alpha
            p_ref[b] = p.astype(BF16)
        p_all = jnp.concatenate([p_ref[b] for b in range(nblk)], axis=1)
        pv = jnp.dot(vt_ref[j], p_all, preferred_element_type=F32)
        for b, csl in enumerate(blocks):
            acc_ref[b] = a_ref[:, csl] * acc_ref[b] + pv[:, csl]

    store_scores(0, scores(0))

    def body(j, carry):
        s_new = scores(j)
        absorb(j - 1, masked=False)
        store_scores(j % 2, s_new)
        return carry

    lax.fori_loop(1, n_vis, body, 0)
    absorb(n_vis - 1, masked=True)
    o = jnp.concatenate([acc_ref[b] / l_ref[:, csl] for b, csl in enumerate(blocks)], axis=1).T
    o_ref[...] = o.astype(BF16).reshape(MLA_HEADS, tq, KV_RANK)


def _attention(qt, kcat, vt, past, tq, tk):
    b, ng, _, rows = qt.shape
    t = ng * tq
    s_len = kcat.shape[1]
    nkv = s_len // tk
    assert tk % tq == 0 and past % tk == 0 and rows == MLA_HEADS * tq and tq % CHUNK == 0
    grid = (b, ng)
    nblk = rows // LANE
    return pl.pallas_call(
        functools.partial(_attn_kernel, tq=tq, tk=tk, past=past, nkv=nkv),
        grid=grid,
        in_specs=[
            pl.BlockSpec((None, None, QK_CAT, rows), lambda i, j: (i, j, 0, 0)),
            pl.BlockSpec((None, s_len, QK_CAT), lambda i, j: (i, 0, 0)),
            pl.BlockSpec((None, nkv, KV_RANK, tk), lambda i, j: (i, 0, 0, 0)),
        ],
        out_specs=pl.BlockSpec((None, MLA_HEADS, tq, KV_RANK), lambda i, j: (i, 0, j, 0)),
        out_shape=jax.ShapeDtypeStruct((b, MLA_HEADS, t, KV_RANK), BF16),
        scratch_shapes=[
            pltpu.VMEM((1, rows), F32),
            pltpu.VMEM((1, rows), F32),
            pltpu.VMEM((1, rows), F32),
            pltpu.VMEM((nblk, KV_RANK, LANE), F32),
            pltpu.VMEM((2, nblk, tk, LANE), F32),
            pltpu.VMEM((nblk, tk, LANE), BF16),
        ],
        compiler_params=_params("parallel", "parallel"),
        name="mla_attention",
    )(qt, kcat, vt)


def _rope_tables(past, t):
    half = QK_ROPE // 2
    inv = jnp.exp(-math.log(ROPE_THETA) * jnp.arange(half, dtype=F32) / half)
    ang = (past + jnp.arange(t, dtype=jnp.int32)).astype(F32)[:, None] * inv[None, :]
    reps = 2 * LANE // QK_ROPE
    return jnp.tile(jnp.cos(ang), (1, reps)), jnp.tile(jnp.sin(ang), (1, reps))


def _rotate_half_cols(w):
    shp = w.shape
    w4 = w.reshape(shp[:-1] + (shp[-1] // QK_ROPE, 2, QK_ROPE // 2))
    return jnp.concatenate([-w4[..., 1:2, :], w4[..., 0:1, :]], axis=-2).reshape(shp)


def _tile(t, pref):
    return pref if t % pref == 0 else t


def _prep_weights(norm_mix_pre, norm_mix_post, norm_ffn_pre, norm_ffn_post,
                  ssm_w_in, ssm_conv_w, ssm_conv_b, ssm_dt_bias, ssm_a_log, ssm_d, ssm_norm, ssm_w_out,
                  kv_norm_in, kv_w_dkv, kv_norm, kv_w_kr, kv_w_uk, kv_w_uv,
                  mla_w_dq, mla_q_norm, mla_w_uq, mla_w_o,
                  ffn_w_up, ffn_conv_w, ffn_conv_b, ffn_w_down):
    w = {}
    w_in = ssm_w_in[0]
    w["wzx"] = w_in.astype(BF16)
    w["wdt"] = jnp.pad(w_in[:, D_INNER + CONV_DIM:], ((0, 0), (0, LANE - SSM_HEADS))).astype(BF16)
    w_dt_t = w_in[:, D_INNER + CONV_DIM:].T
    w["wdtt"] = jnp.concatenate([w_dt_t[0::2], w_dt_t[1::2]]).astype(BF16)
    w["ssm_cw"] = ssm_conv_w[0]
    w["ssm_cb"] = ssm_conv_b[0][None, :]
    w["dtb"] = ssm_dt_bias[0][None, :]
    w["dtbt"] = jnp.concatenate([ssm_dt_bias[0][0::2], ssm_dt_bias[0][1::2]])[:, None]
    a = -jnp.exp(ssm_a_log[0].astype(F32))
    w["arow"] = a[None, :]
    w["acol2"] = jnp.broadcast_to(a.reshape(SSM_HEADS // 2, 2, 1),
                                  (SSM_HEADS // 2, 2, CHUNK)).reshape(SSM_HEADS // 2, 2 * CHUNK)
    w["expand"] = jnp.repeat(jnp.eye(SSM_HEADS, dtype=BF16), SSM_HEADDIM, axis=1)
    w["dskip"] = jnp.repeat(ssm_d[0], SSM_HEADDIM)[None, :]
    w["gn"] = ssm_norm[0][None, :]
    w["w_out"] = ssm_w_out[0].astype(BF16)
    w["g_mix_pre"] = norm_mix_pre[:, None, :]
    w["g_mix_post"] = norm_mix_post[:, None, :]
    w["g_ffn_pre"] = norm_ffn_pre[:, None, :]
    w["g_ffn_post"] = norm_ffn_post[:, None, :]
    w["ffn_up"] = ffn_w_up.astype(BF16)
    w["ffn_cw"] = ffn_conv_w
    w["ffn_cb"] = ffn_conv_b[:, None, :]
    w["ffn_down"] = ffn_w_down.astype(BF16)
    w["kv_gin"] = kv_norm_in[None, :]
    w["kv_w"] = jnp.concatenate([kv_w_dkv, kv_w_kr, _rotate_half_cols(kv_w_kr)], axis=-1).astype(BF16)
    w["kv_g"] = kv_norm[None, :]
    w["wdq"] = mla_w_dq[0].astype(BF16)
    w["gq"] = mla_q_norm[0][None, :]
    wuq = mla_w_uq[0].reshape(Q_RANK, MLA_HEADS, QK_NOPE + QK_ROPE)
    w["wq_nope"] = wuq[:, :, :QK_NOPE].reshape(Q_RANK, MLA_HEADS * QK_NOPE).astype(BF16)
    wq_pe = wuq[:, :, QK_NOPE:].reshape(Q_RANK, MLA_HEADS * QK_ROPE)
    w["wq_pe_t"] = wq_pe.T.astype(BF16)
    w["wq_pe_rot_t"] = _rotate_half_cols(wq_pe).T.astype(BF16)
    w["wuk"] = jnp.transpose(kv_w_uk, (1, 0, 2)).astype(BF16)
    w["wuv"] = jnp.transpose(kv_w_uv, (1, 0, 2)).astype(BF16)
    w["wo"] = mla_w_o[0].astype(BF16)
    return w


def _trunk(x, ssm_conv_buf, ssm_state, ffn_buf, past_kcat, w):
    b, t, _ = x.shape
    past = 0 if past_kcat is None else past_kcat.shape[1]
    assert t % CHUNK == 0 and past % CHUNK == 0
    tm = TOKEN_TILE
    nseq = tm // t if t < tm and tm % t == 0 and b % (tm // t) == 0 else 1
    if t % tm != 0 and nseq == 1:
        tm = t
    bb, tt = b // nseq, nseq * t
    pack = lambda a: a.reshape((bb, tt) + a.shape[2:])
    unpack = lambda a: a.reshape((b, t) + a.shape[2:])
    pack_hist = lambda a: a.reshape((bb, nseq) + a.shape[1:])
    unpack_hist = lambda a: a.reshape((b,) + a.shape[2:])

    z, xbc, dt, dtt2, new_conv = _mamba_in(pack(x), pack_hist(ssm_conv_buf), w["g_mix_pre"][0], w["wzx"], w["wdt"],
                                           w["wdtt"], w["ssm_cw"], w["ssm_cb"], w["dtb"], w["dtbt"], tm)
    nc = _tile(t, 256) // CHUNK
    dtt2 = dtt2.reshape(b, t // CHUNK, SSM_HEADS // 2, 2 * CHUNK)
    yg, new_state = _ssd(unpack(xbc), unpack(z), unpack(dt), dtt2, w["arow"], w["acol2"], w["expand"], w["dskip"],
                         w["gn"], ssm_state.reshape(b, D_INNER, D_STATE), nc)
    new_state = new_state.reshape(b, SSM_HEADS, SSM_HEADDIM, D_STATE)

    ffn_w = lambda i: (w["g_ffn_pre"], w["ffn_up"], w["ffn_cw"], w["ffn_cb"], w["ffn_down"], w["g_ffn_post"])
    cos, sin = _rope_tables(past, t)
    x, new_ffn0, lat, kpe, kcat, q, latt = _layer0_tail(
        pack(yg), pack(x), pack_hist(ffn_buf[0]), jnp.tile(cos, (nseq, 1)), jnp.tile(sin, (nseq, 1)),
        w["w_out"], w["g_mix_post"][0], ffn_w(0), (w["kv_gin"], w["kv_w"], w["kv_g"]),
        (w["g_mix_pre"][1], w["wdq"], w["gq"], w["wq_nope"], w["wq_pe_t"], w["wq_pe_rot_t"], w["wuk"]), tm)
    lat, kpe, kcat = unpack(lat), unpack(kpe), unpack(kcat)
    tq = _tile(t, ATT_TK)
    assert tq == (t if nseq > 1 else tm)
    q = q.reshape(b, t // tq, QK_CAT, MLA_HEADS * tq)

    s_len = past + t
    s_pad = -(-s_len // ATT_TK) * ATT_TK
    if past_kcat is None and tm == ATT_TK and s_pad == s_len:
        vt = latt
    else:
        if past_kcat is not None:
            kcat = jnp.concatenate([past_kcat, kcat], axis=1)
        if s_pad != s_len:
            kcat = jnp.pad(kcat, ((0, 0), (0, s_pad - s_len), (0, 0)))
        vt = kcat[:, :, :KV_RANK].reshape(b, s_pad // ATT_TK, ATT_TK, KV_RANK).transpose(0, 1, 3, 2)
    o_lat = _attention(q, kcat, vt, past, tq, ATT_TK)
    o_lat = o_lat.reshape(bb, nseq, MLA_HEADS, t, KV_RANK)
    x, new_ffn1 = _layer1_tail(o_lat, x, pack_hist(ffn_buf[1]), w["wuv"], w["wo"], w["g_mix_post"][1], ffn_w(1), tm)
    return (unpack(x), new_state[None], unpack_hist(new_conv)[None],
            jnp.stack([unpack_hist(new_ffn0), unpack_hist(new_ffn1)]), lat, kpe)


def kernel(x_prompt, x_sample, state_ssm, state_ssm_conv, state_ffn_conv, cache_kv_latent, cache_k_rope, norm_mix_pre, norm_mix_post, norm_ffn_pre, norm_ffn_post, ssm_w_in, ssm_conv_w, ssm_conv_b, ssm_dt_bias, ssm_a_log, ssm_d, ssm_norm, ssm_w_out, kv_norm_in, kv_w_dkv, kv_norm, kv_w_kr, kv_w_uk, kv_w_uv, mla_w_dq, mla_q_norm, mla_w_uq, mla_w_o, ffn_w_up, ffn_conv_w, ffn_conv_b, ffn_w_down):
    w = _prep_weights(norm_mix_pre, norm_mix_post, norm_ffn_pre, norm_ffn_post,
                      ssm_w_in, ssm_conv_w, ssm_conv_b, ssm_dt_bias, ssm_a_log, ssm_d, ssm_norm, ssm_w_out,
                      kv_norm_in, kv_w_dkv, kv_norm, kv_w_kr, kv_w_uk, kv_w_uv,
                      mla_w_dq, mla_q_norm, mla_w_uq, mla_w_o,
                      ffn_w_up, ffn_conv_w, ffn_conv_b, ffn_w_down)
    bp = x_prompt.shape[0]
    dtp = x_prompt.dtype
    y_p, p_ssm, p_conv, p_ffn, p_lat, p_kpe = _trunk(
        x_prompt,
        jnp.zeros((bp, SSM_CONV - 1, CONV_DIM), dtp),
        jnp.zeros((bp, SSM_HEADS, SSM_HEADDIM, D_STATE), dtp),
        jnp.zeros((2, bp, FFN_CONV - 1, 2 * D_FF), dtp),
        None, w)
    past_kcat = jnp.concatenate([cache_kv_latent, cache_k_rope], axis=-1).astype(BF16)
    y_s, s_ssm, s_conv, s_ffn, s_lat, s_kpe = _trunk(
        x_sample, state_ssm_conv[0], state_ssm[0], state_ffn_conv, past_kcat, w)
    return (y_p, y_s, p_ssm, p_conv, p_ffn, p_lat, p_kpe, s_ssm, s_conv, s_ffn, s_lat, s_kpe)
```

```python
import functools
import math

import jax
import jax.numpy as jnp
from jax import lax
from jax.experimental import pallas as pl
from jax.experimental.pallas import tpu as pltpu

F32 = jnp.float32
BF16 = jnp.bfloat16

D_MODEL = 1024
CHUNK = 64
EPS = 1e-6
D_INNER = 2048
SSM_HEADDIM = 64
SSM_HEADS = 32
SSM_GROUPS = 4
SSM_HPG = 8
D_STATE = 128
SSM_CONV = 4
GROUP_W = D_INNER // SSM_GROUPS
BC_W = SSM_GROUPS * D_STATE
CONV_DIM = D_INNER + 2 * BC_W
MLA_HEADS = 16
Q_RANK = 384
KV_RANK = 256
QK_NOPE = 128
QK_ROPE = 64
V_DIM = 128
QK_CAT = KV_RANK + QK_ROPE
ROPE_THETA = 10000.0
D_FF = 2816
FFN_CONV = 3
LANE = 128
CARRY_ROWS = 8
TOKEN_TILE = 256
COL_CHUNK = 512
FFN_CHUNK = 256
SSD_TILE = 512
SSD_UNROLL = 4
ATT_TK = 256
LOG2E = 1.4426950408889634
VMEM_LIMIT = 56 * 1024 * 1024

NT_DIMS = (((1,), (1,)), ((), ()))
TN_DIMS = (((0,), (0,)), ((), ()))


def _rms(x, g):
    return x * lax.rsqrt(jnp.mean(x * x, axis=-1, keepdims=True) + EPS) * g


def _silu(x):
    return x * (1.0 / (1.0 + jnp.exp(-x)))


def _softplus(x):
    return jnp.maximum(x, 0.0) + jnp.log(1.0 + jnp.exp(-jnp.abs(x)))


def _const_spec(shape):
    zeros = (0,) * len(shape)
    return pl.BlockSpec(shape, lambda *_: zeros, pipeline_mode=pl.Buffered(1))


def _layer_spec(shape, layer):
    zeros = (0,) * len(shape)
    return pl.BlockSpec((None,) + tuple(shape), lambda *_: (layer,) + zeros, pipeline_mode=pl.Buffered(1))


def _params(*sem):
    return pltpu.CompilerParams(dimension_semantics=sem, vmem_limit_bytes=VMEM_LIMIT)


def _init_carry(carry_ref, hist_ref, first_step):
    nseg, hist, _ = hist_ref.shape

    @pl.when(first_step)
    def _():
        carry_ref[...] = jnp.zeros(carry_ref.shape, F32)
        for s in range(nseg):
            carry_ref[(s + 1) * CARRY_ROWS - hist:(s + 1) * CARRY_ROWS, :] = hist_ref[s]


def _store_history(nbuf_ref, carry_ref):
    nseg, hist, _ = nbuf_ref.shape
    for s in range(nseg):
        nbuf_ref[s] = carry_ref[(s + 1) * CARRY_ROWS - hist:(s + 1) * CARRY_ROWS, :]


def _causal_conv(raw, carry_ref, cols, w_ref, b_ref, taps):
    nseg = carry_ref.shape[0] // CARRY_ROWS
    seg_len = raw.shape[0] // nseg
    hist = taps - 1
    outs = []
    for s in range(nseg):
        cur = raw[s * seg_len:(s + 1) * seg_len, :]
        prev = carry_ref[s * CARRY_ROWS:(s + 1) * CARRY_ROWS, cols]
        ext = jnp.concatenate([prev, cur], axis=0)
        acc = b_ref[:, cols] + cur * w_ref[hist:hist + 1, cols]
        for k in range(1, taps):
            acc = acc + ext[CARRY_ROWS - k:CARRY_ROWS - k + seg_len, :] * w_ref[hist - k:hist - k + 1, cols]
        carry_ref[s * CARRY_ROWS:(s + 1) * CARRY_ROWS, cols] = cur[seg_len - CARRY_ROWS:, :]
        outs.append(acc)
    return outs[0] if nseg == 1 else jnp.concatenate(outs, axis=0)


def _mamba_in_kernel(x_ref, buf_ref, g_ref, wzx_ref, wdt_ref, wdtt_ref, cw_ref, cb_ref, dtb_ref, dtbt_ref,
                     z_ref, xbc_ref, dt_ref, dtt_ref, nbuf_ref, ext_ref):
    _init_carry(ext_ref, buf_ref, pl.program_id(1) == 0)
    h = _rms(x_ref[...], g_ref[...]).astype(BF16)
    for j in range(D_INNER // COL_CHUNK):
        cols = slice(j * COL_CHUNK, (j + 1) * COL_CHUNK)
        z_ref[:, cols] = jnp.dot(h, wzx_ref[:, cols], preferred_element_type=F32).astype(BF16)
    nchunk = CONV_DIM // COL_CHUNK
    proj = lambda j: jnp.dot(h, wzx_ref[:, D_INNER + j * COL_CHUNK:D_INNER + (j + 1) * COL_CHUNK],
                             preferred_element_type=F32)
    raw = proj(0)
    for j in range(nchunk):
        nxt = proj(j + 1) if j + 1 < nchunk else None
        cols = slice(j * COL_CHUNK, (j + 1) * COL_CHUNK)
        xbc_ref[:, cols] = _silu(_causal_conv(raw, ext_ref, cols, cw_ref, cb_ref, SSM_CONV)).astype(BF16)
        raw = nxt
    _store_history(nbuf_ref, ext_ref)
    dt = jnp.dot(h, wdt_ref[...], preferred_element_type=F32)[:, :SSM_HEADS]
    dt_ref[...] = _softplus(dt + dtb_ref[...])
    dtt = _softplus(lax.dot_general(wdtt_ref[...], h, NT_DIMS, preferred_element_type=F32) + dtbt_ref[...])
    for c in range(h.shape[0] // CHUNK):
        for i in range(2):
            dtt_ref[c, :, i * CHUNK:(i + 1) * CHUNK] = dtt[i * (SSM_HEADS // 2):(i + 1) * (SSM_HEADS // 2),
                                                          c * CHUNK:(c + 1) * CHUNK]


def _mamba_in(x, conv_buf, g, wzx, wdt, wdtt, cw, cb, dtb, dtbt, tm):
    b, t, _ = x.shape
    nseg = conv_buf.shape[1]
    grid = (b, t // tm)
    row = lambda w: pl.BlockSpec((None, tm, w), lambda i, j: (i, j, 0))
    return pl.pallas_call(
        _mamba_in_kernel,
        grid=grid,
        in_specs=[
            row(D_MODEL),
            pl.BlockSpec((None, nseg, SSM_CONV - 1, CONV_DIM), lambda i, j: (i, 0, 0, 0)),
            _const_spec((1, D_MODEL)),
            _const_spec((D_MODEL, D_INNER + CONV_DIM + SSM_HEADS)),
            _const_spec((D_MODEL, LANE)),
            _const_spec((SSM_HEADS, D_MODEL)),
            _const_spec((SSM_CONV, CONV_DIM)),
            _const_spec((1, CONV_DIM)),
            _const_spec((1, SSM_HEADS)),
            _const_spec((SSM_HEADS, 1)),
        ],
        out_specs=[
            row(D_INNER),
            row(CONV_DIM),
            row(SSM_HEADS),
            pl.BlockSpec((None, tm // CHUNK, SSM_HEADS // 2, 2 * CHUNK), lambda i, j: (i, j, 0, 0)),
            pl.BlockSpec((None, nseg, SSM_CONV - 1, CONV_DIM), lambda i, j: (i, 0, 0, 0)),
        ],
        out_shape=[
            jax.ShapeDtypeStruct((b, t, D_INNER), BF16),
            jax.ShapeDtypeStruct((b, t, CONV_DIM), BF16),
            jax.ShapeDtypeStruct((b, t, SSM_HEADS), F32),
            jax.ShapeDtypeStruct((b, t // CHUNK, SSM_HEADS // 2, 2 * CHUNK), F32),
            jax.ShapeDtypeStruct((b, nseg, SSM_CONV - 1, CONV_DIM), F32),
        ],
        scratch_shapes=[pltpu.VMEM((nseg * CARRY_ROWS, CONV_DIM), F32)],
        compiler_params=_params("parallel", "arbitrary"),
        name="mamba_in",
    )(x, conv_buf, g, wzx, wdt, wdtt, cw, cb, dtb, dtbt)


def _ssd_kernel(xs_ref, bm_ref, cm_ref, z_ref, dt_ref, dtt_ref, arow_ref, acol_ref, exp_ref, dskip_ref, gn_ref,
                h0_ref, yg_ref, hout_ref, st_ref, *, nc):
    step = pl.program_id(1)

    @pl.when(step == 0)
    def _():
        st_ref[...] = h0_ref[...].T

    li = lax.broadcasted_iota(jnp.int32, (CHUNK, CHUNK), 0)
    si = lax.broadcasted_iota(jnp.int32, (CHUNK, CHUNK), 1)
    tril = (si <= li).astype(F32)
    pi = lax.broadcasted_iota(jnp.int32, (LANE, LANE), 0)
    pj = lax.broadcasted_iota(jnp.int32, (LANE, LANE), 1)
    same_head = (pi // CHUNK) == (pj // CHUNK)
    triu2 = jnp.where(same_head, (pi <= pj).astype(F32), 0.0)
    l2 = lax.broadcasted_iota(jnp.int32, (CHUNK, LANE), 0)
    s2 = lax.broadcasted_iota(jnp.int32, (CHUNK, LANE), 1) % CHUNK
    causal2 = s2 <= l2
    expand = exp_ref[...]

    gsls = [slice(g * GROUP_W, (g + 1) * GROUP_W) for g in range(SSM_GROUPS)]

    def decays(c):
        rows = pl.ds(pl.multiple_of(c * CHUNK, CHUNK), CHUNK)
        dt = dt_ref[rows, :]
        dtt2 = dtt_ref[c]
        acs = jnp.dot(tril, dt * arow_ref[...], precision=lax.Precision.HIGHEST,
                      preferred_element_type=F32)
        acst2 = jnp.dot(dtt2 * acol_ref[...], triu2, precision=lax.Precision.HIGHEST,
                        preferred_element_type=F32)
        last = acs[CHUNK - 1:CHUNK, :]
        e_acs = jnp.exp(acs).astype(BF16)
        wst = (jnp.exp(last - acs) * dt).astype(BF16)
        hi = acs.astype(BF16)
        r1 = acs - hi.astype(F32)
        mid = r1.astype(BF16)
        lo = (r1 - mid.astype(F32)).astype(BF16)
        acs_w = (jnp.dot(hi, expand, preferred_element_type=F32) + jnp.dot(mid, expand, preferred_element_type=F32)
                 + jnp.dot(lo, expand, preferred_element_type=F32))
        e_acs_w = jnp.dot(e_acs, expand, preferred_element_type=F32)
        wst_w = jnp.dot(wst, expand, preferred_element_type=F32)
        decay_w = jnp.exp(acs_w[CHUNK - 1:CHUNK, :])
        return dtt2, acst2, acs_w, e_acs_w, wst_w, decay_w

    def through_state(c, wst_w, decay_w):
        rows = pl.ds(pl.multiple_of(c * CHUNK, CHUNK), CHUNK)
        cbs, yoffs = [], []
        for g, gsl in enumerate(gsls):
            bg = bm_ref[rows, g * D_STATE:(g + 1) * D_STATE]
            cg = cm_ref[rows, g * D_STATE:(g + 1) * D_STATE]
            cbs.append(lax.dot_general(cg, bg, NT_DIMS, preferred_element_type=F32))
            stg = st_ref[:, gsl]
            yoffs.append(jnp.dot(cg, stg.astype(BF16), preferred_element_type=F32))
            xw = (xs_ref[rows, gsl].astype(F32) * wst_w[:, gsl]).astype(BF16)
            st_ref[:, gsl] = stg * decay_w[:, gsl] + lax.dot_general(bg, xw, TN_DIMS, preferred_element_type=F32)
        return cbs, yoffs

    def within_chunk(c, dtt2, acst2, acs_w, e_acs_w, cbs, yoffs):
        rows = pl.ds(pl.multiple_of(c * CHUNK, CHUNK), CHUNK)
        for g, gsl in enumerate(gsls):
            xg = xs_ref[rows, gsl]
            cb2 = jnp.concatenate([cbs[g], cbs[g]], axis=1)
            ys = []
            for jj in range(SSM_HPG // 2):
                j = g * (SSM_HPG // 2) + jj
                seg = acs_w[:, j * LANE:(j + 1) * LANE] - acst2[j:j + 1, :]
                wts = cb2 * jnp.exp(jnp.where(causal2, seg, -jnp.inf)) * dtt2[j:j + 1, :]
                xp = xg[:, jj * LANE:(jj + 1) * LANE]
                xbd = jnp.where(same_head, jnp.concatenate([xp, xp], axis=0), jnp.zeros((), BF16))
                ys.append(jnp.dot(wts.astype(BF16), xbd, preferred_element_type=F32))
            y = jnp.concatenate(ys, axis=1)
            y = y + yoffs[g] * e_acs_w[:, gsl] + xg.astype(F32) * dskip_ref[:, gsl]
            yg = y * _silu(z_ref[rows, gsl].astype(F32))
            yg_ref[rows, gsl] = _rms(yg, gn_ref[:, gsl]).astype(BF16)

    unroll = min(nc, SSD_UNROLL)

    def body(i, carry):
        cs = [i * unroll + k for k in range(unroll)]
        pre = [decays(c) for c in cs]
        mid = [through_state(c, p[4], p[5]) for c, p in zip(cs, pre)]
        for c, p, m in zip(cs, pre, mid):
            within_chunk(c, p[0], p[1], p[2], p[3], m[0], m[1])
        return carry

    lax.fori_loop(0, nc // unroll, body, 0)

    @pl.when(step == pl.num_programs(1) - 1)
    def _():
        hout_ref[...] = st_ref[...].T


def _ssd(xbc, z, dt, dtt2, arow, acol2, expand, dskip, gn, h0t, nc):
    b, t, _ = xbc.shape
    lt = nc * CHUNK
    grid = (b, t // lt)
    state_spec = pl.BlockSpec((None, D_INNER, D_STATE), lambda i, j: (i, 0, 0))
    return pl.pallas_call(
        functools.partial(_ssd_kernel, nc=nc),
        grid=grid,
        in_specs=[
            pl.BlockSpec((None, lt, D_INNER), lambda i, j: (i, j, 0)),
            pl.BlockSpec((None, lt, BC_W), lambda i, j: (i, j, D_INNER // BC_W)),
            pl.BlockSpec((None, lt, BC_W), lambda i, j: (i, j, D_INNER // BC_W + 1)),
            pl.BlockSpec((None, lt, D_INNER), lambda i, j: (i, j, 0)),
            pl.BlockSpec((None, lt, SSM_HEADS), lambda i, j: (i, j, 0)),
            pl.BlockSpec((None, nc, SSM_HEADS // 2, 2 * CHUNK), lambda i, j: (i, j, 0, 0)),
            _const_spec((1, SSM_HEADS)),
            _const_spec((SSM_HEADS // 2, 2 * CHUNK)),
            _const_spec((SSM_HEADS, D_INNER)),
            _const_spec((1, D_INNER)),
            _const_spec((1, D_INNER)),
            state_spec,
        ],
        out_specs=[
            pl.BlockSpec((None, lt, D_INNER), lambda i, j: (i, j, 0)),
            state_spec,
        ],
        out_shape=[
            jax.ShapeDtypeStruct((b, t, D_INNER), BF16),
            jax.ShapeDtypeStruct((b, D_INNER, D_STATE), F32),
        ],
        scratch_shapes=[pltpu.VMEM((D_STATE, D_INNER), F32)],
        compiler_params=_params("parallel", "arbitrary"),
        name="ssd_scan",
    )(xbc, xbc, xbc, z, dt, dtt2, arow, acol2, expand, dskip, gn, h0t)


def _mixer_residual(mix_bf16, x, w_ref, g_ref):
    return x + _rms(jnp.dot(mix_bf16, w_ref[...], preferred_element_type=F32), g_ref[...])


def _mla_values(o_ref, wuv_ref):
    nseg = o_ref.shape[0]
    parts = []
    for h in range(MLA_HEADS):
        o_h = o_ref[0, h] if nseg == 1 else jnp.concatenate([o_ref[s, h] for s in range(nseg)], axis=0)
        parts.append(jnp.dot(o_h, wuv_ref[h], preferred_element_type=F32).astype(BF16))
    return jnp.concatenate(parts, axis=-1)


FFN_WEIGHT_SPECS = (
    (1, D_MODEL), (D_MODEL, 2 * D_FF), (FFN_CONV, 2 * D_FF), (1, 2 * D_FF), (D_FF, D_MODEL), (1, D_MODEL))


def _ffn_body(x, buf_ref, gpre_ref, wup_ref, cw_ref, cb_ref, wdn_ref, gpost_ref, nbuf_ref, ext_ref, act_ref):
    _init_carry(ext_ref, buf_ref, pl.program_id(1) == 0)
    h = _rms(x, gpre_ref[...]).astype(BF16)
    nchunk = D_FF // FFN_CHUNK
    vcols = [slice(j * FFN_CHUNK, (j + 1) * FFN_CHUNK) for j in range(nchunk)]
    gcols = [slice(D_FF + j * FFN_CHUNK, D_FF + (j + 1) * FFN_CHUNK) for j in range(nchunk)]
    up = lambda j: (jnp.dot(h, wup_ref[:, vcols[j]], preferred_element_type=F32),
                    jnp.dot(h, wup_ref[:, gcols[j]], preferred_element_type=F32))
    raw = up(0)
    for j in range(nchunk):
        nxt = up(j + 1) if j + 1 < nchunk else None
        val = _causal_conv(raw[0], ext_ref, vcols[j], cw_ref, cb_ref, FFN_CONV)
        gate = _causal_conv(raw[1], ext_ref, gcols[j], cw_ref, cb_ref, FFN_CONV)
        act_ref[:, vcols[j]] = (jax.nn.gelu(gate, approximate=True) * val).astype(BF16)
        raw = nxt
    _store_history(nbuf_ref, ext_ref)
    f = jnp.dot(act_ref[...], wdn_ref[...], preferred_element_type=F32)
    return x + _rms(f, gpost_ref[...])


def _ffn_scratch(tm, nseg):
    return [pltpu.VMEM((nseg * CARRY_ROWS, 2 * D_FF), F32), pltpu.VMEM((tm, D_FF), BF16)]


KV_WEIGHT_SPECS = ((1, D_MODEL), (D_MODEL, KV_RANK + 2 * QK_ROPE), (1, KV_RANK))


def _kv_body(x, gin_ref, w_ref, gkv_ref, cos_ref, sin_ref, lat_ref, kpe_ref, kcat_ref, latt_ref):
    h = _rms(x, gin_ref[...]).astype(BF16)
    hw = jnp.dot(h, w_ref[...], preferred_element_type=F32)
    lat = _rms(hw[:, :KV_RANK], gkv_ref[...])
    kpe = hw[:, KV_RANK:QK_CAT] * cos_ref[:, :QK_ROPE] + hw[:, QK_CAT:] * sin_ref[:, :QK_ROPE]
    lat_ref[...] = lat
    kpe_ref[...] = kpe
    kcat_ref[:, :KV_RANK] = lat.astype(BF16)
    kcat_ref[:, KV_RANK:] = kpe.astype(BF16)
    latt_ref[...] = lat.T.astype(BF16)


Q_SCALE = (QK_NOPE + QK_ROPE) ** -0.5 * LOG2E
Q_WEIGHT_SPECS = (
    (1, D_MODEL), (D_MODEL, Q_RANK), (1, Q_RANK), (Q_RANK, MLA_HEADS * QK_NOPE), (MLA_HEADS * QK_ROPE, Q_RANK),
    (MLA_HEADS * QK_ROPE, Q_RANK), (MLA_HEADS, KV_RANK, QK_NOPE))


def _q_body(x, gpre_ref, wdq_ref, gq_ref, wn_ref, wpt_ref, wprt_ref, wuk_ref, cost_ref, sint_ref, qt_ref):
    tm = x.shape[0]
    nseg = qt_ref.shape[0]
    seg_len = tm // nseg
    scale = Q_SCALE
    h = _rms(x, gpre_ref[...]).astype(BF16)
    cq = _rms(jnp.dot(h, wdq_ref[...], preferred_element_type=F32), gq_ref[...]).astype(BF16)
    qn = jnp.dot(cq, wn_ref[...], preferred_element_type=F32).astype(BF16)
    qpt = lax.dot_general(wpt_ref[...], cq, NT_DIMS, preferred_element_type=F32)
    qprt = lax.dot_general(wprt_ref[...], cq, NT_DIMS, preferred_element_type=F32)
    reps = MLA_HEADS * QK_ROPE // LANE
    cost = jnp.concatenate([cost_ref[...]] * reps, axis=0)
    sint = jnp.concatenate([sint_ref[...]] * reps, axis=0)
    qpet = ((qpt * cost + qprt * sint) * scale).astype(BF16)
    for hd in range(MLA_HEADS):
        qlt = lax.dot_general(wuk_ref[hd], qn[:, hd * QK_NOPE:(hd + 1) * QK_NOPE], NT_DIMS,
                              preferred_element_type=F32)
        qlt = (qlt * scale).astype(BF16)
        for s in range(nseg):
            src = slice(s * seg_len, (s + 1) * seg_len)
            dst = slice(hd * seg_len, (hd + 1) * seg_len)
            qt_ref[s, :KV_RANK, dst] = qlt[:, src]
            qt_ref[s, KV_RANK:, dst] = qpet[hd * QK_ROPE:(hd + 1) * QK_ROPE, src]


def _layer0_tail_kernel(*refs):
    yg_ref, x_ref, buf_ref, cos_ref, sin_ref, cost_ref, sint_ref, wout_ref, gmix_ref = refs[:9]
    ffn_w = refs[9:15]
    kv_w = refs[15:18]
    q_w = refs[18:25]
    x_out_ref, nbuf_ref, lat_ref, kpe_ref, kcat_ref, qt_ref, latt_ref, ext_ref, act_ref = refs[25:]
    x = _mixer_residual(yg_ref[...], x_ref[...], wout_ref, gmix_ref)
    x = _ffn_body(x, buf_ref, *ffn_w, nbuf_ref, ext_ref, act_ref)
    x_out_ref[...] = x
    _kv_body(x, *kv_w, cos_ref, sin_ref, lat_ref, kpe_ref, kcat_ref, latt_ref)
    _q_body(x, *q_w, cost_ref, sint_ref, qt_ref)


def _layer0_tail(yg, x, ffn_buf, cos, sin, w_out, g_mix, ffn_w, kv_w, q_w, tm):
    b, t, _ = x.shape
    grid = (b, t // tm)
    row = lambda wd: pl.BlockSpec((None, tm, wd), lambda i, j: (i, j, 0))
    nseg = ffn_buf.shape[1]
    buf_spec = pl.BlockSpec((None, nseg, FFN_CONV - 1, 2 * D_FF), lambda i, j: (i, 0, 0, 0))
    tab = pl.BlockSpec((tm, LANE), lambda i, j: (j, 0))
    tab_t = pl.BlockSpec((LANE, tm), lambda i, j: (0, j))
    weight_specs = ([_const_spec(s) for s in ((D_INNER, D_MODEL), (1, D_MODEL))]
                    + [_layer_spec(s, 0) for s in FFN_WEIGHT_SPECS]
                    + [_const_spec(s) for s in KV_WEIGHT_SPECS + Q_WEIGHT_SPECS])
    return pl.pallas_call(
        _layer0_tail_kernel,
        grid=grid,
        in_specs=[row(D_INNER), row(D_MODEL), buf_spec, tab, tab, tab_t, tab_t] + weight_specs,
        out_specs=[
            row(D_MODEL), buf_spec, row(KV_RANK), row(QK_ROPE), row(QK_CAT),
            pl.BlockSpec((None, None, nseg, QK_CAT, MLA_HEADS * tm // nseg), lambda i, j: (i, j, 0, 0, 0)),
            pl.BlockSpec((None, None, KV_RANK, tm), lambda i, j: (i, j, 0, 0)),
        ],
        out_shape=[
            jax.ShapeDtypeStruct((b, t, D_MODEL), F32),
            jax.ShapeDtypeStruct((b, nseg, FFN_CONV - 1, 2 * D_FF), F32),
            jax.ShapeDtypeStruct((b, t, KV_RANK), F32),
            jax.ShapeDtypeStruct((b, t, QK_ROPE), F32),
            jax.ShapeDtypeStruct((b, t, QK_CAT), BF16),
            jax.ShapeDtypeStruct((b, t // tm, nseg, QK_CAT, MLA_HEADS * tm // nseg), BF16),
            jax.ShapeDtypeStruct((b, t // tm, KV_RANK, tm), BF16),
        ],
        scratch_shapes=_ffn_scratch(tm, nseg),
        compiler_params=_params("parallel", "arbitrary"),
        name="layer0_tail",
    )(yg, x, ffn_buf, cos, sin, cos.T, sin.T, w_out, g_mix, *ffn_w, *kv_w, *q_w)


def _layer1_tail_kernel(*refs):
    o_ref, x_ref, buf_ref, wuv_ref, wo_ref, gmix_ref = refs[:6]
    ffn_w = refs[6:12]
    x_out_ref, nbuf_ref, ext_ref, act_ref = refs[12:]
    x = _mixer_residual(_mla_values(o_ref, wuv_ref), x_ref[...], wo_ref, gmix_ref)
    x_out_ref[...] = _ffn_body(x, buf_ref, *ffn_w, nbuf_ref, ext_ref, act_ref)


def _layer1_tail(o_lat, x, ffn_buf, wuv, wo, g_mix, ffn_w, tm):
    b, t, _ = x.shape
    grid = (b, t // tm)
    row = lambda wd: pl.BlockSpec((None, tm, wd), lambda i, j: (i, j, 0))
    nseg = ffn_buf.shape[1]
    buf_spec = pl.BlockSpec((None, nseg, FFN_CONV - 1, 2 * D_FF), lambda i, j: (i, 0, 0, 0))
    weight_specs = ([_const_spec(s) for s in ((MLA_HEADS, KV_RANK, V_DIM), (MLA_HEADS * V_DIM, D_MODEL), (1, D_MODEL))]
                    + [_layer_spec(s, 1) for s in FFN_WEIGHT_SPECS])
    return pl.pallas_call(
        _layer1_tail_kernel,
        grid=grid,
        in_specs=[pl.BlockSpec((None, nseg, MLA_HEADS, tm // nseg, KV_RANK), lambda i, j: (i, 0, 0, j, 0)),
                  row(D_MODEL), buf_spec] + weight_specs,
        out_specs=[row(D_MODEL), buf_spec],
        out_shape=[
            jax.ShapeDtypeStruct((b, t, D_MODEL), F32),
            jax.ShapeDtypeStruct((b, nseg, FFN_CONV - 1, 2 * D_FF), F32),
        ],
        scratch_shapes=_ffn_scratch(tm, nseg),
        compiler_params=_params("parallel", "arbitrary"),
        name="layer1_tail",
    )(o_lat, x, ffn_buf, wuv, wo, g_mix, *ffn_w)


def _attn_kernel(qt_ref, k_ref, *rest, tq, tk, past, nkv, values_given):
    vt_ref = rest[0] if values_given else None
    o_ref, m_ref, l_ref, a_ref, acc_ref, s_ref, p_ref = rest[1:] if values_given else rest
    qi = pl.program_id(1)
    rows = MLA_HEADS * tq
    nblk = rows // LANE
    blocks = [slice(b * LANE, (b + 1) * LANE) for b in range(nblk)]
    m_ref[...] = jnp.full(m_ref.shape, -jnp.inf, F32)
    l_ref[...] = jnp.zeros(l_ref.shape, F32)
    acc_ref[...] = jnp.zeros(acc_ref.shape, F32)
    n_vis = jnp.minimum(nkv, (past + (qi + 1) * tq + tk - 1) // tk)

    def scores(j):
        ks = k_ref[pl.ds(pl.multiple_of(j * tk, tk), tk), :]
        return jnp.dot(ks, qt_ref[...], preferred_element_type=F32)

    def store_scores(slot, s):
        for b, csl in enumerate(blocks):
            s_ref[slot, b] = s[:, csl]

    def hidden_bias(j):
        key_chunk = lax.broadcasted_iota(jnp.int32, (tk, LANE), 0) // CHUNK + j * (tk // CHUNK)
        lane = lax.broadcasted_iota(jnp.int32, (tk, LANE), 1)
        pats = []
        for ph in range(max(1, tq // LANE)):
            q_chunk = (past + qi * tq) // CHUNK + ((ph * LANE + lane) % tq) // CHUNK
            pats.append(jnp.where(key_chunk <= q_chunk, 0.0, -jnp.inf))
        return pats

    def absorb(j, masked):
        slot = j % 2
        pats = hidden_bias(j) if masked else None
        for b, csl in enumerate(blocks):
            s = s_ref[slot, b]
            if masked:
                s = s + pats[b % len(pats)]
            m_prev = m_ref[:, csl]
            m_new = jnp.maximum(m_prev, jnp.max(s, axis=0, keepdims=True))
            alpha = jnp.exp2(m_prev - m_new)
            p = jnp.exp2(s - m_new)
            l_ref[:, csl] = alpha * l_ref[:, csl] + jnp.sum(p, axis=0, keepdims=True)
            m_ref[:, csl] = m_new
            a_ref[:, csl] = alpha
            p_ref[b] = p.astype(BF16)
        p_all = jnp.concatenate([p_ref[b] for b in range(nblk)], axis=1)
        if values_given:
            vt = vt_ref[j]
        else:
            lat = k_ref[pl.ds(pl.multiple_of(j * tk, tk), tk), :KV_RANK]
            vt = lat.astype(F32).T.astype(BF16)
        pv = jnp.dot(vt, p_all, preferred_element_type=F32)
        for b, csl in enumerate(blocks):
            acc_ref[b] = a_ref[:, csl] * acc_ref[b] + pv[:, csl]

    store_scores(0, scores(0))

    def body(j, carry):
        s_new = scores(j)
        absorb(j - 1, masked=False)
        store_scores(j % 2, s_new)
        return carry

    lax.fori_loop(1, n_vis, body, 0)
    absorb(n_vis - 1, masked=True)
    o = jnp.concatenate([acc_ref[b] / l_ref[:, csl] for b, csl in enumerate(blocks)], axis=1).T
    o_ref[...] = o.astype(BF16).reshape(MLA_HEADS, tq, KV_RANK)


def _attention(qt, kcat, vt, past, tq, tk):
    b, ng, _, rows = qt.shape
    t = ng * tq
    s_len = kcat.shape[1]
    nkv = s_len // tk
    assert tk % tq == 0 and past % tk == 0 and rows == MLA_HEADS * tq and tq % CHUNK == 0
    grid = (b, ng)
    nblk = rows // LANE
    operands = [qt, kcat] + ([] if vt is None else [vt])
    value_specs = [] if vt is None else [pl.BlockSpec((None, nkv, KV_RANK, tk), lambda i, j: (i, 0, 0, 0))]
    return pl.pallas_call(
        functools.partial(_attn_kernel, tq=tq, tk=tk, past=past, nkv=nkv, values_given=vt is not None),
        grid=grid,
        in_specs=[
            pl.BlockSpec((None, None, QK_CAT, rows), lambda i, j: (i, j, 0, 0)),
            pl.BlockSpec((None, s_len, QK_CAT), lambda i, j: (i, 0, 0)),
        ] + value_specs,
        out_specs=pl.BlockSpec((None, MLA_HEADS, tq, KV_RANK), lambda i, j: (i, 0, j, 0)),
        out_shape=jax.ShapeDtypeStruct((b, MLA_HEADS, t, KV_RANK), BF16),
        scratch_shapes=[
            pltpu.VMEM((1, rows), F32),
            pltpu.VMEM((1, rows), F32),
            pltpu.VMEM((1, rows), F32),
            pltpu.VMEM((nblk, KV_RANK, LANE), F32),
            pltpu.VMEM((2, nblk, tk, LANE), F32),
            pltpu.VMEM((nblk, tk, LANE), BF16),
        ],
        compiler_params=_params("parallel", "parallel"),
        name="mla_attention",
    )(*operands)


def _rope_tables(past, t):
    half = QK_ROPE // 2
    inv = jnp.exp(-math.log(ROPE_THETA) * jnp.arange(half, dtype=F32) / half)
    ang = (past + jnp.arange(t, dtype=jnp.int32)).astype(F32)[:, None] * inv[None, :]
    reps = 2 * LANE // QK_ROPE
    return jnp.tile(jnp.cos(ang), (1, reps)), jnp.tile(jnp.sin(ang), (1, reps))


def _rotate_half_cols(w):
    shp = w.shape
    w4 = w.reshape(shp[:-1] + (shp[-1] // QK_ROPE, 2, QK_ROPE // 2))
    return jnp.concatenate([-w4[..., 1:2, :], w4[..., 0:1, :]], axis=-2).reshape(shp)


def _tile(t, pref):
    return pref if t % pref == 0 else t


def _prep_weights(norm_mix_pre, norm_mix_post, norm_ffn_pre, norm_ffn_post,
                  ssm_w_in, ssm_conv_w, ssm_conv_b, ssm_dt_bias, ssm_a_log, ssm_d, ssm_norm, ssm_w_out,
                  kv_norm_in, kv_w_dkv, kv_norm, kv_w_kr, kv_w_uk, kv_w_uv,
                  mla_w_dq, mla_q_norm, mla_w_uq, mla_w_o,
                  ffn_w_up, ffn_conv_w, ffn_conv_b, ffn_w_down):
    w = {}
    w_in = ssm_w_in[0]
    w["wzx"] = w_in.astype(BF16)
    w["wdt"] = jnp.pad(w_in[:, D_INNER + CONV_DIM:], ((0, 0), (0, LANE - SSM_HEADS))).astype(BF16)
    w_dt_t = w_in[:, D_INNER + CONV_DIM:].T
    w["wdtt"] = jnp.concatenate([w_dt_t[0::2], w_dt_t[1::2]]).astype(BF16)
    w["ssm_cw"] = ssm_conv_w[0]
    w["ssm_cb"] = ssm_conv_b[0][None, :]
    w["dtb"] = ssm_dt_bias[0][None, :]
    w["dtbt"] = jnp.concatenate([ssm_dt_bias[0][0::2], ssm_dt_bias[0][1::2]])[:, None]
    a = -jnp.exp(ssm_a_log[0].astype(F32))
    w["arow"] = a[None, :]
    w["acol2"] = jnp.broadcast_to(a.reshape(SSM_HEADS // 2, 2, 1),
                                  (SSM_HEADS // 2, 2, CHUNK)).reshape(SSM_HEADS // 2, 2 * CHUNK)
    w["expand"] = jnp.repeat(jnp.eye(SSM_HEADS, dtype=BF16), SSM_HEADDIM, axis=1)
    w["dskip"] = jnp.repeat(ssm_d[0], SSM_HEADDIM)[None, :]
    w["gn"] = ssm_norm[0][None, :]
    w["w_out"] = ssm_w_out[0].astype(BF16)
    w["g_mix_pre"] = norm_mix_pre[:, None, :]
    w["g_mix_post"] = norm_mix_post[:, None, :]
    w["g_ffn_pre"] = norm_ffn_pre[:, None, :]
    w["g_ffn_post"] = norm_ffn_post[:, None, :]
    w["ffn_up"] = ffn_w_up.astype(BF16)
    w["ffn_cw"] = ffn_conv_w
    w["ffn_cb"] = ffn_conv_b[:, None, :]
    w["ffn_down"] = ffn_w_down.astype(BF16)
    w["kv_gin"] = kv_norm_in[None, :]
    w["kv_w"] = jnp.concatenate([kv_w_dkv, kv_w_kr, _rotate_half_cols(kv_w_kr)], axis=-1).astype(BF16)
    w["kv_g"] = kv_norm[None, :]
    w["wdq"] = mla_w_dq[0].astype(BF16)
    w["gq"] = mla_q_norm[0][None, :]
    wuq = mla_w_uq[0].reshape(Q_RANK, MLA_HEADS, QK_NOPE + QK_ROPE)
    w["wq_nope"] = wuq[:, :, :QK_NOPE].reshape(Q_RANK, MLA_HEADS * QK_NOPE).astype(BF16)
    wq_pe = wuq[:, :, QK_NOPE:].reshape(Q_RANK, MLA_HEADS * QK_ROPE)
    w["wq_pe_t"] = wq_pe.T.astype(BF16)
    w["wq_pe_rot_t"] = _rotate_half_cols(wq_pe).T.astype(BF16)
    w["wuk"] = jnp.transpose(kv_w_uk, (1, 0, 2)).astype(BF16)
    w["wuv"] = jnp.transpose(kv_w_uv, (1, 0, 2)).astype(BF16)
    w["wo"] = mla_w_o[0].astype(BF16)
    return w


def _trunk(x, ssm_conv_buf, ssm_state, ffn_buf, past_kcat, w):
    b, t, _ = x.shape
    past = 0 if past_kcat is None else past_kcat[0].shape[1]
    assert t % CHUNK == 0 and past % CHUNK == 0
    tm = TOKEN_TILE
    nseq = tm // t if t < tm and tm % t == 0 and b % (tm // t) == 0 else 1
    if t % tm != 0 and nseq == 1:
        tm = t
    bb, tt = b // nseq, nseq * t
    pack = lambda a: a.reshape((bb, tt) + a.shape[2:])
    unpack = lambda a: a.reshape((b, t) + a.shape[2:])
    pack_hist = lambda a: a.reshape((bb, nseq) + a.shape[1:])
    unpack_hist = lambda a: a.reshape((b,) + a.shape[2:])

    z, xbc, dt, dtt2, new_conv = _mamba_in(pack(x), pack_hist(ssm_conv_buf), w["g_mix_pre"][0], w["wzx"], w["wdt"],
                                           w["wdtt"], w["ssm_cw"], w["ssm_cb"], w["dtb"], w["dtbt"], tm)
    nc = _tile(t, SSD_TILE) // CHUNK
    dtt2 = dtt2.reshape(b, t // CHUNK, SSM_HEADS // 2, 2 * CHUNK)
    yg, new_state = _ssd(unpack(xbc), unpack(z), unpack(dt), dtt2, w["arow"], w["acol2"], w["expand"], w["dskip"],
                         w["gn"], ssm_state.reshape(b, D_INNER, D_STATE), nc)
    new_state = new_state.reshape(b, SSM_HEADS, SSM_HEADDIM, D_STATE)

    ffn_w = lambda i: (w["g_ffn_pre"], w["ffn_up"], w["ffn_cw"], w["ffn_cb"], w["ffn_down"], w["g_ffn_post"])
    cos, sin = _rope_tables(past, t)
    x, new_ffn0, lat, kpe, kcat, q, latt = _layer0_tail(
        pack(yg), pack(x), pack_hist(ffn_buf[0]), jnp.tile(cos, (nseq, 1)), jnp.tile(sin, (nseq, 1)),
        w["w_out"], w["g_mix_post"][0], ffn_w(0), (w["kv_gin"], w["kv_w"], w["kv_g"]),
        (w["g_mix_pre"][1], w["wdq"], w["gq"], w["wq_nope"], w["wq_pe_t"], w["wq_pe_rot_t"], w["wuk"]), tm)
    lat, kpe, kcat = unpack(lat), unpack(kpe), unpack(kcat)
    tq = _tile(t, ATT_TK)
    assert tq == (t if nseq > 1 else tm)
    q = q.reshape(b, t // tq, QK_CAT, MLA_HEADS * tq)

    s_len = past + t
    s_pad = -(-s_len // ATT_TK) * ATT_TK
    if past_kcat is None and tm == ATT_TK and s_pad == s_len:
        vt = latt
    else:
        vt = None
        pieces = [kcat]
        if past_kcat is not None:
            pieces.insert(0, jnp.concatenate([a.astype(BF16) for a in past_kcat], axis=-1))
        if s_pad != s_len:
            pieces.append(jnp.zeros((b, s_pad - s_len, QK_CAT), BF16))
        kcat = jnp.concatenate(pieces, axis=1) if len(pieces) > 1 else kcat
    o_lat = _attention(q, kcat, vt, past, tq, ATT_TK)
    o_lat = o_lat.reshape(bb, nseq, MLA_HEADS, t, KV_RANK)
    x, new_ffn1 = _layer1_tail(o_lat, x, pack_hist(ffn_buf[1]), w["wuv"], w["wo"], w["g_mix_post"][1], ffn_w(1), tm)
    return (unpack(x), new_state[None], unpack_hist(new_conv)[None],
            jnp.stack([unpack_hist(new_ffn0), unpack_hist(new_ffn1)]), lat, kpe)


def kernel(x_prompt, x_sample, state_ssm, state_ssm_conv, state_ffn_conv, cache_kv_latent, cache_k_rope, norm_mix_pre, norm_mix_post, norm_ffn_pre, norm_ffn_post, ssm_w_in, ssm_conv_w, ssm_conv_b, ssm_dt_bias, ssm_a_log, ssm_d, ssm_norm, ssm_w_out, kv_norm_in, kv_w_dkv, kv_norm, kv_w_kr, kv_w_uk, kv_w_uv, mla_w_dq, mla_q_norm, mla_w_uq, mla_w_o, ffn_w_up, ffn_conv_w, ffn_conv_b, ffn_w_down):
    w = _prep_weights(norm_mix_pre, norm_mix_post, norm_ffn_pre, norm_ffn_post,
                      ssm_w_in, ssm_conv_w, ssm_conv_b, ssm_dt_bias, ssm_a_log, ssm_d, ssm_norm, ssm_w_out,
                      kv_norm_in, kv_w_dkv, kv_norm, kv_w_kr, kv_w_uk, kv_w_uv,
                      mla_w_dq, mla_q_norm, mla_w_uq, mla_w_o,
                      ffn_w_up, ffn_conv_w, ffn_conv_b, ffn_w_down)
    bp = x_prompt.shape[0]
    dtp = x_prompt.dtype
    y_p, p_ssm, p_conv, p_ffn, p_lat, p_kpe = _trunk(
        x_prompt,
        jnp.zeros((bp, SSM_CONV - 1, CONV_DIM), dtp),
        jnp.zeros((bp, SSM_HEADS, SSM_HEADDIM, D_STATE), dtp),
        jnp.zeros((2, bp, FFN_CONV - 1, 2 * D_FF), dtp),
        None, w)
    y_s, s_ssm, s_conv, s_ffn, s_lat, s_kpe = _trunk(
        x_sample, state_ssm_conv[0], state_ssm[0], state_ffn_conv, (cache_kv_latent, cache_k_rope), w)
    return (y_p, y_s, p_ssm, p_conv, p_ffn, p_lat, p_kpe, s_ssm, s_conv, s_ffn, s_lat, s_kpe)
```

```python
import functools
import math

import jax
import jax.numpy as jnp
from jax import lax
from jax.experimental import pallas as pl
from jax.experimental.pallas import tpu as pltpu

F32 = jnp.float32
BF16 = jnp.bfloat16

D_MODEL = 1024
CHUNK = 64
EPS = 1e-6
D_INNER = 2048
SSM_HEADDIM = 64
SSM_HEADS = 32
SSM_GROUPS = 4
SSM_HPG = 8
D_STATE = 128
SSM_CONV = 4
GROUP_W = D_INNER // SSM_GROUPS
BC_W = SSM_GROUPS * D_STATE
CONV_DIM = D_INNER + 2 * BC_W
MLA_HEADS = 16
Q_RANK = 384
KV_RANK = 256
QK_NOPE = 128
QK_ROPE = 64
V_DIM = 128
QK_CAT = KV_RANK + QK_ROPE
ROPE_THETA = 10000.0
D_FF = 2816
FFN_CONV = 3
LANE = 128
CARRY_ROWS = 8
TOKEN_TILE = 256
COL_CHUNK = 512
FFN_CHUNK = 256
SSD_TILE = 512
SSD_UNROLL = 4
ATT_TK = 256
ATT_PAIR_MAX_TQ = 64
ATT_SIDE_BY_SIDE = 4
LOG2E = 1.4426950408889634
VMEM_LIMIT = 56 * 1024 * 1024

NT_DIMS = (((1,), (1,)), ((), ()))
TN_DIMS = (((0,), (0,)), ((), ()))


def _rms(x, g):
    return x * lax.rsqrt(jnp.mean(x * x, axis=-1, keepdims=True) + EPS) * g


def _silu(x):
    return x * (1.0 / (1.0 + jnp.exp(-x)))


def _softplus(x):
    return jnp.maximum(x, 0.0) + jnp.log(1.0 + jnp.exp(-jnp.abs(x)))


def _const_spec(shape):
    zeros = (0,) * len(shape)
    return pl.BlockSpec(shape, lambda *_: zeros, pipeline_mode=pl.Buffered(1))


def _layer_spec(shape, layer):
    zeros = (0,) * len(shape)
    return pl.BlockSpec((None,) + tuple(shape), lambda *_: (layer,) + zeros, pipeline_mode=pl.Buffered(1))


def _params(*sem):
    return pltpu.CompilerParams(dimension_semantics=sem, vmem_limit_bytes=VMEM_LIMIT)


def _init_carry(carry_ref, hist_ref, first_step):
    nseg, hist, _ = hist_ref.shape

    @pl.when(first_step)
    def _():
        carry_ref[...] = jnp.zeros(carry_ref.shape, F32)
        for s in range(nseg):
            carry_ref[(s + 1) * CARRY_ROWS - hist:(s + 1) * CARRY_ROWS, :] = hist_ref[s]


def _store_history(nbuf_ref, carry_ref):
    nseg, hist, _ = nbuf_ref.shape
    for s in range(nseg):
        nbuf_ref[s] = carry_ref[(s + 1) * CARRY_ROWS - hist:(s + 1) * CARRY_ROWS, :]


def _causal_conv(raw, carry_ref, cols, w_ref, b_ref, taps):
    nseg = carry_ref.shape[0] // CARRY_ROWS
    seg_len = raw.shape[0] // nseg
    hist = taps - 1
    outs = []
    for s in range(nseg):
        cur = raw[s * seg_len:(s + 1) * seg_len, :]
        prev = carry_ref[s * CARRY_ROWS:(s + 1) * CARRY_ROWS, cols]
        ext = jnp.concatenate([prev, cur], axis=0)
        acc = b_ref[:, cols] + cur * w_ref[hist:hist + 1, cols]
        for k in range(1, taps):
            acc = acc + ext[CARRY_ROWS - k:CARRY_ROWS - k + seg_len, :] * w_ref[hist - k:hist - k + 1, cols]
        carry_ref[s * CARRY_ROWS:(s + 1) * CARRY_ROWS, cols] = cur[seg_len - CARRY_ROWS:, :]
        outs.append(acc)
    return outs[0] if nseg == 1 else jnp.concatenate(outs, axis=0)


def _mamba_in_kernel(x_ref, buf_ref, g_ref, wzx_ref, wdt_ref, wdtt_ref, cw_ref, cb_ref, dtb_ref, dtbt_ref,
                     z_ref, xbc_ref, dt_ref, dtt_ref, nbuf_ref, ext_ref):
    _init_carry(ext_ref, buf_ref, pl.program_id(1) == 0)
    h = _rms(x_ref[...], g_ref[...]).astype(BF16)
    for j in range(D_INNER // COL_CHUNK):
        cols = slice(j * COL_CHUNK, (j + 1) * COL_CHUNK)
        z_ref[:, cols] = jnp.dot(h, wzx_ref[:, cols], preferred_element_type=F32).astype(BF16)
    nchunk = CONV_DIM // COL_CHUNK
    proj = lambda j: jnp.dot(h, wzx_ref[:, D_INNER + j * COL_CHUNK:D_INNER + (j + 1) * COL_CHUNK],
                             preferred_element_type=F32)
    raw = proj(0)
    for j in range(nchunk):
        nxt = proj(j + 1) if j + 1 < nchunk else None
        cols = slice(j * COL_CHUNK, (j + 1) * COL_CHUNK)
        xbc_ref[:, cols] = _silu(_causal_conv(raw, ext_ref, cols, cw_ref, cb_ref, SSM_CONV)).astype(BF16)
        raw = nxt
    _store_history(nbuf_ref, ext_ref)
    dt = jnp.dot(h, wdt_ref[...], preferred_element_type=F32)[:, :SSM_HEADS]
    dt_ref[...] = _softplus(dt + dtb_ref[...])
    dtt = _softplus(lax.dot_general(wdtt_ref[...], h, NT_DIMS, preferred_element_type=F32) + dtbt_ref[...])
    for c in range(h.shape[0] // CHUNK):
        for i in range(2):
            dtt_ref[c, :, i * CHUNK:(i + 1) * CHUNK] = dtt[i * (SSM_HEADS // 2):(i + 1) * (SSM_HEADS // 2),
                                                          c * CHUNK:(c + 1) * CHUNK]


def _mamba_in(x, conv_buf, g, wzx, wdt, wdtt, cw, cb, dtb, dtbt, tm):
    b, t, _ = x.shape
    nseg = conv_buf.shape[1]
    grid = (b, t // tm)
    row = lambda w: pl.BlockSpec((None, tm, w), lambda i, j: (i, j, 0))
    return pl.pallas_call(
        _mamba_in_kernel,
        grid=grid,
        in_specs=[
            row(D_MODEL),
            pl.BlockSpec((None, nseg, SSM_CONV - 1, CONV_DIM), lambda i, j: (i, 0, 0, 0)),
            _const_spec((1, D_MODEL)),
            _const_spec((D_MODEL, D_INNER + CONV_DIM + SSM_HEADS)),
            _const_spec((D_MODEL, LANE)),
            _const_spec((SSM_HEADS, D_MODEL)),
            _const_spec((SSM_CONV, CONV_DIM)),
            _const_spec((1, CONV_DIM)),
            _const_spec((1, SSM_HEADS)),
            _const_spec((SSM_HEADS, 1)),
        ],
        out_specs=[
            row(D_INNER),
            row(CONV_DIM),
            row(SSM_HEADS),
            pl.BlockSpec((None, tm // CHUNK, SSM_HEADS // 2, 2 * CHUNK), lambda i, j: (i, j, 0, 0)),
            pl.BlockSpec((None, nseg, SSM_CONV - 1, CONV_DIM), lambda i, j: (i, 0, 0, 0)),
        ],
        out_shape=[
            jax.ShapeDtypeStruct((b, t, D_INNER), BF16),
            jax.ShapeDtypeStruct((b, t, CONV_DIM), BF16),
            jax.ShapeDtypeStruct((b, t, SSM_HEADS), F32),
            jax.ShapeDtypeStruct((b, t // CHUNK, SSM_HEADS // 2, 2 * CHUNK), F32),
            jax.ShapeDtypeStruct((b, nseg, SSM_CONV - 1, CONV_DIM), F32),
        ],
        scratch_shapes=[pltpu.VMEM((nseg * CARRY_ROWS, CONV_DIM), F32)],
        compiler_params=_params("parallel", "arbitrary"),
        name="mamba_in",
    )(x, conv_buf, g, wzx, wdt, wdtt, cw, cb, dtb, dtbt)


def _ssd_kernel(xs_ref, bm_ref, cm_ref, z_ref, dt_ref, dtt_ref, arow_ref, acol_ref, exp_ref, dskip_ref, gn_ref,
                *rest, nc, has_state):
    h0_ref = rest[0] if has_state else None
    yg_ref, hout_ref, st_ref = rest[1:] if has_state else rest
    step = pl.program_id(1)

    @pl.when(step == 0)
    def _():
        st_ref[...] = h0_ref[...].T if has_state else jnp.zeros(st_ref.shape, F32)

    li = lax.broadcasted_iota(jnp.int32, (CHUNK, CHUNK), 0)
    si = lax.broadcasted_iota(jnp.int32, (CHUNK, CHUNK), 1)
    tril = (si <= li).astype(F32)
    pi = lax.broadcasted_iota(jnp.int32, (LANE, LANE), 0)
    pj = lax.broadcasted_iota(jnp.int32, (LANE, LANE), 1)
    same_head = (pi // CHUNK) == (pj // CHUNK)
    triu2 = jnp.where(same_head, (pi <= pj).astype(F32), 0.0)
    l2 = lax.broadcasted_iota(jnp.int32, (CHUNK, LANE), 0)
    s2 = lax.broadcasted_iota(jnp.int32, (CHUNK, LANE), 1) % CHUNK
    causal2 = s2 <= l2
    expand = exp_ref[...]

    gsls = [slice(g * GROUP_W, (g + 1) * GROUP_W) for g in range(SSM_GROUPS)]

    def decays(c):
        rows = pl.ds(pl.multiple_of(c * CHUNK, CHUNK), CHUNK)
        dt = dt_ref[rows, :]
        dtt2 = dtt_ref[c]
        acs = jnp.dot(tril, dt * arow_ref[...], precision=lax.Precision.HIGHEST,
                      preferred_element_type=F32)
        acst2 = jnp.dot(dtt2 * acol_ref[...], triu2, precision=lax.Precision.HIGHEST,
                        preferred_element_type=F32)
        last = acs[CHUNK - 1:CHUNK, :]
        wst = (jnp.exp(last - acs) * dt).astype(BF16)
        hi = acs.astype(BF16)
        r1 = acs - hi.astype(F32)
        mid = r1.astype(BF16)
        lo = (r1 - mid.astype(F32)).astype(BF16)
        acs_w = (jnp.dot(hi, expand, preferred_element_type=F32) + jnp.dot(mid, expand, preferred_element_type=F32)
                 + jnp.dot(lo, expand, preferred_element_type=F32))
        e_acs_w = jnp.exp(acs_w)
        wst_w = jnp.dot(wst, expand, preferred_element_type=F32)
        decay_w = jnp.exp(acs_w[CHUNK - 1:CHUNK, :])
        return dtt2, acst2, acs_w, e_acs_w, wst_w, decay_w

    def through_state(c, wst_w, decay_w):
        rows = pl.ds(pl.multiple_of(c * CHUNK, CHUNK), CHUNK)
        cbs, yoffs = [], []
        for g, gsl in enumerate(gsls):
            bg = bm_ref[rows, g * D_STATE:(g + 1) * D_STATE]
            cg = cm_ref[rows, g * D_STATE:(g + 1) * D_STATE]
            cbs.append(lax.dot_general(cg, bg, NT_DIMS, preferred_element_type=F32))
            stg = st_ref[:, gsl]
            yoffs.append(jnp.dot(cg, stg.astype(BF16), preferred_element_type=F32))
            xw = (xs_ref[rows, gsl].astype(F32) * wst_w[:, gsl]).astype(BF16)
            st_ref[:, gsl] = stg * decay_w[:, gsl] + lax.dot_general(bg, xw, TN_DIMS, preferred_element_type=F32)
        return cbs, yoffs

    def within_chunk(c, dtt2, acst2, acs_w, e_acs_w, cbs, yoffs):
        rows = pl.ds(pl.multiple_of(c * CHUNK, CHUNK), CHUNK)
        for g, gsl in enumerate(gsls):
            xg = xs_ref[rows, gsl]
            cb2 = jnp.concatenate([cbs[g], cbs[g]], axis=1)
            ys = []
            for jj in range(SSM_HPG // 2):
                j = g * (SSM_HPG // 2) + jj
                seg = acs_w[:, j * LANE:(j + 1) * LANE] - acst2[j:j + 1, :]
                wts = cb2 * jnp.exp(jnp.where(causal2, seg, -jnp.inf)) * dtt2[j:j + 1, :]
                xp = xg[:, jj * LANE:(jj + 1) * LANE]
                xbd = jnp.where(same_head, jnp.concatenate([xp, xp], axis=0), jnp.zeros((), BF16))
                ys.append(jnp.dot(wts.astype(BF16), xbd, preferred_element_type=F32))
            y = jnp.concatenate(ys, axis=1)
            y = y + yoffs[g] * e_acs_w[:, gsl] + xg.astype(F32) * dskip_ref[:, gsl]
            yg = y * _silu(z_ref[rows, gsl].astype(F32))
            yg_ref[rows, gsl] = _rms(yg, gn_ref[:, gsl]).astype(BF16)

    unroll = min(nc, SSD_UNROLL)

    def body(i, carry):
        cs = [i * unroll + k for k in range(unroll)]
        pre = [decays(c) for c in cs]
        mid = [through_state(c, p[4], p[5]) for c, p in zip(cs, pre)]
        for c, p, m in zip(cs, pre, mid):
            within_chunk(c, p[0], p[1], p[2], p[3], m[0], m[1])
        return carry

    lax.fori_loop(0, nc // unroll, body, 0)

    @pl.when(step == pl.num_programs(1) - 1)
    def _():
        hout_ref[...] = st_ref[...].T


def _ssd(xbc, z, dt, dtt2, arow, acol2, expand, dskip, gn, h0t, nc):
    b, t, _ = xbc.shape
    lt = nc * CHUNK
    grid = (b, t // lt)
    state_spec = pl.BlockSpec((None, D_INNER, D_STATE), lambda i, j: (i, 0, 0))
    state_in = [] if h0t is None else [h0t]
    return pl.pallas_call(
        functools.partial(_ssd_kernel, nc=nc, has_state=h0t is not None),
        grid=grid,
        in_specs=[
            pl.BlockSpec((None, lt, D_INNER), lambda i, j: (i, j, 0)),
            pl.BlockSpec((None, lt, BC_W), lambda i, j: (i, j, D_INNER // BC_W)),
            pl.BlockSpec((None, lt, BC_W), lambda i, j: (i, j, D_INNER // BC_W + 1)),
            pl.BlockSpec((None, lt, D_INNER), lambda i, j: (i, j, 0)),
            pl.BlockSpec((None, lt, SSM_HEADS), lambda i, j: (i, j, 0)),
            pl.BlockSpec((None, nc, SSM_HEADS // 2, 2 * CHUNK), lambda i, j: (i, j, 0, 0)),
            _const_spec((1, SSM_HEADS)),
            _const_spec((SSM_HEADS // 2, 2 * CHUNK)),
            _const_spec((SSM_HEADS, D_INNER)),
            _const_spec((1, D_INNER)),
            _const_spec((1, D_INNER)),
        ] + [state_spec] * len(state_in),
        out_specs=[
            pl.BlockSpec((None, lt, D_INNER), lambda i, j: (i, j, 0)),
            state_spec,
        ],
        out_shape=[
            jax.ShapeDtypeStruct((b, t, D_INNER), BF16),
            jax.ShapeDtypeStruct((b, D_INNER, D_STATE), F32),
        ],
        scratch_shapes=[pltpu.VMEM((D_STATE, D_INNER), F32)],
        compiler_params=_params("parallel", "arbitrary"),
        name="ssd_scan",
    )(xbc, xbc, xbc, z, dt, dtt2, arow, acol2, expand, dskip, gn, *state_in)


def _mixer_residual(mix_bf16, x, w_ref, g_ref):
    return x + _rms(jnp.dot(mix_bf16, w_ref[...], preferred_element_type=F32), g_ref[...])


def _mla_values(o_ref, wuv_ref):
    nseg = o_ref.shape[0]
    parts = []
    for h in range(MLA_HEADS):
        o_h = o_ref[0, h] if nseg == 1 else jnp.concatenate([o_ref[s, h] for s in range(nseg)], axis=0)
        parts.append(jnp.dot(o_h, wuv_ref[h], preferred_element_type=F32).astype(BF16))
    return jnp.concatenate(parts, axis=-1)


FFN_WEIGHT_SPECS = (
    (1, D_MODEL), (D_MODEL, 2 * D_FF), (FFN_CONV, 2 * D_FF), (1, 2 * D_FF), (D_FF, D_MODEL), (1, D_MODEL))


def _ffn_body(x, buf_ref, gpre_ref, wup_ref, cw_ref, cb_ref, wdn_ref, gpost_ref, nbuf_ref, ext_ref, act_ref):
    _init_carry(ext_ref, buf_ref, pl.program_id(1) == 0)
    h = _rms(x, gpre_ref[...]).astype(BF16)
    nchunk = D_FF // FFN_CHUNK
    vcols = [slice(j * FFN_CHUNK, (j + 1) * FFN_CHUNK) for j in range(nchunk)]
    gcols = [slice(D_FF + j * FFN_CHUNK, D_FF + (j + 1) * FFN_CHUNK) for j in range(nchunk)]
    up = lambda j: (jnp.dot(h, wup_ref[:, vcols[j]], preferred_element_type=F32),
                    jnp.dot(h, wup_ref[:, gcols[j]], preferred_element_type=F32))
    raw = up(0)
    for j in range(nchunk):
        nxt = up(j + 1) if j + 1 < nchunk else None
        val = _causal_conv(raw[0], ext_ref, vcols[j], cw_ref, cb_ref, FFN_CONV)
        gate = _causal_conv(raw[1], ext_ref, gcols[j], cw_ref, cb_ref, FFN_CONV)
        act_ref[:, vcols[j]] = (jax.nn.gelu(gate, approximate=True) * val).astype(BF16)
        raw = nxt
    _store_history(nbuf_ref, ext_ref)
    f = jnp.dot(act_ref[...], wdn_ref[...], preferred_element_type=F32)
    return x + _rms(f, gpost_ref[...])


def _ffn_scratch(tm, nseg):
    return [pltpu.VMEM((nseg * CARRY_ROWS, 2 * D_FF), F32), pltpu.VMEM((tm, D_FF), BF16)]


KV_WEIGHT_SPECS = ((1, D_MODEL), (D_MODEL, KV_RANK + 2 * QK_ROPE), (1, KV_RANK))


def _kv_body(x, gin_ref, w_ref, gkv_ref, cos_ref, sin_ref, lat_ref, kpe_ref, kcat_ref, latt_ref):
    h = _rms(x, gin_ref[...]).astype(BF16)
    hw = jnp.dot(h, w_ref[...], preferred_element_type=F32)
    lat = _rms(hw[:, :KV_RANK], gkv_ref[...])
    kpe = hw[:, KV_RANK:QK_CAT] * cos_ref[:, :QK_ROPE] + hw[:, QK_CAT:] * sin_ref[:, :QK_ROPE]
    lat_ref[...] = lat
    kpe_ref[...] = kpe
    kcat_ref[:, :KV_RANK] = lat.astype(BF16)
    kcat_ref[:, KV_RANK:] = kpe.astype(BF16)
    latt_ref[...] = lat.T.astype(BF16)


Q_SCALE = (QK_NOPE + QK_ROPE) ** -0.5 * LOG2E
Q_WEIGHT_SPECS = (
    (1, D_MODEL), (D_MODEL, Q_RANK), (1, Q_RANK), (Q_RANK, MLA_HEADS * QK_NOPE), (MLA_HEADS * QK_ROPE, Q_RANK),
    (MLA_HEADS * QK_ROPE, Q_RANK), (MLA_HEADS, KV_RANK, QK_NOPE))


def _q_body(x, gpre_ref, wdq_ref, gq_ref, wn_ref, wpt_ref, wprt_ref, wuk_ref, cost_ref, sint_ref, qt_ref):
    tm = x.shape[0]
    nseg = qt_ref.shape[0]
    seg_len = tm // nseg
    scale = Q_SCALE
    h = _rms(x, gpre_ref[...]).astype(BF16)
    cq = _rms(jnp.dot(h, wdq_ref[...], preferred_element_type=F32), gq_ref[...]).astype(BF16)
    qn = jnp.dot(cq, wn_ref[...], preferred_element_type=F32).astype(BF16)
    qpt = lax.dot_general(wpt_ref[...], cq, NT_DIMS, preferred_element_type=F32)
    qprt = lax.dot_general(wprt_ref[...], cq, NT_DIMS, preferred_element_type=F32)
    reps = MLA_HEADS * QK_ROPE // LANE
    cost = jnp.concatenate([cost_ref[...]] * reps, axis=0)
    sint = jnp.concatenate([sint_ref[...]] * reps, axis=0)
    qpet = ((qpt * cost + qprt * sint) * scale).astype(BF16)
    for hd in range(MLA_HEADS):
        qlt = lax.dot_general(wuk_ref[hd], qn[:, hd * QK_NOPE:(hd + 1) * QK_NOPE], NT_DIMS,
                              preferred_element_type=F32)
        qlt = (qlt * scale).astype(BF16)
        for s in range(nseg):
            src = slice(s * seg_len, (s + 1) * seg_len)
            dst = slice(hd * seg_len, (hd + 1) * seg_len)
            qt_ref[s, :KV_RANK, dst] = qlt[:, src]
            qt_ref[s, KV_RANK:, dst] = qpet[hd * QK_ROPE:(hd + 1) * QK_ROPE, src]


def _layer0_tail_kernel(*refs):
    yg_ref, x_ref, buf_ref, cos_ref, sin_ref, cost_ref, sint_ref, wout_ref, gmix_ref = refs[:9]
    ffn_w = refs[9:15]
    kv_w = refs[15:18]
    q_w = refs[18:25]
    x_out_ref, nbuf_ref, lat_ref, kpe_ref, kcat_ref, qt_ref, latt_ref, ext_ref, act_ref = refs[25:]
    x = _mixer_residual(yg_ref[...], x_ref[...], wout_ref, gmix_ref)
    x = _ffn_body(x, buf_ref, *ffn_w, nbuf_ref, ext_ref, act_ref)
    x_out_ref[...] = x
    _kv_body(x, *kv_w, cos_ref, sin_ref, lat_ref, kpe_ref, kcat_ref, latt_ref)
    _q_body(x, *q_w, cost_ref, sint_ref, qt_ref)


def _layer0_tail(yg, x, ffn_buf, cos, sin, w_out, g_mix, ffn_w, kv_w, q_w, tm):
    b, t, _ = x.shape
    grid = (b, t // tm)
    row = lambda wd: pl.BlockSpec((None, tm, wd), lambda i, j: (i, j, 0))
    nseg = ffn_buf.shape[1]
    buf_spec = pl.BlockSpec((None, nseg, FFN_CONV - 1, 2 * D_FF), lambda i, j: (i, 0, 0, 0))
    tab = pl.BlockSpec((tm, LANE), lambda i, j: (j, 0))
    tab_t = pl.BlockSpec((LANE, tm), lambda i, j: (0, j))
    weight_specs = ([_const_spec(s) for s in ((D_INNER, D_MODEL), (1, D_MODEL))]
                    + [_layer_spec(s, 0) for s in FFN_WEIGHT_SPECS]
                    + [_const_spec(s) for s in KV_WEIGHT_SPECS + Q_WEIGHT_SPECS])
    return pl.pallas_call(
        _layer0_tail_kernel,
        grid=grid,
        in_specs=[row(D_INNER), row(D_MODEL), buf_spec, tab, tab, tab_t, tab_t] + weight_specs,
        out_specs=[
            row(D_MODEL), buf_spec, row(KV_RANK), row(QK_ROPE), row(QK_CAT),
            pl.BlockSpec((None, None, nseg, QK_CAT, MLA_HEADS * tm // nseg), lambda i, j: (i, j, 0, 0, 0)),
            pl.BlockSpec((None, None, KV_RANK, tm), lambda i, j: (i, j, 0, 0)),
        ],
        out_shape=[
            jax.ShapeDtypeStruct((b, t, D_MODEL), F32),
            jax.ShapeDtypeStruct((b, nseg, FFN_CONV - 1, 2 * D_FF), F32),
            jax.ShapeDtypeStruct((b, t, KV_RANK), F32),
            jax.ShapeDtypeStruct((b, t, QK_ROPE), F32),
            jax.ShapeDtypeStruct((b, t, QK_CAT), BF16),
            jax.ShapeDtypeStruct((b, t // tm, nseg, QK_CAT, MLA_HEADS * tm // nseg), BF16),
            jax.ShapeDtypeStruct((b, t // tm, KV_RANK, tm), BF16),
        ],
        scratch_shapes=_ffn_scratch(tm, nseg),
        compiler_params=_params("parallel", "arbitrary"),
        name="layer0_tail",
    )(yg, x, ffn_buf, cos, sin, cos.T, sin.T, w_out, g_mix, *ffn_w, *kv_w, *q_w)


def _layer1_tail_kernel(*refs):
    o_ref, x_ref, buf_ref, wuv_ref, wo_ref, gmix_ref = refs[:6]
    ffn_w = refs[6:12]
    x_out_ref, nbuf_ref, ext_ref, act_ref = refs[12:]
    x = _mixer_residual(_mla_values(o_ref, wuv_ref), x_ref[...], wo_ref, gmix_ref)
    x_out_ref[...] = _ffn_body(x, buf_ref, *ffn_w, nbuf_ref, ext_ref, act_ref)


def _layer1_tail(o_lat, x, ffn_buf, wuv, wo, g_mix, ffn_w, tm):
    b, t, _ = x.shape
    grid = (b, t // tm)
    row = lambda wd: pl.BlockSpec((None, tm, wd), lambda i, j: (i, j, 0))
    nseg = ffn_buf.shape[1]
    buf_spec = pl.BlockSpec((None, nseg, FFN_CONV - 1, 2 * D_FF), lambda i, j: (i, 0, 0, 0))
    weight_specs = ([_const_spec(s) for s in ((MLA_HEADS, KV_RANK, V_DIM), (MLA_HEADS * V_DIM, D_MODEL), (1, D_MODEL))]
                    + [_layer_spec(s, 1) for s in FFN_WEIGHT_SPECS])
    return pl.pallas_call(
        _layer1_tail_kernel,
        grid=grid,
        in_specs=[pl.BlockSpec((None, nseg, MLA_HEADS, tm // nseg, KV_RANK), lambda i, j: (i, 0, 0, j, 0)),
                  row(D_MODEL), buf_spec] + weight_specs,
        out_specs=[row(D_MODEL), buf_spec],
        out_shape=[
            jax.ShapeDtypeStruct((b, t, D_MODEL), F32),
            jax.ShapeDtypeStruct((b, nseg, FFN_CONV - 1, 2 * D_FF), F32),
        ],
        scratch_shapes=_ffn_scratch(tm, nseg),
        compiler_params=_params("parallel", "arbitrary"),
        name="layer1_tail",
    )(o_lat, x, ffn_buf, wuv, wo, g_mix, *ffn_w)


def _attn_kernel(qt_ref, k_ref, *rest, tq, tk, past, nkv, values_given):
    vt_all = rest[0] if values_given else None
    o_all, m_all, l_all, a_all, acc_all, s_all, p_all_ref = rest[1:] if values_given else rest
    nb = qt_ref.shape[0]
    qi = pl.program_id(1)
    rows = MLA_HEADS * tq
    nblk = rows // LANE
    blocks = [slice(b * LANE, (b + 1) * LANE) for b in range(nblk)]
    m_all[...] = jnp.full(m_all.shape, -jnp.inf, F32)
    l_all[...] = jnp.zeros(l_all.shape, F32)
    acc_all[...] = jnp.zeros(acc_all.shape, F32)
    n_vis = jnp.minimum(nkv, (past + (qi + 1) * tq + tk - 1) // tk)

    def scores(bi, j):
        ks = k_ref[bi, pl.ds(pl.multiple_of(j * tk, tk), tk), :]
        return jnp.dot(ks, qt_ref[bi], preferred_element_type=F32)

    def store_scores(bi, slot, s):
        for b, csl in enumerate(blocks):
            s_all[bi, slot, b] = s[:, csl]

    def hidden_bias(j):
        key_chunk = lax.broadcasted_iota(jnp.int32, (tk, LANE), 0) // CHUNK + j * (tk // CHUNK)
        lane = lax.broadcasted_iota(jnp.int32, (tk, LANE), 1)
        pats = []
        for ph in range(max(1, tq // LANE)):
            q_chunk = (past + qi * tq) // CHUNK + ((ph * LANE + lane) % tq) // CHUNK
            pats.append(jnp.where(key_chunk <= q_chunk, 0.0, -jnp.inf))
        return pats

    def absorb(bi, j, masked):
        m_ref, l_ref, a_ref, acc_ref, p_ref = m_all.at[bi], l_all.at[bi], a_all.at[bi], acc_all.at[bi], p_all_ref.at[bi]
        slot = j % 2
        pats = hidden_bias(j) if masked else None
        for b, csl in enumerate(blocks):
            s = s_all[bi, slot, b]
            if masked:
                s = s + pats[b % len(pats)]
            m_prev = m_ref[:, csl]
            m_new = jnp.maximum(m_prev, jnp.max(s, axis=0, keepdims=True))
            alpha = jnp.exp2(m_prev - m_new)
            p = jnp.exp2(s - m_new)
            l_ref[:, csl] = alpha * l_ref[:, csl] + jnp.sum(p, axis=0, keepdims=True)
            m_ref[:, csl] = m_new
            a_ref[:, csl] = alpha
            p_ref[b] = p.astype(BF16)
        p_all = jnp.concatenate([p_ref[b] for b in range(nblk)], axis=1)
        if values_given:
            vt = vt_all[bi, j]
        else:
            lat = k_ref[bi, pl.ds(pl.multiple_of(j * tk, tk), tk), :KV_RANK]
            vt = lat.astype(F32).T.astype(BF16)
        pv = jnp.dot(vt, p_all, preferred_element_type=F32)
        for b, csl in enumerate(blocks):
            acc_ref[b] = a_ref[:, csl] * acc_ref[b] + pv[:, csl]

    for bi in range(nb):
        store_scores(bi, 0, scores(bi, 0))

    def body(j, carry):
        s_new = [scores(bi, j) for bi in range(nb)]
        for bi in range(nb):
            absorb(bi, j - 1, masked=False)
        for bi in range(nb):
            store_scores(bi, j % 2, s_new[bi])
        return carry

    lax.fori_loop(1, n_vis, body, 0)
    for bi in range(nb):
        absorb(bi, n_vis - 1, masked=True)
    for bi in range(nb):
        o = jnp.concatenate([acc_all[bi, b] * (1.0 / l_all[bi, :, csl]) for b, csl in enumerate(blocks)], axis=1).T
        o_all[bi] = o.astype(BF16).reshape(MLA_HEADS, tq, KV_RANK)


def _attention(qt, kcat, vt, past, tq, tk):
    b, ng, _, rows = qt.shape
    t = ng * tq
    s_len = kcat.shape[1]
    nkv = s_len // tk
    assert tk % tq == 0 and past % tk == 0 and rows == MLA_HEADS * tq and tq % CHUNK == 0
    nb = ATT_SIDE_BY_SIDE if tq <= ATT_PAIR_MAX_TQ and b % ATT_SIDE_BY_SIDE == 0 else 1
    grid = (b // nb, ng)
    nblk = rows // LANE
    operands = [qt, kcat] + ([] if vt is None else [vt])
    value_specs = [] if vt is None else [pl.BlockSpec((nb, nkv, KV_RANK, tk), lambda i, j: (i, 0, 0, 0))]
    return pl.pallas_call(
        functools.partial(_attn_kernel, tq=tq, tk=tk, past=past, nkv=nkv, values_given=vt is not None),
        grid=grid,
        in_specs=[
            pl.BlockSpec((nb, None, QK_CAT, rows), lambda i, j: (i, j, 0, 0)),
            pl.BlockSpec((nb, s_len, QK_CAT), lambda i, j: (i, 0, 0)),
        ] + value_specs,
        out_specs=pl.BlockSpec((nb, MLA_HEADS, tq, KV_RANK), lambda i, j: (i, 0, j, 0)),
        out_shape=jax.ShapeDtypeStruct((b, MLA_HEADS, t, KV_RANK), BF16),
        scratch_shapes=[
            pltpu.VMEM((nb, 1, rows), F32),
            pltpu.VMEM((nb, 1, rows), F32),
            pltpu.VMEM((nb, 1, rows), F32),
            pltpu.VMEM((nb, nblk, KV_RANK, LANE), F32),
            pltpu.VMEM((nb, 2, nblk, tk, LANE), F32),
            pltpu.VMEM((nb, nblk, tk, LANE), BF16),
        ],
        compiler_params=_params("parallel", "parallel"),
        name="mla_attention",
    )(*operands)


def _rope_tables(past, t):
    half = QK_ROPE // 2
    inv = jnp.exp(-math.log(ROPE_THETA) * jnp.arange(half, dtype=F32) / half)
    ang = (past + jnp.arange(t, dtype=jnp.int32)).astype(F32)[:, None] * inv[None, :]
    reps = 2 * LANE // QK_ROPE
    return jnp.tile(jnp.cos(ang), (1, reps)), jnp.tile(jnp.sin(ang), (1, reps))


def _rotate_half_cols(w):
    shp = w.shape
    w4 = w.reshape(shp[:-1] + (shp[-1] // QK_ROPE, 2, QK_ROPE // 2))
    return jnp.concatenate([-w4[..., 1:2, :], w4[..., 0:1, :]], axis=-2).reshape(shp)


def _tile(t, pref):
    return pref if t % pref == 0 else t


def _prep_weights(norm_mix_pre, norm_mix_post, norm_ffn_pre, norm_ffn_post,
                  ssm_w_in, ssm_conv_w, ssm_conv_b, ssm_dt_bias, ssm_a_log, ssm_d, ssm_norm, ssm_w_out,
                  kv_norm_in, kv_w_dkv, kv_norm, kv_w_kr, kv_w_uk, kv_w_uv,
                  mla_w_dq, mla_q_norm, mla_w_uq, mla_w_o,
                  ffn_w_up, ffn_conv_w, ffn_conv_b, ffn_w_down):
    w = {}
    w_in = ssm_w_in[0]
    w["wzx"] = w_in.astype(BF16)
    w["wdt"] = jnp.pad(w_in[:, D_INNER + CONV_DIM:], ((0, 0), (0, LANE - SSM_HEADS))).astype(BF16)
    w_dt_t = w_in[:, D_INNER + CONV_DIM:].T
    w["wdtt"] = jnp.concatenate([w_dt_t[0::2], w_dt_t[1::2]]).astype(BF16)
    w["ssm_cw"] = ssm_conv_w[0]
    w["ssm_cb"] = ssm_conv_b[0][None, :]
    w["dtb"] = ssm_dt_bias[0][None, :]
    w["dtbt"] = jnp.concatenate([ssm_dt_bias[0][0::2], ssm_dt_bias[0][1::2]])[:, None]
    a = -jnp.exp(ssm_a_log[0].astype(F32))
    w["arow"] = a[None, :]
    w["acol2"] = jnp.broadcast_to(a.reshape(SSM_HEADS // 2, 2, 1),
                                  (SSM_HEADS // 2, 2, CHUNK)).reshape(SSM_HEADS // 2, 2 * CHUNK)
    w["expand"] = jnp.repeat(jnp.eye(SSM_HEADS, dtype=BF16), SSM_HEADDIM, axis=1)
    w["dskip"] = jnp.repeat(ssm_d[0], SSM_HEADDIM)[None, :]
    w["gn"] = ssm_norm[0][None, :]
    w["w_out"] = ssm_w_out[0].astype(BF16)
    w["g_mix_pre"] = norm_mix_pre[:, None, :]
    w["g_mix_post"] = norm_mix_post[:, None, :]
    w["g_ffn_pre"] = norm_ffn_pre[:, None, :]
    w["g_ffn_post"] = norm_ffn_post[:, None, :]
    w["ffn_up"] = ffn_w_up.astype(BF16)
    w["ffn_cw"] = ffn_conv_w
    w["ffn_cb"] = ffn_conv_b[:, None, :]
    w["ffn_down"] = ffn_w_down.astype(BF16)
    w["kv_gin"] = kv_norm_in[None, :]
    w["kv_w"] = jnp.concatenate([kv_w_dkv, kv_w_kr, _rotate_half_cols(kv_w_kr)], axis=-1).astype(BF16)
    w["kv_g"] = kv_norm[None, :]
    w["wdq"] = mla_w_dq[0].astype(BF16)
    w["gq"] = mla_q_norm[0][None, :]
    wuq = mla_w_uq[0].reshape(Q_RANK, MLA_HEADS, QK_NOPE + QK_ROPE)
    w["wq_nope"] = wuq[:, :, :QK_NOPE].reshape(Q_RANK, MLA_HEADS * QK_NOPE).astype(BF16)
    wq_pe = wuq[:, :, QK_NOPE:].reshape(Q_RANK, MLA_HEADS * QK_ROPE)
    w["wq_pe_t"] = wq_pe.T.astype(BF16)
    w["wq_pe_rot_t"] = _rotate_half_cols(wq_pe).T.astype(BF16)
    w["wuk"] = jnp.transpose(kv_w_uk, (1, 0, 2)).astype(BF16)
    w["wuv"] = jnp.transpose(kv_w_uv, (1, 0, 2)).astype(BF16)
    w["wo"] = mla_w_o[0].astype(BF16)
    return w


def _trunk(x, ssm_conv_buf, ssm_state, ffn_buf, past_kcat, w):
    b, t, _ = x.shape
    past = 0 if past_kcat is None else past_kcat[0].shape[1]
    assert t % CHUNK == 0 and past % CHUNK == 0
    tm = TOKEN_TILE
    nseq = tm // t if t < tm and tm % t == 0 and b % (tm // t) == 0 else 1
    if t % tm != 0 and nseq == 1:
        tm = t
    bb, tt = b // nseq, nseq * t
    pack = lambda a: a.reshape((bb, tt) + a.shape[2:])
    unpack = lambda a: a.reshape((b, t) + a.shape[2:])
    pack_hist = lambda a: a.reshape((bb, nseq) + a.shape[1:])
    unpack_hist = lambda a: a.reshape((b,) + a.shape[2:])

    z, xbc, dt, dtt2, new_conv = _mamba_in(pack(x), pack_hist(ssm_conv_buf), w["g_mix_pre"][0], w["wzx"], w["wdt"],
                                           w["wdtt"], w["ssm_cw"], w["ssm_cb"], w["dtb"], w["dtbt"], tm)
    nc = _tile(t, SSD_TILE) // CHUNK
    dtt2 = dtt2.reshape(b, t // CHUNK, SSM_HEADS // 2, 2 * CHUNK)
    yg, new_state = _ssd(unpack(xbc), unpack(z), unpack(dt), dtt2, w["arow"], w["acol2"], w["expand"], w["dskip"],
                         w["gn"], None if ssm_state is None else ssm_state.reshape(b, D_INNER, D_STATE), nc)
    new_state = new_state.reshape(b, SSM_HEADS, SSM_HEADDIM, D_STATE)

    ffn_w = lambda i: (w["g_ffn_pre"], w["ffn_up"], w["ffn_cw"], w["ffn_cb"], w["ffn_down"], w["g_ffn_post"])
    cos, sin = _rope_tables(past, t)
    x, new_ffn0, lat, kpe, kcat, q, latt = _layer0_tail(
        pack(yg), pack(x), pack_hist(ffn_buf[0]), jnp.tile(cos, (nseq, 1)), jnp.tile(sin, (nseq, 1)),
        w["w_out"], w["g_mix_post"][0], ffn_w(0), (w["kv_gin"], w["kv_w"], w["kv_g"]),
        (w["g_mix_pre"][1], w["wdq"], w["gq"], w["wq_nope"], w["wq_pe_t"], w["wq_pe_rot_t"], w["wuk"]), tm)
    lat, kpe, kcat = unpack(lat), unpack(kpe), unpack(kcat)
    tq = _tile(t, ATT_TK)
    assert tq == (t if nseq > 1 else tm)
    q = q.reshape(b, t // tq, QK_CAT, MLA_HEADS * tq)

    s_len = past + t
    s_pad = -(-s_len // ATT_TK) * ATT_TK
    if past_kcat is None and tm == ATT_TK and s_pad == s_len:
        vt = latt
    else:
        vt = None
        pieces = [kcat]
        if past_kcat is not None:
            pieces.insert(0, jnp.concatenate([a.astype(BF16) for a in past_kcat], axis=-1))
        if s_pad != s_len:
            pieces.append(jnp.zeros((b, s_pad - s_len, QK_CAT), BF16))
        kcat = jnp.concatenate(pieces, axis=1) if len(pieces) > 1 else kcat
    o_lat = _attention(q, kcat, vt, past, tq, ATT_TK)
    o_lat = o_lat.reshape(bb, nseq, MLA_HEADS, t, KV_RANK)
    x, new_ffn1 = _layer1_tail(o_lat, x, pack_hist(ffn_buf[1]), w["wuv"], w["wo"], w["g_mix_post"][1], ffn_w(1), tm)
    return (unpack(x), new_state[None], unpack_hist(new_conv)[None],
            jnp.stack([unpack_hist(new_ffn0), unpack_hist(new_ffn1)]), lat, kpe)


def kernel(x_prompt, x_sample, state_ssm, state_ssm_conv, state_ffn_conv, cache_kv_latent, cache_k_rope, norm_mix_pre, norm_mix_post, norm_ffn_pre, norm_ffn_post, ssm_w_in, ssm_conv_w, ssm_conv_b, ssm_dt_bias, ssm_a_log, ssm_d, ssm_norm, ssm_w_out, kv_norm_in, kv_w_dkv, kv_norm, kv_w_kr, kv_w_uk, kv_w_uv, mla_w_dq, mla_q_norm, mla_w_uq, mla_w_o, ffn_w_up, ffn_conv_w, ffn_conv_b, ffn_w_down):
    w = _prep_weights(norm_mix_pre, norm_mix_post, norm_ffn_pre, norm_ffn_post,
                      ssm_w_in, ssm_conv_w, ssm_conv_b, ssm_dt_bias, ssm_a_log, ssm_d, ssm_norm, ssm_w_out,
                      kv_norm_in, kv_w_dkv, kv_norm, kv_w_kr, kv_w_uk, kv_w_uv,
                      mla_w_dq, mla_q_norm, mla_w_uq, mla_w_o,
                      ffn_w_up, ffn_conv_w, ffn_conv_b, ffn_w_down)
    bp = x_prompt.shape[0]
    dtp = x_prompt.dtype
    y_p, p_ssm, p_conv, p_ffn, p_lat, p_kpe = _trunk(
        x_prompt,
        jnp.zeros((bp, SSM_CONV - 1, CONV_DIM), dtp),
        None,
        jnp.zeros((2, bp, FFN_CONV - 1, 2 * D_FF), dtp),
        None, w)
    y_s, s_ssm, s_conv, s_ffn, s_lat, s_kpe = _trunk(
        x_sample, state_ssm_conv[0], state_ssm[0], state_ffn_conv, (cache_kv_latent, cache_k_rope), w)
    return (y_p, y_s, p_ssm, p_conv, p_ffn, p_lat, p_kpe, s_ssm, s_conv, s_ffn, s_lat, s_kpe)
```

```python
import functools
import math

import jax
import jax.numpy as jnp
from jax import lax
from jax.experimental import pallas as pl
from jax.experimental.pallas import tpu as pltpu

F32 = jnp.float32
BF16 = jnp.bfloat16

D_MODEL = 1024
CHUNK = 64
EPS = 1e-6
D_INNER = 2048
SSM_HEADDIM = 64
SSM_HEADS = 32
SSM_GROUPS = 4
SSM_HPG = 8
D_STATE = 128
SSM_CONV = 4
GROUP_W = D_INNER // SSM_GROUPS
BC_W = SSM_GROUPS * D_STATE
CONV_DIM = D_INNER + 2 * BC_W
MLA_HEADS = 16
Q_RANK = 384
KV_RANK = 256
QK_NOPE = 128
QK_ROPE = 64
V_DIM = 128
QK_CAT = KV_RANK + QK_ROPE
ROPE_THETA = 10000.0
D_FF = 2816
FFN_CONV = 3
LANE = 128
CARRY_ROWS = 8
TOKEN_TILE = 256
COL_CHUNK = 512
FFN_CHUNK = 256
SSD_TILE = 512
SSD_UNROLL = 8
ATT_TK = 256
ATT_PAIR_MAX_TQ = 64
ATT_SIDE_BY_SIDE = 4
LOG2E = 1.4426950408889634
VMEM_LIMIT = 56 * 1024 * 1024

NT_DIMS = (((1,), (1,)), ((), ()))
TN_DIMS = (((0,), (0,)), ((), ()))


def _rms(x, g):
    return x * lax.rsqrt(jnp.mean(x * x, axis=-1, keepdims=True) + EPS) * g


def _silu(x):
    return x * (1.0 / (1.0 + jnp.exp(-x)))


def _softplus(x):
    return jnp.maximum(x, 0.0) + jnp.log(1.0 + jnp.exp(-jnp.abs(x)))


def _const_spec(shape):
    zeros = (0,) * len(shape)
    return pl.BlockSpec(shape, lambda *_: zeros, pipeline_mode=pl.Buffered(1))


def _layer_spec(shape, layer):
    zeros = (0,) * len(shape)
    return pl.BlockSpec((None,) + tuple(shape), lambda *_: (layer,) + zeros, pipeline_mode=pl.Buffered(1))


def _params(*sem):
    return pltpu.CompilerParams(dimension_semantics=sem, vmem_limit_bytes=VMEM_LIMIT)


def _init_carry(carry_ref, hist_ref, first_step):
    nseg, hist, _ = hist_ref.shape

    @pl.when(first_step)
    def _():
        carry_ref[...] = jnp.zeros(carry_ref.shape, F32)
        for s in range(nseg):
            carry_ref[(s + 1) * CARRY_ROWS - hist:(s + 1) * CARRY_ROWS, :] = hist_ref[s]


def _store_history(nbuf_ref, carry_ref):
    nseg, hist, _ = nbuf_ref.shape
    for s in range(nseg):
        nbuf_ref[s] = carry_ref[(s + 1) * CARRY_ROWS - hist:(s + 1) * CARRY_ROWS, :]


def _causal_conv(raw, carry_ref, cols, w_ref, b_ref, taps):
    nseg = carry_ref.shape[0] // CARRY_ROWS
    seg_len = raw.shape[0] // nseg
    hist = taps - 1
    outs = []
    for s in range(nseg):
        cur = raw[s * seg_len:(s + 1) * seg_len, :]
        prev = carry_ref[s * CARRY_ROWS:(s + 1) * CARRY_ROWS, cols]
        ext = jnp.concatenate([prev, cur], axis=0)
        acc = b_ref[:, cols] + cur * w_ref[hist:hist + 1, cols]
        for k in range(1, taps):
            acc = acc + ext[CARRY_ROWS - k:CARRY_ROWS - k + seg_len, :] * w_ref[hist - k:hist - k + 1, cols]
        carry_ref[s * CARRY_ROWS:(s + 1) * CARRY_ROWS, cols] = cur[seg_len - CARRY_ROWS:, :]
        outs.append(acc)
    return outs[0] if nseg == 1 else jnp.concatenate(outs, axis=0)


def _mamba_in_kernel(x_ref, buf_ref, g_ref, wzx_ref, wdt_ref, wdtt_ref, cw_ref, cb_ref, dtb_ref, dtbt_ref,
                     z_ref, xbc_ref, dt_ref, dtt_ref, nbuf_ref, ext_ref):
    _init_carry(ext_ref, buf_ref, pl.program_id(1) == 0)
    h = _rms(x_ref[...], g_ref[...]).astype(BF16)
    for j in range(D_INNER // COL_CHUNK):
        cols = slice(j * COL_CHUNK, (j + 1) * COL_CHUNK)
        z_ref[:, cols] = jnp.dot(h, wzx_ref[:, cols], preferred_element_type=F32).astype(BF16)
    nchunk = CONV_DIM // COL_CHUNK
    proj = lambda j: jnp.dot(h, wzx_ref[:, D_INNER + j * COL_CHUNK:D_INNER + (j + 1) * COL_CHUNK],
                             preferred_element_type=F32)
    raw = proj(0)
    for j in range(nchunk):
        nxt = proj(j + 1) if j + 1 < nchunk else None
        cols = slice(j * COL_CHUNK, (j + 1) * COL_CHUNK)
        xbc_ref[:, cols] = _silu(_causal_conv(raw, ext_ref, cols, cw_ref, cb_ref, SSM_CONV)).astype(BF16)
        raw = nxt
    _store_history(nbuf_ref, ext_ref)
    dt = jnp.dot(h, wdt_ref[...], preferred_element_type=F32)[:, :SSM_HEADS]
    dt_ref[...] = _softplus(dt + dtb_ref[...])
    dtt = _softplus(lax.dot_general(wdtt_ref[...], h, NT_DIMS, preferred_element_type=F32) + dtbt_ref[...])
    for c in range(h.shape[0] // CHUNK):
        for i in range(2):
            dtt_ref[c, :, i * CHUNK:(i + 1) * CHUNK] = dtt[i * (SSM_HEADS // 2):(i + 1) * (SSM_HEADS // 2),
                                                          c * CHUNK:(c + 1) * CHUNK]


def _mamba_in(x, conv_buf, g, wzx, wdt, wdtt, cw, cb, dtb, dtbt, tm):
    b, t, _ = x.shape
    nseg = conv_buf.shape[1]
    grid = (b, t // tm)
    row = lambda w: pl.BlockSpec((None, tm, w), lambda i, j: (i, j, 0))
    return pl.pallas_call(
        _mamba_in_kernel,
        grid=grid,
        in_specs=[
            row(D_MODEL),
            pl.BlockSpec((None, nseg, SSM_CONV - 1, CONV_DIM), lambda i, j: (i, 0, 0, 0)),
            _const_spec((1, D_MODEL)),
            _const_spec((D_MODEL, D_INNER + CONV_DIM + SSM_HEADS)),
            _const_spec((D_MODEL, LANE)),
            _const_spec((SSM_HEADS, D_MODEL)),
            _const_spec((SSM_CONV, CONV_DIM)),
            _const_spec((1, CONV_DIM)),
            _const_spec((1, SSM_HEADS)),
            _const_spec((SSM_HEADS, 1)),
        ],
        out_specs=[
            row(D_INNER),
            row(CONV_DIM),
            row(SSM_HEADS),
            pl.BlockSpec((None, tm // CHUNK, SSM_HEADS // 2, 2 * CHUNK), lambda i, j: (i, j, 0, 0)),
            pl.BlockSpec((None, nseg, SSM_CONV - 1, CONV_DIM), lambda i, j: (i, 0, 0, 0)),
        ],
        out_shape=[
            jax.ShapeDtypeStruct((b, t, D_INNER), BF16),
            jax.ShapeDtypeStruct((b, t, CONV_DIM), BF16),
            jax.ShapeDtypeStruct((b, t, SSM_HEADS), F32),
            jax.ShapeDtypeStruct((b, t // CHUNK, SSM_HEADS // 2, 2 * CHUNK), F32),
            jax.ShapeDtypeStruct((b, nseg, SSM_CONV - 1, CONV_DIM), F32),
        ],
        scratch_shapes=[pltpu.VMEM((nseg * CARRY_ROWS, CONV_DIM), F32)],
        compiler_params=_params("parallel", "arbitrary"),
        name="mamba_in",
    )(x, conv_buf, g, wzx, wdt, wdtt, cw, cb, dtb, dtbt)


def _ssd_kernel(xs_ref, bm_ref, cm_ref, z_ref, dt_ref, dtt_ref, arow_ref, acol_ref, exp_ref, dskip_ref, gn_ref,
                *rest, nc, has_state):
    h0_ref = rest[0] if has_state else None
    yg_ref, hout_ref, st_ref = rest[1:] if has_state else rest
    step = pl.program_id(1)

    @pl.when(step == 0)
    def _():
        st_ref[...] = h0_ref[...].T if has_state else jnp.zeros(st_ref.shape, F32)

    li = lax.broadcasted_iota(jnp.int32, (CHUNK, CHUNK), 0)
    si = lax.broadcasted_iota(jnp.int32, (CHUNK, CHUNK), 1)
    tril = (si <= li).astype(F32)
    pi = lax.broadcasted_iota(jnp.int32, (LANE, LANE), 0)
    pj = lax.broadcasted_iota(jnp.int32, (LANE, LANE), 1)
    same_head = (pi // CHUNK) == (pj // CHUNK)
    triu2 = jnp.where(same_head, (pi <= pj).astype(F32), 0.0)
    l2 = lax.broadcasted_iota(jnp.int32, (CHUNK, LANE), 0)
    s2 = lax.broadcasted_iota(jnp.int32, (CHUNK, LANE), 1) % CHUNK
    causal2 = s2 <= l2
    expand = exp_ref[...]

    gsls = [slice(g * GROUP_W, (g + 1) * GROUP_W) for g in range(SSM_GROUPS)]

    def decays(c):
        rows = pl.ds(pl.multiple_of(c * CHUNK, CHUNK), CHUNK)
        dt = dt_ref[rows, :]
        dtt2 = dtt_ref[c]
        acs = jnp.dot(tril, dt * arow_ref[...], precision=lax.Precision.HIGHEST,
                      preferred_element_type=F32)
        acst2 = jnp.dot(dtt2 * acol_ref[...], triu2, precision=lax.Precision.HIGHEST,
                        preferred_element_type=F32)
        last = acs[CHUNK - 1:CHUNK, :]
        wst = (jnp.exp(last - acs) * dt).astype(BF16)
        hi = acs.astype(BF16)
        r1 = acs - hi.astype(F32)
        mid = r1.astype(BF16)
        lo = (r1 - mid.astype(F32)).astype(BF16)
        acs_w = (jnp.dot(hi, expand, preferred_element_type=F32) + jnp.dot(mid, expand, preferred_element_type=F32)
                 + jnp.dot(lo, expand, preferred_element_type=F32))
        e_acs_w = jnp.exp(acs_w)
        wst_w = jnp.dot(wst, expand, preferred_element_type=F32)
        decay_w = jnp.exp(acs_w[CHUNK - 1:CHUNK, :])
        return dtt2, acst2, acs_w, e_acs_w, wst_w, decay_w

    def through_state(c, wst_w, decay_w):
        rows = pl.ds(pl.multiple_of(c * CHUNK, CHUNK), CHUNK)
        cbs, yoffs = [], []
        for g, gsl in enumerate(gsls):
            bg = bm_ref[rows, g * D_STATE:(g + 1) * D_STATE]
            cg = cm_ref[rows, g * D_STATE:(g + 1) * D_STATE]
            cbs.append(lax.dot_general(cg, bg, NT_DIMS, preferred_element_type=F32))
            stg = st_ref[:, gsl]
            yoffs.append(jnp.dot(cg, stg.astype(BF16), preferred_element_type=F32))
            xw = (xs_ref[rows, gsl].astype(F32) * wst_w[:, gsl]).astype(BF16)
            st_ref[:, gsl] = stg * decay_w[:, gsl] + lax.dot_general(bg, xw, TN_DIMS, preferred_element_type=F32)
        return cbs, yoffs

    def within_chunk(c, dtt2, acst2, acs_w, e_acs_w, cbs, yoffs):
        rows = pl.ds(pl.multiple_of(c * CHUNK, CHUNK), CHUNK)
        for g, gsl in enumerate(gsls):
            xg = xs_ref[rows, gsl]
            cb2 = jnp.concatenate([cbs[g], cbs[g]], axis=1)
            ys = []
            for jj in range(SSM_HPG // 2):
                j = g * (SSM_HPG // 2) + jj
                seg = acs_w[:, j * LANE:(j + 1) * LANE] - acst2[j:j + 1, :]
                wts = cb2 * jnp.exp(jnp.where(causal2, seg, -jnp.inf)) * dtt2[j:j + 1, :]
                xp = xg[:, jj * LANE:(jj + 1) * LANE]
                xbd = jnp.where(same_head, jnp.concatenate([xp, xp], axis=0), jnp.zeros((), BF16))
                ys.append(jnp.dot(wts.astype(BF16), xbd, preferred_element_type=F32))
            y = jnp.concatenate(ys, axis=1)
            y = y + yoffs[g] * e_acs_w[:, gsl] + xg.astype(F32) * dskip_ref[:, gsl]
            yg = y * _silu(z_ref[rows, gsl].astype(F32))
            yg_ref[rows, gsl] = _rms(yg, gn_ref[:, gsl]).astype(BF16)

    unroll = min(nc, SSD_UNROLL)

    def body(i, carry):
        cs = [i * unroll + k for k in range(unroll)]
        pre = [decays(c) for c in cs]
        mid = [through_state(c, p[4], p[5]) for c, p in zip(cs, pre)]
        for c, p, m in zip(cs, pre, mid):
            within_chunk(c, p[0], p[1], p[2], p[3], m[0], m[1])
        return carry

    lax.fori_loop(0, nc // unroll, body, 0)

    @pl.when(step == pl.num_programs(1) - 1)
    def _():
        hout_ref[...] = st_ref[...].T


def _ssd(xbc, z, dt, dtt2, arow, acol2, expand, dskip, gn, h0t, nc):
    b, t, _ = xbc.shape
    lt = nc * CHUNK
    grid = (b, t // lt)
    state_spec = pl.BlockSpec((None, D_INNER, D_STATE), lambda i, j: (i, 0, 0))
    state_in = [] if h0t is None else [h0t]
    return pl.pallas_call(
        functools.partial(_ssd_kernel, nc=nc, has_state=h0t is not None),
        grid=grid,
        in_specs=[
            pl.BlockSpec((None, lt, D_INNER), lambda i, j: (i, j, 0)),
            pl.BlockSpec((None, lt, BC_W), lambda i, j: (i, j, D_INNER // BC_W)),
            pl.BlockSpec((None, lt, BC_W), lambda i, j: (i, j, D_INNER // BC_W + 1)),
            pl.BlockSpec((None, lt, D_INNER), lambda i, j: (i, j, 0)),
            pl.BlockSpec((None, lt, SSM_HEADS), lambda i, j: (i, j, 0)),
            pl.BlockSpec((None, nc, SSM_HEADS // 2, 2 * CHUNK), lambda i, j: (i, j, 0, 0)),
            _const_spec((1, SSM_HEADS)),
            _const_spec((SSM_HEADS // 2, 2 * CHUNK)),
            _const_spec((SSM_HEADS, D_INNER)),
            _const_spec((1, D_INNER)),
            _const_spec((1, D_INNER)),
        ] + [state_spec] * len(state_in),
        out_specs=[
            pl.BlockSpec((None, lt, D_INNER), lambda i, j: (i, j, 0)),
            state_spec,
        ],
        out_shape=[
            jax.ShapeDtypeStruct((b, t, D_INNER), BF16),
            jax.ShapeDtypeStruct((b, D_INNER, D_STATE), F32),
        ],
        scratch_shapes=[pltpu.VMEM((D_STATE, D_INNER), F32)],
        compiler_params=_params("parallel", "arbitrary"),
        name="ssd_scan",
    )(xbc, xbc, xbc, z, dt, dtt2, arow, acol2, expand, dskip, gn, *state_in)


def _mixer_residual(mix_bf16, x, w_ref, g_ref):
    return x + _rms(jnp.dot(mix_bf16, w_ref[...], preferred_element_type=F32), g_ref[...])


def _mla_values(o_ref, wuv_ref):
    nseg = o_ref.shape[0]
    parts = []
    for h in range(MLA_HEADS):
        o_h = o_ref[0, h] if nseg == 1 else jnp.concatenate([o_ref[s, h] for s in range(nseg)], axis=0)
        parts.append(jnp.dot(o_h, wuv_ref[h], preferred_element_type=F32).astype(BF16))
    return jnp.concatenate(parts, axis=-1)


FFN_WEIGHT_SPECS = (
    (1, D_MODEL), (D_MODEL, 2 * D_FF), (FFN_CONV, 2 * D_FF), (1, 2 * D_FF), (D_FF, D_MODEL), (1, D_MODEL))


def _ffn_body(x, buf_ref, gpre_ref, wup_ref, cw_ref, cb_ref, wdn_ref, gpost_ref, nbuf_ref, ext_ref, act_ref):
    _init_carry(ext_ref, buf_ref, pl.program_id(1) == 0)
    h = _rms(x, gpre_ref[...]).astype(BF16)
    nchunk = D_FF // FFN_CHUNK
    vcols = [slice(j * FFN_CHUNK, (j + 1) * FFN_CHUNK) for j in range(nchunk)]
    gcols = [slice(D_FF + j * FFN_CHUNK, D_FF + (j + 1) * FFN_CHUNK) for j in range(nchunk)]
    up = lambda j: (jnp.dot(h, wup_ref[:, vcols[j]], preferred_element_type=F32),
                    jnp.dot(h, wup_ref[:, gcols[j]], preferred_element_type=F32))
    raw = up(0)
    for j in range(nchunk):
        nxt = up(j + 1) if j + 1 < nchunk else None
        val = _causal_conv(raw[0], ext_ref, vcols[j], cw_ref, cb_ref, FFN_CONV)
        gate = _causal_conv(raw[1], ext_ref, gcols[j], cw_ref, cb_ref, FFN_CONV)
        act_ref[:, vcols[j]] = (jax.nn.gelu(gate, approximate=True) * val).astype(BF16)
        raw = nxt
    _store_history(nbuf_ref, ext_ref)
    f = jnp.dot(act_ref[...], wdn_ref[...], preferred_element_type=F32)
    return x + _rms(f, gpost_ref[...])


def _ffn_scratch(tm, nseg):
    return [pltpu.VMEM((nseg * CARRY_ROWS, 2 * D_FF), F32), pltpu.VMEM((tm, D_FF), BF16)]


KV_WEIGHT_SPECS = ((1, D_MODEL), (D_MODEL, KV_RANK + 2 * QK_ROPE), (1, KV_RANK))


def _kv_stages(x, gin_ref, w_ref, gkv_ref, cos_ref, sin_ref, lat_ref, kpe_ref, kcat_ref, latt_ref):
    box = []

    def project():
        h = _rms(x, gin_ref[...]).astype(BF16)
        box.append(jnp.dot(h, w_ref[...], preferred_element_type=F32))

    def finish():
        hw = box[0]
        lat = _rms(hw[:, :KV_RANK], gkv_ref[...])
        kpe = hw[:, KV_RANK:QK_CAT] * cos_ref[:, :QK_ROPE] + hw[:, QK_CAT:] * sin_ref[:, :QK_ROPE]
        lat_ref[...] = lat
        kpe_ref[...] = kpe
        kcat_ref[:, :KV_RANK] = lat.astype(BF16)
        kcat_ref[:, KV_RANK:] = kpe.astype(BF16)
        latt_ref[...] = lat.T.astype(BF16)

    return project, finish


Q_SCALE = (QK_NOPE + QK_ROPE) ** -0.5 * LOG2E
Q_WEIGHT_SPECS = (
    (1, D_MODEL), (D_MODEL, Q_RANK), (1, Q_RANK), (Q_RANK, MLA_HEADS * QK_NOPE), (MLA_HEADS * QK_ROPE, Q_RANK),
    (MLA_HEADS * QK_ROPE, Q_RANK), (MLA_HEADS, KV_RANK, QK_NOPE))


def _q_body(x, gpre_ref, wdq_ref, gq_ref, wn_ref, wpt_ref, wprt_ref, wuk_ref, cost_ref, sint_ref, qt_ref,
            kv_stages):
    tm = x.shape[0]
    nseg = qt_ref.shape[0]
    seg_len = tm // nseg
    scale = Q_SCALE
    kv_project, kv_finish = kv_stages
    h = _rms(x, gpre_ref[...]).astype(BF16)
    cq_raw = jnp.dot(h, wdq_ref[...], preferred_element_type=F32)
    kv_project()
    cq = _rms(cq_raw, gq_ref[...]).astype(BF16)
    qn = jnp.dot(cq, wn_ref[...], preferred_element_type=F32).astype(BF16)
    qpt = lax.dot_general(wpt_ref[...], cq, NT_DIMS, preferred_element_type=F32)
    qprt = lax.dot_general(wprt_ref[...], cq, NT_DIMS, preferred_element_type=F32)
    kv_finish()
    reps = MLA_HEADS * QK_ROPE // LANE
    cost = jnp.concatenate([cost_ref[...]] * reps, axis=0)
    sint = jnp.concatenate([sint_ref[...]] * reps, axis=0)
    qpet = ((qpt * cost + qprt * sint) * scale).astype(BF16)
    for hd in range(MLA_HEADS):
        qlt = lax.dot_general(wuk_ref[hd], qn[:, hd * QK_NOPE:(hd + 1) * QK_NOPE], NT_DIMS,
                              preferred_element_type=F32)
        qlt = (qlt * scale).astype(BF16)
        for s in range(nseg):
            src = slice(s * seg_len, (s + 1) * seg_len)
            dst = slice(hd * seg_len, (hd + 1) * seg_len)
            qt_ref[s, :KV_RANK, dst] = qlt[:, src]
            qt_ref[s, KV_RANK:, dst] = qpet[hd * QK_ROPE:(hd + 1) * QK_ROPE, src]


def _layer0_tail_kernel(*refs):
    yg_ref, x_ref, buf_ref, cos_ref, sin_ref, cost_ref, sint_ref, wout_ref, gmix_ref = refs[:9]
    ffn_w = refs[9:15]
    kv_w = refs[15:18]
    q_w = refs[18:25]
    x_out_ref, nbuf_ref, lat_ref, kpe_ref, kcat_ref, qt_ref, latt_ref, ext_ref, act_ref = refs[25:]
    x = _mixer_residual(yg_ref[...], x_ref[...], wout_ref, gmix_ref)
    x = _ffn_body(x, buf_ref, *ffn_w, nbuf_ref, ext_ref, act_ref)
    x_out_ref[...] = x
    kv_stages = _kv_stages(x, *kv_w, cos_ref, sin_ref, lat_ref, kpe_ref, kcat_ref, latt_ref)
    _q_body(x, *q_w, cost_ref, sint_ref, qt_ref, kv_stages)


def _layer0_tail(yg, x, ffn_buf, cos, sin, w_out, g_mix, ffn_w, kv_w, q_w, tm):
    b, t, _ = x.shape
    grid = (b, t // tm)
    row = lambda wd: pl.BlockSpec((None, tm, wd), lambda i, j: (i, j, 0))
    nseg = ffn_buf.shape[1]
    buf_spec = pl.BlockSpec((None, nseg, FFN_CONV - 1, 2 * D_FF), lambda i, j: (i, 0, 0, 0))
    tab = pl.BlockSpec((tm, LANE), lambda i, j: (j, 0))
    tab_t = pl.BlockSpec((LANE, tm), lambda i, j: (0, j))
    weight_specs = ([_const_spec(s) for s in ((D_INNER, D_MODEL), (1, D_MODEL))]
                    + [_layer_spec(s, 0) for s in FFN_WEIGHT_SPECS]
                    + [_const_spec(s) for s in KV_WEIGHT_SPECS + Q_WEIGHT_SPECS])
    return pl.pallas_call(
        _layer0_tail_kernel,
        grid=grid,
        in_specs=[row(D_INNER), row(D_MODEL), buf_spec, tab, tab, tab_t, tab_t] + weight_specs,
        out_specs=[
            row(D_MODEL), buf_spec, row(KV_RANK), row(QK_ROPE), row(QK_CAT),
            pl.BlockSpec((None, None, nseg, QK_CAT, MLA_HEADS * tm // nseg), lambda i, j: (i, j, 0, 0, 0)),
            pl.BlockSpec((None, None, KV_RANK, tm), lambda i, j: (i, j, 0, 0)),
        ],
        out_shape=[
            jax.ShapeDtypeStruct((b, t, D_MODEL), F32),
            jax.ShapeDtypeStruct((b, nseg, FFN_CONV - 1, 2 * D_FF), F32),
            jax.ShapeDtypeStruct((b, t, KV_RANK), F32),
            jax.ShapeDtypeStruct((b, t, QK_ROPE), F32),
            jax.ShapeDtypeStruct((b, t, QK_CAT), BF16),
            jax.ShapeDtypeStruct((b, t // tm, nseg, QK_CAT, MLA_HEADS * tm // nseg), BF16),
            jax.ShapeDtypeStruct((b, t // tm, KV_RANK, tm), BF16),
        ],
        scratch_shapes=_ffn_scratch(tm, nseg),
        compiler_params=_params("parallel", "arbitrary"),
        name="layer0_tail",
    )(yg, x, ffn_buf, cos, sin, cos.T, sin.T, w_out, g_mix, *ffn_w, *kv_w, *q_w)


def _layer1_tail_kernel(*refs):
    o_ref, x_ref, buf_ref, wuv_ref, wo_ref, gmix_ref = refs[:6]
    ffn_w = refs[6:12]
    x_out_ref, nbuf_ref, ext_ref, act_ref = refs[12:]
    x = _mixer_residual(_mla_values(o_ref, wuv_ref), x_ref[...], wo_ref, gmix_ref)
    x_out_ref[...] = _ffn_body(x, buf_ref, *ffn_w, nbuf_ref, ext_ref, act_ref)


def _layer1_tail(o_lat, x, ffn_buf, wuv, wo, g_mix, ffn_w, tm):
    b, t, _ = x.shape
    grid = (b, t // tm)
    row = lambda wd: pl.BlockSpec((None, tm, wd), lambda i, j: (i, j, 0))
    nseg = ffn_buf.shape[1]
    buf_spec = pl.BlockSpec((None, nseg, FFN_CONV - 1, 2 * D_FF), lambda i, j: (i, 0, 0, 0))
    weight_specs = ([_const_spec(s) for s in ((MLA_HEADS, KV_RANK, V_DIM), (MLA_HEADS * V_DIM, D_MODEL), (1, D_MODEL))]
                    + [_layer_spec(s, 1) for s in FFN_WEIGHT_SPECS])
    return pl.pallas_call(
        _layer1_tail_kernel,
        grid=grid,
        in_specs=[pl.BlockSpec((None, nseg, MLA_HEADS, tm // nseg, KV_RANK), lambda i, j: (i, 0, 0, j, 0)),
                  row(D_MODEL), buf_spec] + weight_specs,
        out_specs=[row(D_MODEL), buf_spec],
        out_shape=[
            jax.ShapeDtypeStruct((b, t, D_MODEL), F32),
            jax.ShapeDtypeStruct((b, nseg, FFN_CONV - 1, 2 * D_FF), F32),
        ],
        scratch_shapes=_ffn_scratch(tm, nseg),
        compiler_params=_params("parallel", "arbitrary"),
        name="layer1_tail",
    )(o_lat, x, ffn_buf, wuv, wo, g_mix, *ffn_w)


def _attn_kernel(qt_ref, *rest, tq, tk, past, nkv, mode):
    cached = mode == "cached"
    if cached:
        clat_ref, crope_ref, knew_ref = rest[:3]
        rest = rest[3:]
    else:
        k_ref, vt_all = rest[:2]
        rest = rest[2:]
    o_all, m_all, l_all, a_all, acc_all, s_all, p_all_ref = rest
    nb = qt_ref.shape[0]
    qi = pl.program_id(1)
    rows = MLA_HEADS * tq
    nblk = rows // LANE
    blocks = [slice(b * LANE, (b + 1) * LANE) for b in range(nblk)]
    m_all[...] = jnp.full(m_all.shape, -jnp.inf, F32)
    l_all[...] = jnp.zeros(l_all.shape, F32)
    acc_all[...] = jnp.zeros(acc_all.shape, F32)
    n_vis = jnp.minimum(nkv, (past + (qi + 1) * tq + tk - 1) // tk)

    def key_value_tile(bi, j, new):
        if not cached:
            ks = k_ref[bi, pl.ds(pl.multiple_of(j * tk, tk), tk), :]
            return ks, lambda: vt_all[bi, j]
        if new:
            ks = jnp.concatenate([knew_ref[bi], jnp.zeros((tk - knew_ref.shape[1], QK_CAT), BF16)], axis=0)
            return ks, lambda: ks[:, :KV_RANK].astype(F32).T.astype(BF16)
        tile = pl.ds(pl.multiple_of(j * tk, tk), tk)
        lat = clat_ref[bi, tile, :]
        ks = jnp.concatenate([lat.astype(BF16), crope_ref[bi, tile, :].astype(BF16)], axis=1)
        return ks, lambda: lat.T.astype(BF16)

    def scores(bi, j, new=False):
        ks, _ = key_value_tile(bi, j, new)
        return jnp.dot(ks, qt_ref[bi], preferred_element_type=F32)

    def store_scores(bi, slot, s):
        for b, csl in enumerate(blocks):
            s_all[bi, slot, b] = s[:, csl]

    def hidden_bias(j):
        key_chunk = lax.broadcasted_iota(jnp.int32, (tk, LANE), 0) // CHUNK + j * (tk // CHUNK)
        lane = lax.broadcasted_iota(jnp.int32, (tk, LANE), 1)
        pats = []
        for ph in range(max(1, tq // LANE)):
            q_chunk = (past + qi * tq) // CHUNK + ((ph * LANE + lane) % tq) // CHUNK
            pats.append(jnp.where(key_chunk <= q_chunk, 0.0, -jnp.inf))
        return pats

    def absorb(bi, j, masked, new=False):
        m_ref, l_ref, a_ref, acc_ref, p_ref = m_all.at[bi], l_all.at[bi], a_all.at[bi], acc_all.at[bi], p_all_ref.at[bi]
        slot = j % 2
        pats = hidden_bias(j) if masked else None
        for b, csl in enumerate(blocks):
            s = s_all[bi, slot, b]
            if masked:
                s = s + pats[b % len(pats)]
            m_prev = m_ref[:, csl]
            m_new = jnp.maximum(m_prev, jnp.max(s, axis=0, keepdims=True))
            alpha = jnp.exp2(m_prev - m_new)
            p = jnp.exp2(s - m_new)
            l_ref[:, csl] = alpha * l_ref[:, csl] + jnp.sum(p, axis=0, keepdims=True)
            m_ref[:, csl] = m_new
            a_ref[:, csl] = alpha
            p_ref[b] = p.astype(BF16)
        p_all = jnp.concatenate([p_ref[b] for b in range(nblk)], axis=1)
        _, value_tile = key_value_tile(bi, j, new)
        pv = jnp.dot(value_tile(), p_all, preferred_element_type=F32)
        for b, csl in enumerate(blocks):
            acc_ref[b] = a_ref[:, csl] * acc_ref[b] + pv[:, csl]

    for bi in range(nb):
        store_scores(bi, 0, scores(bi, 0))

    def body(j, carry):
        s_new = [scores(bi, j) for bi in range(nb)]
        for bi in range(nb):
            absorb(bi, j - 1, masked=False)
        for bi in range(nb):
            store_scores(bi, j % 2, s_new[bi])
        return carry

    if cached:
        n_cache = past // tk
        lax.fori_loop(1, n_cache, body, 0)
        s_new = [scores(bi, n_cache, new=True) for bi in range(nb)]
        for bi in range(nb):
            absorb(bi, n_cache - 1, masked=False)
        for bi in range(nb):
            store_scores(bi, n_cache % 2, s_new[bi])
        for bi in range(nb):
            absorb(bi, n_cache, masked=True, new=True)
    else:
        lax.fori_loop(1, n_vis, body, 0)
        for bi in range(nb):
            absorb(bi, n_vis - 1, masked=True)
    for bi in range(nb):
        o = jnp.concatenate([acc_all[bi, b] * (1.0 / l_all[bi, :, csl]) for b, csl in enumerate(blocks)], axis=1).T
        o_all[bi] = o.astype(BF16).reshape(MLA_HEADS, tq, KV_RANK)


def _attention(qt, kcat, vt, cache, tq, tk):
    b, ng, _, rows = qt.shape
    t = ng * tq
    past = 0 if cache is None else cache[0].shape[1]
    assert tk % tq == 0 and past % tk == 0 and rows == MLA_HEADS * tq and tq % CHUNK == 0
    nb = ATT_SIDE_BY_SIDE if tq <= ATT_PAIR_MAX_TQ and b % ATT_SIDE_BY_SIDE == 0 else 1
    grid = (b // nb, ng)
    nblk = rows // LANE
    whole = lambda a: pl.BlockSpec((nb,) + a.shape[1:], lambda i, j: (i,) + (0,) * (a.ndim - 1))
    if cache is None:
        nkv = kcat.shape[1] // tk
        assert kcat.shape[1] % tk == 0
        operands = [qt, kcat, vt]
    else:
        nkv = past // tk + 1
        assert ng == 1 and t <= tk
        operands = [qt, cache[0], cache[1], kcat]
    return pl.pallas_call(
        functools.partial(_attn_kernel, tq=tq, tk=tk, past=past, nkv=nkv,
                          mode="tiles" if cache is None else "cached"),
        grid=grid,
        in_specs=[pl.BlockSpec((nb, None, QK_CAT, rows), lambda i, j: (i, j, 0, 0))] + [whole(a) for a in operands[1:]],
        out_specs=pl.BlockSpec((nb, MLA_HEADS, tq, KV_RANK), lambda i, j: (i, 0, j, 0)),
        out_shape=jax.ShapeDtypeStruct((b, MLA_HEADS, t, KV_RANK), BF16),
        scratch_shapes=[
            pltpu.VMEM((nb, 1, rows), F32),
            pltpu.VMEM((nb, 1, rows), F32),
            pltpu.VMEM((nb, 1, rows), F32),
            pltpu.VMEM((nb, nblk, KV_RANK, LANE), F32),
            pltpu.VMEM((nb, 2, nblk, tk, LANE), F32),
            pltpu.VMEM((nb, nblk, tk, LANE), BF16),
        ],
        compiler_params=_params("parallel", "parallel"),
        name="mla_attention",
    )(*operands)


def _rope_tables(past, t):
    half = QK_ROPE // 2
    inv = jnp.exp(-math.log(ROPE_THETA) * jnp.arange(half, dtype=F32) / half)
    ang = (past + jnp.arange(t, dtype=jnp.int32)).astype(F32)[:, None] * inv[None, :]
    reps = 2 * LANE // QK_ROPE
    return jnp.tile(jnp.cos(ang), (1, reps)), jnp.tile(jnp.sin(ang), (1, reps))


def _rotate_half_cols(w):
    shp = w.shape
    w4 = w.reshape(shp[:-1] + (shp[-1] // QK_ROPE, 2, QK_ROPE // 2))
    return jnp.concatenate([-w4[..., 1:2, :], w4[..., 0:1, :]], axis=-2).reshape(shp)


def _tile(t, pref):
    return pref if t % pref == 0 else t


def _prep_weights(norm_mix_pre, norm_mix_post, norm_ffn_pre, norm_ffn_post,
                  ssm_w_in, ssm_conv_w, ssm_conv_b, ssm_dt_bias, ssm_a_log, ssm_d, ssm_norm, ssm_w_out,
                  kv_norm_in, kv_w_dkv, kv_norm, kv_w_kr, kv_w_uk, kv_w_uv,
                  mla_w_dq, mla_q_norm, mla_w_uq, mla_w_o,
                  ffn_w_up, ffn_conv_w, ffn_conv_b, ffn_w_down):
    w = {}
    w_in = ssm_w_in[0]
    w["wzx"] = w_in.astype(BF16)
    w["wdt"] = jnp.pad(w_in[:, D_INNER + CONV_DIM:], ((0, 0), (0, LANE - SSM_HEADS))).astype(BF16)
    w_dt_t = w_in[:, D_INNER + CONV_DIM:].T
    w["wdtt"] = jnp.concatenate([w_dt_t[0::2], w_dt_t[1::2]]).astype(BF16)
    w["ssm_cw"] = ssm_conv_w[0]
    w["ssm_cb"] = ssm_conv_b[0][None, :]
    w["dtb"] = ssm_dt_bias[0][None, :]
    w["dtbt"] = jnp.concatenate([ssm_dt_bias[0][0::2], ssm_dt_bias[0][1::2]])[:, None]
    a = -jnp.exp(ssm_a_log[0].astype(F32))
    w["arow"] = a[None, :]
    w["acol2"] = jnp.broadcast_to(a.reshape(SSM_HEADS // 2, 2, 1),
                                  (SSM_HEADS // 2, 2, CHUNK)).reshape(SSM_HEADS // 2, 2 * CHUNK)
    w["expand"] = jnp.repeat(jnp.eye(SSM_HEADS, dtype=BF16), SSM_HEADDIM, axis=1)
    w["dskip"] = jnp.repeat(ssm_d[0], SSM_HEADDIM)[None, :]
    w["gn"] = ssm_norm[0][None, :]
    w["w_out"] = ssm_w_out[0].astype(BF16)
    w["g_mix_pre"] = norm_mix_pre[:, None, :]
    w["g_mix_post"] = norm_mix_post[:, None, :]
    w["g_ffn_pre"] = norm_ffn_pre[:, None, :]
    w["g_ffn_post"] = norm_ffn_post[:, None, :]
    w["ffn_up"] = ffn_w_up.astype(BF16)
    w["ffn_cw"] = ffn_conv_w
    w["ffn_cb"] = ffn_conv_b[:, None, :]
    w["ffn_down"] = ffn_w_down.astype(BF16)
    w["kv_gin"] = kv_norm_in[None, :]
    w["kv_w"] = jnp.concatenate([kv_w_dkv, kv_w_kr, _rotate_half_cols(kv_w_kr)], axis=-1).astype(BF16)
    w["kv_g"] = kv_norm[None, :]
    w["wdq"] = mla_w_dq[0].astype(BF16)
    w["gq"] = mla_q_norm[0][None, :]
    wuq = mla_w_uq[0].reshape(Q_RANK, MLA_HEADS, QK_NOPE + QK_ROPE)
    w["wq_nope"] = wuq[:, :, :QK_NOPE].reshape(Q_RANK, MLA_HEADS * QK_NOPE).astype(BF16)
    wq_pe = wuq[:, :, QK_NOPE:].reshape(Q_RANK, MLA_HEADS * QK_ROPE)
    w["wq_pe_t"] = wq_pe.T.astype(BF16)
    w["wq_pe_rot_t"] = _rotate_half_cols(wq_pe).T.astype(BF16)
    w["wuk"] = jnp.transpose(kv_w_uk, (1, 0, 2)).astype(BF16)
    w["wuv"] = jnp.transpose(kv_w_uv, (1, 0, 2)).astype(BF16)
    w["wo"] = mla_w_o[0].astype(BF16)
    return w


def _trunk(x, ssm_conv_buf, ssm_state, ffn_buf, past_kcat, w):
    b, t, _ = x.shape
    past = 0 if past_kcat is None else past_kcat[0].shape[1]
    assert t % CHUNK == 0 and past % CHUNK == 0
    tm = TOKEN_TILE
    nseq = tm // t if t < tm and tm % t == 0 and b % (tm // t) == 0 else 1
    if t % tm != 0 and nseq == 1:
        tm = t
    bb, tt = b // nseq, nseq * t
    pack = lambda a: a.reshape((bb, tt) + a.shape[2:])
    unpack = lambda a: a.reshape((b, t) + a.shape[2:])
    pack_hist = lambda a: a.reshape((bb, nseq) + a.shape[1:])
    unpack_hist = lambda a: a.reshape((b,) + a.shape[2:])

    z, xbc, dt, dtt2, new_conv = _mamba_in(pack(x), pack_hist(ssm_conv_buf), w["g_mix_pre"][0], w["wzx"], w["wdt"],
                                           w["wdtt"], w["ssm_cw"], w["ssm_cb"], w["dtb"], w["dtbt"], tm)
    nc = _tile(t, SSD_TILE) // CHUNK
    dtt2 = dtt2.reshape(b, t // CHUNK, SSM_HEADS // 2, 2 * CHUNK)
    yg, new_state = _ssd(unpack(xbc), unpack(z), unpack(dt), dtt2, w["arow"], w["acol2"], w["expand"], w["dskip"],
                         w["gn"], None if ssm_state is None else ssm_state.reshape(b, D_INNER, D_STATE), nc)
    new_state = new_state.reshape(b, SSM_HEADS, SSM_HEADDIM, D_STATE)

    ffn_w = lambda i: (w["g_ffn_pre"], w["ffn_up"], w["ffn_cw"], w["ffn_cb"], w["ffn_down"], w["g_ffn_post"])
    cos, sin = _rope_tables(past, t)
    x, new_ffn0, lat, kpe, kcat, q, latt = _layer0_tail(
        pack(yg), pack(x), pack_hist(ffn_buf[0]), jnp.tile(cos, (nseq, 1)), jnp.tile(sin, (nseq, 1)),
        w["w_out"], w["g_mix_post"][0], ffn_w(0), (w["kv_gin"], w["kv_w"], w["kv_g"]),
        (w["g_mix_pre"][1], w["wdq"], w["gq"], w["wq_nope"], w["wq_pe_t"], w["wq_pe_rot_t"], w["wuk"]), tm)
    lat, kpe, kcat = unpack(lat), unpack(kpe), unpack(kcat)
    tq = _tile(t, ATT_TK)
    assert tq == (t if nseq > 1 else tm)
    q = q.reshape(b, t // tq, QK_CAT, MLA_HEADS * tq)

    if past_kcat is not None:
        vt = None
    elif tm == ATT_TK:
        vt = latt
    else:
        vt = kcat[:, :, :KV_RANK].reshape(b, t // ATT_TK, ATT_TK, KV_RANK).transpose(0, 1, 3, 2)
    o_lat = _attention(q, kcat, vt, past_kcat, tq, ATT_TK)
    o_lat = o_lat.reshape(bb, nseq, MLA_HEADS, t, KV_RANK)
    x, new_ffn1 = _layer1_tail(o_lat, x, pack_hist(ffn_buf[1]), w["wuv"], w["wo"], w["g_mix_post"][1], ffn_w(1), tm)
    return (unpack(x), new_state[None], unpack_hist(new_conv)[None],
            jnp.stack([unpack_hist(new_ffn0), unpack_hist(new_ffn1)]), lat, kpe)


def kernel(x_prompt, x_sample, state_ssm, state_ssm_conv, state_ffn_conv, cache_kv_latent, cache_k_rope, norm_mix_pre, norm_mix_post, norm_ffn_pre, norm_ffn_post, ssm_w_in, ssm_conv_w, ssm_conv_b, ssm_dt_bias, ssm_a_log, ssm_d, ssm_norm, ssm_w_out, kv_norm_in, kv_w_dkv, kv_norm, kv_w_kr, kv_w_uk, kv_w_uv, mla_w_dq, mla_q_norm, mla_w_uq, mla_w_o, ffn_w_up, ffn_conv_w, ffn_conv_b, ffn_w_down):
    w = _prep_weights(norm_mix_pre, norm_mix_post, norm_ffn_pre, norm_ffn_post,
                      ssm_w_in, ssm_conv_w, ssm_conv_b, ssm_dt_bias, ssm_a_log, ssm_d, ssm_norm, ssm_w_out,
                      kv_norm_in, kv_w_dkv, kv_norm, kv_w_kr, kv_w_uk, kv_w_uv,
                      mla_w_dq, mla_q_norm, mla_w_uq, mla_w_o,
                      ffn_w_up, ffn_conv_w, ffn_conv_b, ffn_w_down)
    bp = x_prompt.shape[0]
    dtp = x_prompt.dtype
    y_p, p_ssm, p_conv, p_ffn, p_lat, p_kpe = _trunk(
        x_prompt,
        jnp.zeros((bp, SSM_CONV - 1, CONV_DIM), dtp),
        None,
        jnp.zeros((2, bp, FFN_CONV - 1, 2 * D_FF), dtp),
        None, w)
    y_s, s_ssm, s_conv, s_ffn, s_lat, s_kpe = _trunk(
        x_sample, state_ssm_conv[0], state_ssm[0], state_ffn_conv, (cache_kv_latent, cache_k_rope), w)
    return (y_p, y_s, p_ssm, p_conv, p_ffn, p_lat, p_kpe, s_ssm, s_conv, s_ffn, s_lat, s_kpe)
```

```python
import functools
import math

import jax
import jax.numpy as jnp
from jax import lax
from jax.experimental import pallas as pl
from jax.experimental.pallas import tpu as pltpu

F32 = jnp.float32
BF16 = jnp.bfloat16

D_MODEL = 1024
CHUNK = 64
EPS = 1e-6
D_INNER = 2048
SSM_HEADDIM = 64
SSM_HEADS = 32
SSM_GROUPS = 4
SSM_HPG = 8
D_STATE = 128
SSM_CONV = 4
GROUP_W = D_INNER // SSM_GROUPS
BC_W = SSM_GROUPS * D_STATE
CONV_DIM = D_INNER + 2 * BC_W
MLA_HEADS = 16
Q_RANK = 384
KV_RANK = 256
QK_NOPE = 128
QK_ROPE = 64
V_DIM = 128
QK_CAT = KV_RANK + QK_ROPE
ROPE_THETA = 10000.0
D_FF = 2816
FFN_CONV = 3
LANE = 128
CARRY_ROWS = 8
TOKEN_TILE = 256
COL_CHUNK = 512
FFN_CHUNK = 256
DOWN_SLAB = 256
W_SLAB = 256
SSD_TILE = 512
SSD_UNROLL = 8
ATT_TK = 256
ATT_TQ = 256
ATT_STEP_ROWS = 4096
LOG2E = 1.4426950408889634
VMEM_LIMIT = 56 * 1024 * 1024

NT_DIMS = (((1,), (1,)), ((), ()))
TN_DIMS = (((0,), (0,)), ((), ()))


def _rms(x, g):
    return x * lax.rsqrt(jnp.mean(x * x, axis=-1, keepdims=True) + EPS) * g


def _silu(x):
    return x * (1.0 / (1.0 + jnp.exp(-x)))


def _softplus(x):
    return jnp.maximum(x, 0.0) + jnp.log(1.0 + jnp.exp(-jnp.abs(x)))


def _const_spec(shape):
    zeros = (0,) * len(shape)
    return pl.BlockSpec(shape, lambda *_: zeros, pipeline_mode=pl.Buffered(1))


def _layer_spec(shape, layer):
    zeros = (0,) * len(shape)
    return pl.BlockSpec((None,) + tuple(shape), lambda *_: (layer,) + zeros, pipeline_mode=pl.Buffered(1))


def _params(*sem):
    return pltpu.CompilerParams(dimension_semantics=sem, vmem_limit_bytes=VMEM_LIMIT)


def _init_carry(carry_ref, hist_ref, first_step):
    nseg, hist, _ = hist_ref.shape

    @pl.when(first_step)
    def _():
        carry_ref[...] = jnp.zeros(carry_ref.shape, F32)
        for s in range(nseg):
            carry_ref[(s + 1) * CARRY_ROWS - hist:(s + 1) * CARRY_ROWS, :] = hist_ref[s]


def _store_history(nbuf_ref, carry_ref):
    nseg, hist, _ = nbuf_ref.shape
    for s in range(nseg):
        nbuf_ref[s] = carry_ref[(s + 1) * CARRY_ROWS - hist:(s + 1) * CARRY_ROWS, :]


def _causal_conv(raw, carry_ref, cols, w_ref, b_ref, taps):
    nseg = carry_ref.shape[0] // CARRY_ROWS
    seg_len = raw.shape[0] // nseg
    hist = taps - 1
    outs = []
    for s in range(nseg):
        cur = raw[s * seg_len:(s + 1) * seg_len, :]
        prev = carry_ref[s * CARRY_ROWS:(s + 1) * CARRY_ROWS, cols]
        ext = jnp.concatenate([prev, cur], axis=0)
        acc = b_ref[:, cols] + cur * w_ref[hist:hist + 1, cols]
        for k in range(1, taps):
            acc = acc + ext[CARRY_ROWS - k:CARRY_ROWS - k + seg_len, :] * w_ref[hist - k:hist - k + 1, cols]
        carry_ref[s * CARRY_ROWS:(s + 1) * CARRY_ROWS, cols] = cur[seg_len - CARRY_ROWS:, :]
        outs.append(acc)
    return outs[0] if nseg == 1 else jnp.concatenate(outs, axis=0)


def _mamba_in_kernel(x_ref, buf_ref, g_ref, wzx_ref, wdt_ref, wdtt_ref, cw_ref, cb_ref, dtb_ref, dtbt_ref,
                     z_ref, xbc_ref, dt_ref, dtt_ref, nbuf_ref, ext_ref):
    _init_carry(ext_ref, buf_ref, pl.program_id(1) == 0)
    h = _rms(x_ref[...], g_ref[...]).astype(BF16)
    nz = D_INNER // COL_CHUNK
    for j in range(nz):
        cols = slice(j * COL_CHUNK, (j + 1) * COL_CHUNK)
        z_ref[:, cols] = jnp.dot(h, wzx_ref[j], preferred_element_type=F32).astype(BF16)
    nchunk = CONV_DIM // COL_CHUNK
    proj = lambda j: jnp.dot(h, wzx_ref[nz + j], preferred_element_type=F32)
    raw = proj(0)
    for j in range(nchunk):
        nxt = proj(j + 1) if j + 1 < nchunk else None
        cols = slice(j * COL_CHUNK, (j + 1) * COL_CHUNK)
        xbc_ref[:, cols] = _silu(_causal_conv(raw, ext_ref, cols, cw_ref, cb_ref, SSM_CONV)).astype(BF16)
        raw = nxt
    _store_history(nbuf_ref, ext_ref)
    dt = jnp.dot(h, wdt_ref[...], preferred_element_type=F32)[:, :SSM_HEADS]
    dt_ref[...] = _softplus(dt + dtb_ref[...])
    dtt = _softplus(lax.dot_general(wdtt_ref[...], h, NT_DIMS, preferred_element_type=F32) + dtbt_ref[...])
    for c in range(h.shape[0] // CHUNK):
        for i in range(2):
            dtt_ref[c, :, i * CHUNK:(i + 1) * CHUNK] = dtt[i * (SSM_HEADS // 2):(i + 1) * (SSM_HEADS // 2),
                                                          c * CHUNK:(c + 1) * CHUNK]


def _mamba_in(x, conv_buf, g, wzx, wdt, wdtt, cw, cb, dtb, dtbt, tm):
    b, t, _ = x.shape
    nseg = conv_buf.shape[1]
    grid = (b, t // tm)
    row = lambda w: pl.BlockSpec((None, tm, w), lambda i, j: (i, j, 0))
    return pl.pallas_call(
        _mamba_in_kernel,
        grid=grid,
        in_specs=[
            row(D_MODEL),
            pl.BlockSpec((None, nseg, SSM_CONV - 1, CONV_DIM), lambda i, j: (i, 0, 0, 0)),
            _const_spec((1, D_MODEL)),
            _const_spec(((D_INNER + CONV_DIM) // COL_CHUNK, D_MODEL, COL_CHUNK)),
            _const_spec((D_MODEL, LANE)),
            _const_spec((SSM_HEADS, D_MODEL)),
            _const_spec((SSM_CONV, CONV_DIM)),
            _const_spec((1, CONV_DIM)),
            _const_spec((1, SSM_HEADS)),
            _const_spec((SSM_HEADS, 1)),
        ],
        out_specs=[
            row(D_INNER),
            row(CONV_DIM),
            row(SSM_HEADS),
            pl.BlockSpec((None, tm // CHUNK, SSM_HEADS // 2, 2 * CHUNK), lambda i, j: (i, j, 0, 0)),
            pl.BlockSpec((None, nseg, SSM_CONV - 1, CONV_DIM), lambda i, j: (i, 0, 0, 0)),
        ],
        out_shape=[
            jax.ShapeDtypeStruct((b, t, D_INNER), BF16),
            jax.ShapeDtypeStruct((b, t, CONV_DIM), BF16),
            jax.ShapeDtypeStruct((b, t, SSM_HEADS), F32),
            jax.ShapeDtypeStruct((b, t // CHUNK, SSM_HEADS // 2, 2 * CHUNK), F32),
            jax.ShapeDtypeStruct((b, nseg, SSM_CONV - 1, CONV_DIM), F32),
        ],
        scratch_shapes=[pltpu.VMEM((nseg * CARRY_ROWS, CONV_DIM), F32)],
        compiler_params=_params("parallel", "arbitrary"),
        name="mamba_in",
    )(x, conv_buf, g, wzx, wdt, wdtt, cw, cb, dtb, dtbt)


def _ssd_kernel(xs_ref, bm_ref, cm_ref, z_ref, dt_ref, dtt_ref, arow_ref, acol_ref, exp_ref, dskip_ref, gn_ref,
                *rest, nc, has_state):
    h0_ref = rest[0] if has_state else None
    yg_ref, hout_ref, st_ref = rest[1:] if has_state else rest
    step = pl.program_id(1)

    @pl.when(step == 0)
    def _():
        st_ref[...] = h0_ref[...].T if has_state else jnp.zeros(st_ref.shape, F32)

    li = lax.broadcasted_iota(jnp.int32, (CHUNK, CHUNK), 0)
    si = lax.broadcasted_iota(jnp.int32, (CHUNK, CHUNK), 1)
    tril = (si <= li).astype(F32)
    pi = lax.broadcasted_iota(jnp.int32, (LANE, LANE), 0)
    pj = lax.broadcasted_iota(jnp.int32, (LANE, LANE), 1)
    same_head = (pi // CHUNK) == (pj // CHUNK)
    triu2 = jnp.where(same_head, (pi <= pj).astype(F32), 0.0)
    l2 = lax.broadcasted_iota(jnp.int32, (CHUNK, LANE), 0)
    s2 = lax.broadcasted_iota(jnp.int32, (CHUNK, LANE), 1) % CHUNK
    causal2 = s2 <= l2
    expand = exp_ref[...]

    gsls = [slice(g * GROUP_W, (g + 1) * GROUP_W) for g in range(SSM_GROUPS)]

    def decays(c):
        rows = pl.ds(pl.multiple_of(c * CHUNK, CHUNK), CHUNK)
        dt = dt_ref[rows, :]
        dtt2 = dtt_ref[c]
        acs = jnp.dot(tril, dt * arow_ref[...], precision=lax.Precision.HIGHEST,
                      preferred_element_type=F32)
        acst2 = jnp.dot(dtt2 * acol_ref[...], triu2, precision=lax.Precision.HIGHEST,
                        preferred_element_type=F32)
        last = acs[CHUNK - 1:CHUNK, :]
        wst = (jnp.exp(last - acs) * dt).astype(BF16)
        hi = acs.astype(BF16)
        r1 = acs - hi.astype(F32)
        mid = r1.astype(BF16)
        lo = (r1 - mid.astype(F32)).astype(BF16)
        acs_w = (jnp.dot(hi, expand, preferred_element_type=F32) + jnp.dot(mid, expand, preferred_element_type=F32)
                 + jnp.dot(lo, expand, preferred_element_type=F32))
        e_acs_w = jnp.exp(acs_w)
        wst_w = jnp.dot(wst, expand, preferred_element_type=F32)
        decay_w = jnp.exp(acs_w[CHUNK - 1:CHUNK, :])
        return dtt2, acst2, acs_w, e_acs_w, wst_w, decay_w

    def through_state(c, wst_w, decay_w):
        rows = pl.ds(pl.multiple_of(c * CHUNK, CHUNK), CHUNK)
        cbs, yoffs = [], []
        for g, gsl in enumerate(gsls):
            bg = bm_ref[rows, g * D_STATE:(g + 1) * D_STATE]
            cg = cm_ref[rows, g * D_STATE:(g + 1) * D_STATE]
            cbs.append(lax.dot_general(cg, bg, NT_DIMS, preferred_element_type=F32))
            stg = st_ref[:, gsl]
            yoffs.append(jnp.dot(cg, stg.astype(BF16), preferred_element_type=F32))
            xw = (xs_ref[rows, gsl].astype(F32) * wst_w[:, gsl]).astype(BF16)
            st_ref[:, gsl] = stg * decay_w[:, gsl] + lax.dot_general(bg, xw, TN_DIMS, preferred_element_type=F32)
        return cbs, yoffs

    def within_chunk(c, dtt2, acst2, acs_w, e_acs_w, cbs, yoffs):
        rows = pl.ds(pl.multiple_of(c * CHUNK, CHUNK), CHUNK)
        for g, gsl in enumerate(gsls):
            xg = xs_ref[rows, gsl]
            cb2 = jnp.concatenate([cbs[g], cbs[g]], axis=1)
            ys = []
            for jj in range(SSM_HPG // 2):
                j = g * (SSM_HPG // 2) + jj
                seg = acs_w[:, j * LANE:(j + 1) * LANE] - acst2[j:j + 1, :]
                wts = cb2 * jnp.exp(jnp.where(causal2, seg, -jnp.inf)) * dtt2[j:j + 1, :]
                xp = xg[:, jj * LANE:(jj + 1) * LANE]
                xbd = jnp.where(same_head, jnp.concatenate([xp, xp], axis=0), jnp.zeros((), BF16))
                ys.append(jnp.dot(wts.astype(BF16), xbd, preferred_element_type=F32))
            y = jnp.concatenate(ys, axis=1)
            y = y + yoffs[g] * e_acs_w[:, gsl] + xg.astype(F32) * dskip_ref[:, gsl]
            yg = y * _silu(z_ref[rows, gsl].astype(F32))
            yg_ref[rows, gsl] = _rms(yg, gn_ref[:, gsl]).astype(BF16)

    unroll = min(nc, SSD_UNROLL)

    def body(i, carry):
        cs = [i * unroll + k for k in range(unroll)]
        pre = [decays(c) for c in cs]
        mid = [through_state(c, p[4], p[5]) for c, p in zip(cs, pre)]
        for c, p, m in zip(cs, pre, mid):
            within_chunk(c, p[0], p[1], p[2], p[3], m[0], m[1])
        return carry

    lax.fori_loop(0, nc // unroll, body, 0)

    @pl.when(step == pl.num_programs(1) - 1)
    def _():
        hout_ref[...] = st_ref[...].T


def _ssd(xbc, z, dt, dtt2, arow, acol2, expand, dskip, gn, h0t, nc):
    b, t, _ = xbc.shape
    lt = nc * CHUNK
    grid = (b, t // lt)
    state_spec = pl.BlockSpec((None, D_INNER, D_STATE), lambda i, j: (i, 0, 0))
    state_in = [] if h0t is None else [h0t]
    return pl.pallas_call(
        functools.partial(_ssd_kernel, nc=nc, has_state=h0t is not None),
        grid=grid,
        in_specs=[
            pl.BlockSpec((None, lt, D_INNER), lambda i, j: (i, j, 0)),
            pl.BlockSpec((None, lt, BC_W), lambda i, j: (i, j, D_INNER // BC_W)),
            pl.BlockSpec((None, lt, BC_W), lambda i, j: (i, j, D_INNER // BC_W + 1)),
            pl.BlockSpec((None, lt, D_INNER), lambda i, j: (i, j, 0)),
            pl.BlockSpec((None, lt, SSM_HEADS), lambda i, j: (i, j, 0)),
            pl.BlockSpec((None, nc, SSM_HEADS // 2, 2 * CHUNK), lambda i, j: (i, j, 0, 0)),
            _const_spec((1, SSM_HEADS)),
            _const_spec((SSM_HEADS // 2, 2 * CHUNK)),
            _const_spec((SSM_HEADS, D_INNER)),
            _const_spec((1, D_INNER)),
            _const_spec((1, D_INNER)),
        ] + [state_spec] * len(state_in),
        out_specs=[
            pl.BlockSpec((None, lt, D_INNER), lambda i, j: (i, j, 0)),
            state_spec,
        ],
        out_shape=[
            jax.ShapeDtypeStruct((b, t, D_INNER), BF16),
            jax.ShapeDtypeStruct((b, D_INNER, D_STATE), F32),
        ],
        scratch_shapes=[pltpu.VMEM((D_STATE, D_INNER), F32)],
        compiler_params=_params("parallel", "arbitrary"),
        name="ssd_scan",
    )(xbc, xbc, xbc, z, dt, dtt2, arow, acol2, expand, dskip, gn, *state_in)


def _whole(w_ref):
    return jnp.concatenate([w_ref[k] for k in range(w_ref.shape[0])], axis=1)


def _mixer_residual(mix_bf16, x, w_ref, g_ref):
    return x + _rms(jnp.dot(mix_bf16, _whole(w_ref), preferred_element_type=F32), g_ref[...])


def _mla_values(o_ref, wuv_ref):
    nseg = o_ref.shape[0]
    parts = []
    for h in range(MLA_HEADS):
        o_h = o_ref[0, h] if nseg == 1 else jnp.concatenate([o_ref[s, h] for s in range(nseg)], axis=0)
        parts.append(jnp.dot(o_h, wuv_ref[h], preferred_element_type=F32).astype(BF16))
    return jnp.concatenate(parts, axis=-1)


FFN_WEIGHT_SPECS = (
    (1, D_MODEL), (2 * D_FF // FFN_CHUNK, D_MODEL, FFN_CHUNK), (FFN_CONV, 2 * D_FF), (1, 2 * D_FF),
    (D_MODEL // DOWN_SLAB, D_FF, DOWN_SLAB), (1, D_MODEL))


def _ffn_body(x, buf_ref, gpre_ref, wup_ref, cw_ref, cb_ref, wdn_ref, gpost_ref, nbuf_ref, ext_ref, act_ref):
    _init_carry(ext_ref, buf_ref, pl.program_id(1) == 0)
    h = _rms(x, gpre_ref[...]).astype(BF16)
    nchunk = D_FF // FFN_CHUNK
    vcols = [slice(j * FFN_CHUNK, (j + 1) * FFN_CHUNK) for j in range(nchunk)]
    gcols = [slice(D_FF + j * FFN_CHUNK, D_FF + (j + 1) * FFN_CHUNK) for j in range(nchunk)]
    up = lambda j: (jnp.dot(h, wup_ref[j], preferred_element_type=F32),
                    jnp.dot(h, wup_ref[nchunk + j], preferred_element_type=F32))
    raw = up(0)
    for j in range(nchunk):
        nxt = up(j + 1) if j + 1 < nchunk else None
        val = _causal_conv(raw[0], ext_ref, vcols[j], cw_ref, cb_ref, FFN_CONV)
        gate = _causal_conv(raw[1], ext_ref, gcols[j], cw_ref, cb_ref, FFN_CONV)
        act_ref[:, vcols[j]] = (jax.nn.gelu(gate, approximate=True) * val).astype(BF16)
        raw = nxt
    _store_history(nbuf_ref, ext_ref)
    f = jnp.dot(act_ref[...], _whole(wdn_ref), preferred_element_type=F32)
    return x + _rms(f, gpost_ref[...])


def _ffn_scratch(tm, nseg):
    return [pltpu.VMEM((nseg * CARRY_ROWS, 2 * D_FF), F32), pltpu.VMEM((tm, D_FF), BF16)]


KV_WEIGHT_SPECS = ((1, D_MODEL), ((KV_RANK + 2 * QK_ROPE) // LANE, D_MODEL, LANE), (1, KV_RANK))


def _kv_stages(x, gin_ref, w_ref, gkv_ref, cos_ref, sin_ref, lat_ref, kpe_ref, kcat_ref, latt_ref):
    box = []

    def project():
        h = _rms(x, gin_ref[...]).astype(BF16)
        box.append(jnp.dot(h, _whole(w_ref), preferred_element_type=F32))

    def finish():
        hw = box[0]
        lat = _rms(hw[:, :KV_RANK], gkv_ref[...])
        kpe = hw[:, KV_RANK:QK_CAT] * cos_ref[:, :QK_ROPE] + hw[:, QK_CAT:] * sin_ref[:, :QK_ROPE]
        lat_ref[...] = lat
        kpe_ref[...] = kpe
        kcat_ref[:, :KV_RANK] = lat.astype(BF16)
        kcat_ref[:, KV_RANK:] = kpe.astype(BF16)
        latt_ref[...] = lat.T.astype(BF16)

    return project, finish


Q_SCALE = (QK_NOPE + QK_ROPE) ** -0.5 * LOG2E
Q_WEIGHT_SPECS = (
    (1, D_MODEL), (Q_RANK // LANE, D_MODEL, LANE), (1, Q_RANK),
    (MLA_HEADS * QK_NOPE // W_SLAB, Q_RANK, W_SLAB), (MLA_HEADS * QK_ROPE, Q_RANK),
    (MLA_HEADS * QK_ROPE, Q_RANK), (MLA_HEADS, KV_RANK, QK_NOPE))


def _q_body(x, gpre_ref, wdq_ref, gq_ref, wn_ref, wpt_ref, wprt_ref, wuk_ref, cost_ref, sint_ref, qt_ref,
            kv_stages):
    tm = x.shape[0]
    nseg = qt_ref.shape[0]
    seg_len = tm // nseg
    scale = Q_SCALE
    kv_project, kv_finish = kv_stages
    h = _rms(x, gpre_ref[...]).astype(BF16)
    cq_raw = jnp.dot(h, _whole(wdq_ref), preferred_element_type=F32)
    kv_project()
    cq = _rms(cq_raw, gq_ref[...]).astype(BF16)
    qn = jnp.dot(cq, _whole(wn_ref), preferred_element_type=F32).astype(BF16)
    qpt = lax.dot_general(wpt_ref[...], cq, NT_DIMS, preferred_element_type=F32)
    qprt = lax.dot_general(wprt_ref[...], cq, NT_DIMS, preferred_element_type=F32)
    kv_finish()
    reps = MLA_HEADS * QK_ROPE // LANE
    cost = jnp.concatenate([cost_ref[...]] * reps, axis=0)
    sint = jnp.concatenate([sint_ref[...]] * reps, axis=0)
    qpet = ((qpt * cost + qprt * sint) * scale).astype(BF16)
    for hd in range(MLA_HEADS):
        qlt = lax.dot_general(wuk_ref[hd], qn[:, hd * QK_NOPE:(hd + 1) * QK_NOPE], NT_DIMS,
                              preferred_element_type=F32)
        qlt = (qlt * scale).astype(BF16)
        hps = qt_ref.shape[-1] // seg_len
        for s in range(nseg):
            src = slice(s * seg_len, (s + 1) * seg_len)
            dst = slice((hd % hps) * seg_len, (hd % hps + 1) * seg_len)
            qt_ref[s, hd // hps, :KV_RANK, dst] = qlt[:, src]
            qt_ref[s, hd // hps, KV_RANK:, dst] = qpet[hd * QK_ROPE:(hd + 1) * QK_ROPE, src]


def _layer0_tail_kernel(*refs):
    yg_ref, x_ref, buf_ref, cos_ref, sin_ref, cost_ref, sint_ref, wout_ref, gmix_ref = refs[:9]
    ffn_w = refs[9:15]
    kv_w = refs[15:18]
    q_w = refs[18:25]
    x_out_ref, nbuf_ref, lat_ref, kpe_ref, kcat_ref, qt_ref, latt_ref, ext_ref, act_ref = refs[25:]
    x = _mixer_residual(yg_ref[...], x_ref[...], wout_ref, gmix_ref)
    x = _ffn_body(x, buf_ref, *ffn_w, nbuf_ref, ext_ref, act_ref)
    x_out_ref[...] = x
    kv_stages = _kv_stages(x, *kv_w, cos_ref, sin_ref, lat_ref, kpe_ref, kcat_ref, latt_ref)
    _q_body(x, *q_w, cost_ref, sint_ref, qt_ref, kv_stages)


def _layer0_tail(yg, x, ffn_buf, cos, sin, w_out, g_mix, ffn_w, kv_w, q_w, tm, qseg):
    b, t, _ = x.shape
    qslabs = min(MLA_HEADS, MLA_HEADS * (tm // qseg) // LANE)
    grid = (b, t // tm)
    row = lambda wd: pl.BlockSpec((None, tm, wd), lambda i, j: (i, j, 0))
    nseg = ffn_buf.shape[1]
    buf_spec = pl.BlockSpec((None, nseg, FFN_CONV - 1, 2 * D_FF), lambda i, j: (i, 0, 0, 0))
    tab = pl.BlockSpec((tm, LANE), lambda i, j: (j, 0))
    tab_t = pl.BlockSpec((LANE, tm), lambda i, j: (0, j))
    weight_specs = ([_const_spec(s) for s in ((D_MODEL // W_SLAB, D_INNER, W_SLAB), (1, D_MODEL))]
                    + [_layer_spec(s, 0) for s in FFN_WEIGHT_SPECS]
                    + [_const_spec(s) for s in KV_WEIGHT_SPECS + Q_WEIGHT_SPECS])
    return pl.pallas_call(
        _layer0_tail_kernel,
        grid=grid,
        in_specs=[row(D_INNER), row(D_MODEL), buf_spec, tab, tab, tab_t, tab_t] + weight_specs,
        out_specs=[
            row(D_MODEL), buf_spec, row(KV_RANK), row(QK_ROPE), row(QK_CAT),
            pl.BlockSpec((None, None, qseg, qslabs, QK_CAT, MLA_HEADS * tm // qseg // qslabs),
                         lambda i, j: (i, j, 0, 0, 0, 0)),
            pl.BlockSpec((None, None, KV_RANK, tm), lambda i, j: (i, j, 0, 0)),
        ],
        out_shape=[
            jax.ShapeDtypeStruct((b, t, D_MODEL), F32),
            jax.ShapeDtypeStruct((b, nseg, FFN_CONV - 1, 2 * D_FF), F32),
            jax.ShapeDtypeStruct((b, t, KV_RANK), F32),
            jax.ShapeDtypeStruct((b, t, QK_ROPE), F32),
            jax.ShapeDtypeStruct((b, t, QK_CAT), BF16),
            jax.ShapeDtypeStruct((b, t // tm, qseg, qslabs, QK_CAT, MLA_HEADS * tm // qseg // qslabs), BF16),
            jax.ShapeDtypeStruct((b, t // tm, KV_RANK, tm), BF16),
        ],
        scratch_shapes=_ffn_scratch(tm, nseg),
        compiler_params=_params("parallel", "arbitrary"),
        name="layer0_tail",
    )(yg, x, ffn_buf, cos, sin, cos.T, sin.T, w_out, g_mix, *ffn_w, *kv_w, *q_w)


def _layer1_tail_kernel(*refs):
    o_ref, x_ref, buf_ref, wuv_ref, wo_ref, gmix_ref = refs[:6]
    ffn_w = refs[6:12]
    x_out_ref, nbuf_ref, ext_ref, act_ref = refs[12:]
    x = _mixer_residual(_mla_values(o_ref, wuv_ref), x_ref[...], wo_ref, gmix_ref)
    x_out_ref[...] = _ffn_body(x, buf_ref, *ffn_w, nbuf_ref, ext_ref, act_ref)


def _layer1_tail(o_lat, x, ffn_buf, wuv, wo, g_mix, ffn_w, tm):
    b, t, _ = x.shape
    grid = (b, t // tm)
    row = lambda wd: pl.BlockSpec((None, tm, wd), lambda i, j: (i, j, 0))
    nseg = ffn_buf.shape[1]
    buf_spec = pl.BlockSpec((None, nseg, FFN_CONV - 1, 2 * D_FF), lambda i, j: (i, 0, 0, 0))
    weight_specs = ([_const_spec(s) for s in ((MLA_HEADS, KV_RANK, V_DIM),
                                              (D_MODEL // W_SLAB, MLA_HEADS * V_DIM, W_SLAB), (1, D_MODEL))]
                    + [_layer_spec(s, 1) for s in FFN_WEIGHT_SPECS])
    return pl.pallas_call(
        _layer1_tail_kernel,
        grid=grid,
        in_specs=[pl.BlockSpec((None, nseg, MLA_HEADS, tm // nseg, KV_RANK), lambda i, j: (i, 0, 0, j, 0)),
                  row(D_MODEL), buf_spec] + weight_specs,
        out_specs=[row(D_MODEL), buf_spec],
        out_shape=[
            jax.ShapeDtypeStruct((b, t, D_MODEL), F32),
            jax.ShapeDtypeStruct((b, nseg, FFN_CONV - 1, 2 * D_FF), F32),
        ],
        scratch_shapes=_ffn_scratch(tm, nseg),
        compiler_params=_params("parallel", "arbitrary"),
        name="layer1_tail",
    )(o_lat, x, ffn_buf, wuv, wo, g_mix, *ffn_w)


def _attn_kernel(qt_ref, *rest, tq, tk, past, nkv, mode):
    cached = mode == "cached"
    if cached:
        clat_ref, crope_ref, knew_ref = rest[:3]
        rest = rest[3:]
    else:
        k_ref, vt_all = rest[:2]
        rest = rest[2:]
    o_all, m_all, l_all, a_all, acc_all, s_all, p_all_ref = rest
    nb = qt_ref.shape[0]
    qi = pl.program_id(1)
    rows = MLA_HEADS * tq
    nblk = rows // LANE
    blocks = [slice(b * LANE, (b + 1) * LANE) for b in range(nblk)]
    m_all[...] = jnp.full(m_all.shape, -jnp.inf, F32)
    l_all[...] = jnp.zeros(l_all.shape, F32)
    acc_all[...] = jnp.zeros(acc_all.shape, F32)
    n_vis = jnp.minimum(nkv, (past + (qi + 1) * tq + tk - 1) // tk)

    def key_value_tile(bi, j, new):
        if not cached:
            ks = k_ref[bi, pl.ds(pl.multiple_of(j * tk, tk), tk), :]
            return ks, lambda: vt_all[bi, j]
        if new:
            ks = jnp.concatenate([knew_ref[bi], jnp.zeros((tk - knew_ref.shape[1], QK_CAT), BF16)], axis=0)
            return ks, lambda: ks[:, :KV_RANK].astype(F32).T.astype(BF16)
        tile = pl.ds(pl.multiple_of(j * tk, tk), tk)
        lat = clat_ref[bi, tile, :]
        ks = jnp.concatenate([lat.astype(BF16), crope_ref[bi, tile, :].astype(BF16)], axis=1)
        return ks, lambda: lat.T.astype(BF16)

    def scores(bi, j, new=False):
        ks, _ = key_value_tile(bi, j, new)
        qt = jnp.concatenate([qt_ref[bi, sl] for sl in range(qt_ref.shape[1])], axis=1)
        return jnp.dot(ks, qt, preferred_element_type=F32)

    def store_scores(bi, slot, s):
        for b, csl in enumerate(blocks):
            s_all[bi, slot, b] = s[:, csl]

    def hidden_bias(j):
        key_chunk = lax.broadcasted_iota(jnp.int32, (tk, LANE), 0) // CHUNK + j * (tk // CHUNK)
        lane = lax.broadcasted_iota(jnp.int32, (tk, LANE), 1)
        pats = []
        for ph in range(max(1, tq // LANE)):
            q_chunk = (past + qi * tq) // CHUNK + ((ph * LANE + lane) % tq) // CHUNK
            pats.append(jnp.where(key_chunk <= q_chunk, 0.0, -jnp.inf))
        return pats

    def absorb(bi, j, masked, new=False):
        m_ref, l_ref, a_ref, acc_ref, p_ref = m_all.at[bi], l_all.at[bi], a_all.at[bi], acc_all.at[bi], p_all_ref.at[bi]
        slot = j % 2
        pats = hidden_bias(j) if masked else None
        for b, csl in enumerate(blocks):
            s = s_all[bi, slot, b]
            if masked:
                s = s + pats[b % len(pats)]
            m_prev = m_ref[:, csl]
            m_new = jnp.maximum(m_prev, jnp.max(s, axis=0, keepdims=True))
            alpha = jnp.exp2(m_prev - m_new)
            p = jnp.exp2(s - m_new)
            l_ref[:, csl] = alpha * l_ref[:, csl] + jnp.sum(p, axis=0, keepdims=True)
            m_ref[:, csl] = m_new
            a_ref[:, csl] = alpha
            p_ref[b] = p.astype(BF16)
        p_all = jnp.concatenate([p_ref[b] for b in range(nblk)], axis=1)
        _, value_tile = key_value_tile(bi, j, new)
        pv = jnp.dot(value_tile(), p_all, preferred_element_type=F32)
        for b, csl in enumerate(blocks):
            acc_ref[b] = a_ref[:, csl] * acc_ref[b] + pv[:, csl]

    for bi in range(nb):
        store_scores(bi, 0, scores(bi, 0))

    def body(j, carry):
        s_new = [scores(bi, j) for bi in range(nb)]
        for bi in range(nb):
            absorb(bi, j - 1, masked=False)
        for bi in range(nb):
            store_scores(bi, j % 2, s_new[bi])
        return carry

    if cached:
        n_cache = past // tk
        lax.fori_loop(1, n_cache, body, 0)
        s_new = [scores(bi, n_cache, new=True) for bi in range(nb)]
        for bi in range(nb):
            absorb(bi, n_cache - 1, masked=False)
        for bi in range(nb):
            store_scores(bi, n_cache % 2, s_new[bi])
        for bi in range(nb):
            absorb(bi, n_cache, masked=True, new=True)
    else:
        lax.fori_loop(1, n_vis, body, 0)
        for bi in range(nb):
            absorb(bi, n_vis - 1, masked=True)
    for bi in range(nb):
        o = jnp.concatenate([acc_all[bi, b] * (1.0 / l_all[bi, :, csl]) for b, csl in enumerate(blocks)], axis=1).T
        o_all[bi] = o.astype(BF16).reshape(MLA_HEADS, tq, KV_RANK)


def _attention(qt, kcat, vt, cache, tq, tk):
    b, ng, qslabs, _, slab_w = qt.shape
    rows = qslabs * slab_w
    t = ng * tq
    past = 0 if cache is None else cache[0].shape[1]
    assert tk % tq == 0 and past % tk == 0 and rows == MLA_HEADS * tq and tq % CHUNK == 0
    nb = max(1, ATT_STEP_ROWS // rows)
    nb = nb if b % nb == 0 else 1
    grid = (b // nb, ng)
    nblk = rows // LANE
    whole = lambda a: pl.BlockSpec((nb,) + a.shape[1:], lambda i, j: (i,) + (0,) * (a.ndim - 1))
    if cache is None:
        nkv = kcat.shape[1] // tk
        assert kcat.shape[1] % tk == 0
        operands = [qt, kcat, vt]
    else:
        nkv = past // tk + 1
        assert ng == 1 and t <= tk
        operands = [qt, cache[0], cache[1], kcat]
    return pl.pallas_call(
        functools.partial(_attn_kernel, tq=tq, tk=tk, past=past, nkv=nkv,
                          mode="tiles" if cache is None else "cached"),
        grid=grid,
        in_specs=[pl.BlockSpec((nb, None, qslabs, QK_CAT, rows // qslabs), lambda i, j: (i, j, 0, 0, 0))]
        + [whole(a) for a in operands[1:]],
        out_specs=pl.BlockSpec((nb, MLA_HEADS, tq, KV_RANK), lambda i, j: (i, 0, j, 0)),
        out_shape=jax.ShapeDtypeStruct((b, MLA_HEADS, t, KV_RANK), BF16),
        scratch_shapes=[
            pltpu.VMEM((nb, 1, rows), F32),
            pltpu.VMEM((nb, 1, rows), F32),
            pltpu.VMEM((nb, 1, rows), F32),
            pltpu.VMEM((nb, nblk, KV_RANK, LANE), F32),
            pltpu.VMEM((nb, 2, nblk, tk, LANE), F32),
            pltpu.VMEM((nb, nblk, tk, LANE), BF16),
        ],
        compiler_params=_params("parallel", "parallel"),
        name="mla_attention",
    )(*operands)


def _rope_tables(past, t):
    half = QK_ROPE // 2
    inv = jnp.exp(-math.log(ROPE_THETA) * jnp.arange(half, dtype=F32) / half)
    ang = (past + jnp.arange(t, dtype=jnp.int32)).astype(F32)[:, None] * inv[None, :]
    reps = 2 * LANE // QK_ROPE
    return jnp.tile(jnp.cos(ang), (1, reps)), jnp.tile(jnp.sin(ang), (1, reps))


def _rotate_half_cols(w):
    shp = w.shape
    w4 = w.reshape(shp[:-1] + (shp[-1] // QK_ROPE, 2, QK_ROPE // 2))
    return jnp.concatenate([-w4[..., 1:2, :], w4[..., 0:1, :]], axis=-2).reshape(shp)


def _tile(t, pref):
    return pref if t % pref == 0 else t


def _prep_weights(norm_mix_pre, norm_mix_post, norm_ffn_pre, norm_ffn_post,
                  ssm_w_in, ssm_conv_w, ssm_conv_b, ssm_dt_bias, ssm_a_log, ssm_d, ssm_norm, ssm_w_out,
                  kv_norm_in, kv_w_dkv, kv_norm, kv_w_kr, kv_w_uk, kv_w_uv,
                  mla_w_dq, mla_q_norm, mla_w_uq, mla_w_o,
                  ffn_w_up, ffn_conv_w, ffn_conv_b, ffn_w_down):
    w = {}
    slabs = lambda a, width: jnp.moveaxis(a.reshape(a.shape[:-1] + (a.shape[-1] // width, width)), -2, -3)
    w_in = ssm_w_in[0]
    w["wzx"] = slabs(w_in[:, :D_INNER + CONV_DIM], COL_CHUNK).astype(BF16)
    w["wdt"] = jnp.pad(w_in[:, D_INNER + CONV_DIM:], ((0, 0), (0, LANE - SSM_HEADS))).astype(BF16)
    w_dt_t = w_in[:, D_INNER + CONV_DIM:].T
    w["wdtt"] = jnp.concatenate([w_dt_t[0::2], w_dt_t[1::2]]).astype(BF16)
    w["ssm_cw"] = ssm_conv_w[0]
    w["ssm_cb"] = ssm_conv_b[0][None, :]
    w["dtb"] = ssm_dt_bias[0][None, :]
    w["dtbt"] = jnp.concatenate([ssm_dt_bias[0][0::2], ssm_dt_bias[0][1::2]])[:, None]
    a = -jnp.exp(ssm_a_log[0].astype(F32))
    w["arow"] = a[None, :]
    w["acol2"] = jnp.broadcast_to(a.reshape(SSM_HEADS // 2, 2, 1),
                                  (SSM_HEADS // 2, 2, CHUNK)).reshape(SSM_HEADS // 2, 2 * CHUNK)
    w["expand"] = jnp.repeat(jnp.eye(SSM_HEADS, dtype=BF16), SSM_HEADDIM, axis=1)
    w["dskip"] = jnp.repeat(ssm_d[0], SSM_HEADDIM)[None, :]
    w["gn"] = ssm_norm[0][None, :]
    w["w_out"] = slabs(ssm_w_out[0], W_SLAB).astype(BF16)
    w["g_mix_pre"] = norm_mix_pre[:, None, :]
    w["g_mix_post"] = norm_mix_post[:, None, :]
    w["g_ffn_pre"] = norm_ffn_pre[:, None, :]
    w["g_ffn_post"] = norm_ffn_post[:, None, :]
    w["ffn_up"] = slabs(ffn_w_up, FFN_CHUNK).astype(BF16)
    w["ffn_cw"] = ffn_conv_w
    w["ffn_cb"] = ffn_conv_b[:, None, :]
    w["ffn_down"] = slabs(ffn_w_down, DOWN_SLAB).astype(BF16)
    w["kv_gin"] = kv_norm_in[None, :]
    w["kv_w"] = slabs(jnp.concatenate([kv_w_dkv, kv_w_kr, _rotate_half_cols(kv_w_kr)], axis=-1), LANE).astype(BF16)
    w["kv_g"] = kv_norm[None, :]
    w["wdq"] = slabs(mla_w_dq[0], LANE).astype(BF16)
    w["gq"] = mla_q_norm[0][None, :]
    wuq = mla_w_uq[0].reshape(Q_RANK, MLA_HEADS, QK_NOPE + QK_ROPE)
    w["wq_nope"] = slabs(wuq[:, :, :QK_NOPE].reshape(Q_RANK, MLA_HEADS * QK_NOPE), W_SLAB).astype(BF16)
    wq_pe = wuq[:, :, QK_NOPE:].reshape(Q_RANK, MLA_HEADS * QK_ROPE)
    w["wq_pe_t"] = wq_pe.T.astype(BF16)
    w["wq_pe_rot_t"] = _rotate_half_cols(wq_pe).T.astype(BF16)
    w["wuk"] = jnp.transpose(kv_w_uk, (1, 0, 2)).astype(BF16)
    w["wuv"] = jnp.transpose(kv_w_uv, (1, 0, 2)).astype(BF16)
    w["wo"] = slabs(mla_w_o[0], W_SLAB).astype(BF16)
    return w


def _trunk(x, ssm_conv_buf, ssm_state, ffn_buf, past_kcat, w):
    b, t, _ = x.shape
    past = 0 if past_kcat is None else past_kcat[0].shape[1]
    assert t % CHUNK == 0 and past % CHUNK == 0
    tm = TOKEN_TILE
    nseq = tm // t if t < tm and tm % t == 0 and b % (tm // t) == 0 else 1
    if t % tm != 0 and nseq == 1:
        tm = t
    bb, tt = b // nseq, nseq * t
    pack = lambda a: a.reshape((bb, tt) + a.shape[2:])
    unpack = lambda a: a.reshape((b, t) + a.shape[2:])
    pack_hist = lambda a: a.reshape((bb, nseq) + a.shape[1:])
    unpack_hist = lambda a: a.reshape((b,) + a.shape[2:])

    z, xbc, dt, dtt2, new_conv = _mamba_in(pack(x), pack_hist(ssm_conv_buf), w["g_mix_pre"][0], w["wzx"], w["wdt"],
                                           w["wdtt"], w["ssm_cw"], w["ssm_cb"], w["dtb"], w["dtbt"], tm)
    nc = _tile(t, SSD_TILE) // CHUNK
    dtt2 = dtt2.reshape(b, t // CHUNK, SSM_HEADS // 2, 2 * CHUNK)
    yg, new_state = _ssd(unpack(xbc), unpack(z), unpack(dt), dtt2, w["arow"], w["acol2"], w["expand"], w["dskip"],
                         w["gn"], None if ssm_state is None else ssm_state.reshape(b, D_INNER, D_STATE), nc)
    new_state = new_state.reshape(b, SSM_HEADS, SSM_HEADDIM, D_STATE)

    ffn_w = lambda i: (w["g_ffn_pre"], w["ffn_up"], w["ffn_cw"], w["ffn_cb"], w["ffn_down"], w["g_ffn_post"])
    cos, sin = _rope_tables(past, t)
    tq = t if nseq > 1 else (ATT_TQ if tm % ATT_TQ == 0 else tm)
    x, new_ffn0, lat, kpe, kcat, q, latt = _layer0_tail(
        pack(yg), pack(x), pack_hist(ffn_buf[0]), jnp.tile(cos, (nseq, 1)), jnp.tile(sin, (nseq, 1)),
        w["w_out"], w["g_mix_post"][0], ffn_w(0), (w["kv_gin"], w["kv_w"], w["kv_g"]),
        (w["g_mix_pre"][1], w["wdq"], w["gq"], w["wq_nope"], w["wq_pe_t"], w["wq_pe_rot_t"], w["wuk"]), tm, tm // tq)
    lat, kpe, kcat = unpack(lat), unpack(kpe), unpack(kcat)
    q = q.reshape((b, t // tq) + q.shape[3:])

    if past_kcat is not None:
        vt = None
    elif tm == ATT_TK:
        vt = latt
    else:
        vt = kcat[:, :, :KV_RANK].reshape(b, t // ATT_TK, ATT_TK, KV_RANK).transpose(0, 1, 3, 2)
    o_lat = _attention(q, kcat, vt, past_kcat, tq, ATT_TK)
    o_lat = o_lat.reshape(bb, nseq, MLA_HEADS, t, KV_RANK)
    x, new_ffn1 = _layer1_tail(o_lat, x, pack_hist(ffn_buf[1]), w["wuv"], w["wo"], w["g_mix_post"][1], ffn_w(1), tm)
    return (unpack(x), new_state[None], unpack_hist(new_conv)[None],
            jnp.stack([unpack_hist(new_ffn0), unpack_hist(new_ffn1)]), lat, kpe)


def kernel(x_prompt, x_sample, state_ssm, state_ssm_conv, state_ffn_conv, cache_kv_latent, cache_k_rope, norm_mix_pre, norm_mix_post, norm_ffn_pre, norm_ffn_post, ssm_w_in, ssm_conv_w, ssm_conv_b, ssm_dt_bias, ssm_a_log, ssm_d, ssm_norm, ssm_w_out, kv_norm_in, kv_w_dkv, kv_norm, kv_w_kr, kv_w_uk, kv_w_uv, mla_w_dq, mla_q_norm, mla_w_uq, mla_w_o, ffn_w_up, ffn_conv_w, ffn_conv_b, ffn_w_down):
    w = _prep_weights(norm_mix_pre, norm_mix_post, norm_ffn_pre, norm_ffn_post,
                      ssm_w_in, ssm_conv_w, ssm_conv_b, ssm_dt_bias, ssm_a_log, ssm_d, ssm_norm, ssm_w_out,
                      kv_norm_in, kv_w_dkv, kv_norm, kv_w_kr, kv_w_uk, kv_w_uv,
                      mla_w_dq, mla_q_norm, mla_w_uq, mla_w_o,
                      ffn_w_up, ffn_conv_w, ffn_conv_b, ffn_w_down)
    bp = x_prompt.shape[0]
    dtp = x_prompt.dtype
    y_p, p_ssm, p_conv, p_ffn, p_lat, p_kpe = _trunk(
        x_prompt,
        jnp.zeros((bp, SSM_CONV - 1, CONV_DIM), dtp),
        None,
        jnp.zeros((2, bp, FFN_CONV - 1, 2 * D_FF), dtp),
        None, w)
    y_s, s_ssm, s_conv, s_ffn, s_lat, s_kpe = _trunk(
        x_sample, state_ssm_conv[0], state_ssm[0], state_ffn_conv, (cache_kv_latent, cache_k_rope), w)
    return (y_p, y_s, p_ssm, p_conv, p_ffn, p_lat, p_kpe, s_ssm, s_conv, s_ffn, s_lat, s_kpe)
```

```python
import functools
import math

import jax
import jax.numpy as jnp
from jax import lax
from jax.experimental import pallas as pl
from jax.experimental.pallas import tpu as pltpu

F32 = jnp.float32
BF16 = jnp.bfloat16

D_MODEL = 1024
CHUNK = 64
EPS = 1e-6
D_INNER = 2048
SSM_HEADDIM = 64
SSM_HEADS = 32
SSM_GROUPS = 4
SSM_HPG = 8
D_STATE = 128
SSM_CONV = 4
GROUP_W = D_INNER // SSM_GROUPS
BC_W = SSM_GROUPS * D_STATE
CONV_DIM = D_INNER + 2 * BC_W
MLA_HEADS = 16
Q_RANK = 384
KV_RANK = 256
QK_NOPE = 128
QK_ROPE = 64
V_DIM = 128
QK_CAT = KV_RANK + QK_ROPE
ROPE_THETA = 10000.0
D_FF = 2816
FFN_CONV = 3
LANE = 128
CARRY_ROWS = 8
TOKEN_TILE = 256
COL_CHUNK = 512
FFN_CHUNK = 256
SSD_TILE = 512
ST_SLAB = 256
SSD_UNROLL = 8
ATT_TK = 256
ATT_TQ = 256
ATT_STEP_ROWS = 4096
LOG2E = 1.4426950408889634
VMEM_LIMIT = 56 * 1024 * 1024

NT_DIMS = (((1,), (1,)), ((), ()))
TN_DIMS = (((0,), (0,)), ((), ()))


def _rms(x, g):
    return x * lax.rsqrt(jnp.mean(x * x, axis=-1, keepdims=True) + EPS) * g


def _silu(x):
    return x * (1.0 / (1.0 + jnp.exp(-x)))


def _softplus(x):
    return jnp.maximum(x, 0.0) + jnp.log(1.0 + jnp.exp(-jnp.abs(x)))


def _const_spec(shape):
    zeros = (0,) * len(shape)
    return pl.BlockSpec(shape, lambda *_: zeros, pipeline_mode=pl.Buffered(1))


def _layer_spec(shape, layer):
    zeros = (0,) * len(shape)
    return pl.BlockSpec((None,) + tuple(shape), lambda *_: (layer,) + zeros, pipeline_mode=pl.Buffered(1))


def _params(*sem):
    return pltpu.CompilerParams(dimension_semantics=sem, vmem_limit_bytes=VMEM_LIMIT)


def _init_carry(carry_ref, hist_ref, first_step):
    nseg, hist, _ = hist_ref.shape

    @pl.when(first_step)
    def _():
        carry_ref[...] = jnp.zeros(carry_ref.shape, F32)
        for s in range(nseg):
            carry_ref[(s + 1) * CARRY_ROWS - hist:(s + 1) * CARRY_ROWS, :] = hist_ref[s]


def _store_history(nbuf_ref, carry_ref):
    nseg, hist, _ = nbuf_ref.shape
    for s in range(nseg):
        nbuf_ref[s] = carry_ref[(s + 1) * CARRY_ROWS - hist:(s + 1) * CARRY_ROWS, :]


def _causal_conv(raw, carry_ref, cols, w_ref, b_ref, taps):
    nseg = carry_ref.shape[0] // CARRY_ROWS
    seg_len = raw.shape[0] // nseg
    hist = taps - 1
    outs = []
    for s in range(nseg):
        cur = raw[s * seg_len:(s + 1) * seg_len, :]
        prev = carry_ref[s * CARRY_ROWS:(s + 1) * CARRY_ROWS, cols]
        ext = jnp.concatenate([prev, cur], axis=0)
        acc = b_ref[:, cols] + cur * w_ref[hist:hist + 1, cols]
        for k in range(1, taps):
            acc = acc + ext[CARRY_ROWS - k:CARRY_ROWS - k + seg_len, :] * w_ref[hist - k:hist - k + 1, cols]
        carry_ref[s * CARRY_ROWS:(s + 1) * CARRY_ROWS, cols] = cur[seg_len - CARRY_ROWS:, :]
        outs.append(acc)
    return outs[0] if nseg == 1 else jnp.concatenate(outs, axis=0)


def _mamba_in_kernel(x_ref, buf_ref, g_ref, wzx_ref, wdt_ref, wdtt_ref, cw_ref, cb_ref, dtb_ref, dtbt_ref,
                     z_ref, xbc_ref, dt_ref, dtt_ref, nbuf_ref, ext_ref):
    _init_carry(ext_ref, buf_ref, pl.program_id(1) == 0)
    h = _rms(x_ref[...], g_ref[...]).astype(BF16)
    for j in range(D_INNER // COL_CHUNK):
        cols = slice(j * COL_CHUNK, (j + 1) * COL_CHUNK)
        z_ref[:, cols] = jnp.dot(h, wzx_ref[:, cols], preferred_element_type=F32).astype(BF16)
    nchunk = CONV_DIM // COL_CHUNK
    proj = lambda j: jnp.dot(h, wzx_ref[:, D_INNER + j * COL_CHUNK:D_INNER + (j + 1) * COL_CHUNK],
                             preferred_element_type=F32)
    raw = proj(0)
    for j in range(nchunk):
        nxt = proj(j + 1) if j + 1 < nchunk else None
        cols = slice(j * COL_CHUNK, (j + 1) * COL_CHUNK)
        xbc_ref[:, cols] = _silu(_causal_conv(raw, ext_ref, cols, cw_ref, cb_ref, SSM_CONV)).astype(BF16)
        raw = nxt
    _store_history(nbuf_ref, ext_ref)
    dt = jnp.dot(h, wdt_ref[...], preferred_element_type=F32)[:, :SSM_HEADS]
    dt_ref[...] = _softplus(dt + dtb_ref[...])
    dtt = _softplus(lax.dot_general(wdtt_ref[...], h, NT_DIMS, preferred_element_type=F32) + dtbt_ref[...])
    for c in range(h.shape[0] // CHUNK):
        for i in range(2):
            dtt_ref[c, :, i * CHUNK:(i + 1) * CHUNK] = dtt[i * (SSM_HEADS // 2):(i + 1) * (SSM_HEADS // 2),
                                                          c * CHUNK:(c + 1) * CHUNK]


def _mamba_in(x, conv_buf, g, wzx, wdt, wdtt, cw, cb, dtb, dtbt, tm):
    b, t, _ = x.shape
    nseg = conv_buf.shape[1]
    grid = (b, t // tm)
    row = lambda w: pl.BlockSpec((None, tm, w), lambda i, j: (i, j, 0))
    return pl.pallas_call(
        _mamba_in_kernel,
        grid=grid,
        in_specs=[
            row(D_MODEL),
            pl.BlockSpec((None, nseg, SSM_CONV - 1, CONV_DIM), lambda i, j: (i, 0, 0, 0)),
            _const_spec((1, D_MODEL)),
            _const_spec((D_MODEL, D_INNER + CONV_DIM + SSM_HEADS)),
            _const_spec((D_MODEL, LANE)),
            _const_spec((SSM_HEADS, D_MODEL)),
            _const_spec((SSM_CONV, CONV_DIM)),
            _const_spec((1, CONV_DIM)),
            _const_spec((1, SSM_HEADS)),
            _const_spec((SSM_HEADS, 1)),
        ],
        out_specs=[
            row(D_INNER),
            row(CONV_DIM),
            row(SSM_HEADS),
            pl.BlockSpec((None, tm // CHUNK, SSM_HEADS // 2, 2 * CHUNK), lambda i, j: (i, j, 0, 0)),
            pl.BlockSpec((None, nseg, SSM_CONV - 1, CONV_DIM), lambda i, j: (i, 0, 0, 0)),
        ],
        out_shape=[
            jax.ShapeDtypeStruct((b, t, D_INNER), BF16),
            jax.ShapeDtypeStruct((b, t, CONV_DIM), BF16),
            jax.ShapeDtypeStruct((b, t, SSM_HEADS), F32),
            jax.ShapeDtypeStruct((b, t // CHUNK, SSM_HEADS // 2, 2 * CHUNK), F32),
            jax.ShapeDtypeStruct((b, nseg, SSM_CONV - 1, CONV_DIM), F32),
        ],
        scratch_shapes=[pltpu.VMEM((nseg * CARRY_ROWS, CONV_DIM), F32)],
        compiler_params=_params("parallel", "arbitrary"),
        name="mamba_in",
    )(x, conv_buf, g, wzx, wdt, wdtt, cw, cb, dtb, dtbt)


def _ssd_kernel(xs_ref, bm_ref, cm_ref, z_ref, dt_ref, dtt_ref, arow_ref, acol_ref, exp_ref, dskip_ref, gn_ref,
                *rest, nc, has_state):
    h0_ref = rest[0] if has_state else None
    yg_ref, hout_ref, st_ref = rest[1:] if has_state else rest
    step = pl.program_id(1)

    @pl.when(step == 0)
    def _():
        if has_state:
            h0t = h0_ref[...].T
            for k in range(st_ref.shape[0]):
                st_ref[k] = h0t[:, k * ST_SLAB:(k + 1) * ST_SLAB]
        else:
            st_ref[...] = jnp.zeros(st_ref.shape, F32)

    li = lax.broadcasted_iota(jnp.int32, (CHUNK, CHUNK), 0)
    si = lax.broadcasted_iota(jnp.int32, (CHUNK, CHUNK), 1)
    tril = (si <= li).astype(F32)
    pi = lax.broadcasted_iota(jnp.int32, (LANE, LANE), 0)
    pj = lax.broadcasted_iota(jnp.int32, (LANE, LANE), 1)
    same_head = (pi // CHUNK) == (pj // CHUNK)
    triu2 = jnp.where(same_head, (pi <= pj).astype(F32), 0.0)
    l2 = lax.broadcasted_iota(jnp.int32, (CHUNK, LANE), 0)
    s2 = lax.broadcasted_iota(jnp.int32, (CHUNK, LANE), 1) % CHUNK
    causal2 = s2 <= l2
    expand = exp_ref[...]

    gsls = [slice(g * GROUP_W, (g + 1) * GROUP_W) for g in range(SSM_GROUPS)]

    def decays(c):
        rows = pl.ds(pl.multiple_of(c * CHUNK, CHUNK), CHUNK)
        dt = dt_ref[rows, :]
        dtt2 = dtt_ref[c]
        acs = jnp.dot(tril, dt * arow_ref[...], precision=lax.Precision.HIGHEST,
                      preferred_element_type=F32)
        acst2 = jnp.dot(dtt2 * acol_ref[...], triu2, precision=lax.Precision.HIGHEST,
                        preferred_element_type=F32)
        last = acs[CHUNK - 1:CHUNK, :]
        wst = (jnp.exp(last - acs) * dt).astype(BF16)
        hi = acs.astype(BF16)
        r1 = acs - hi.astype(F32)
        mid = r1.astype(BF16)
        lo = (r1 - mid.astype(F32)).astype(BF16)
        acs_w = (jnp.dot(hi, expand, preferred_element_type=F32) + jnp.dot(mid, expand, preferred_element_type=F32)
                 + jnp.dot(lo, expand, preferred_element_type=F32))
        e_acs_w = jnp.exp(acs_w)
        wst_w = jnp.dot(wst, expand, preferred_element_type=F32)
        decay_w = jnp.exp(acs_w[CHUNK - 1:CHUNK, :])
        return dtt2, acst2, acs_w, e_acs_w, wst_w, decay_w

    def through_state(c, wst_w, decay_w):
        rows = pl.ds(pl.multiple_of(c * CHUNK, CHUNK), CHUNK)
        cbs, yoffs = [], []
        for g, gsl in enumerate(gsls):
            bg = bm_ref[rows, g * D_STATE:(g + 1) * D_STATE]
            cg = cm_ref[rows, g * D_STATE:(g + 1) * D_STATE]
            cbs.append(lax.dot_general(cg, bg, NT_DIMS, preferred_element_type=F32))
            spg = GROUP_W // ST_SLAB
            stg = jnp.concatenate([st_ref[g * spg + k] for k in range(spg)], axis=1)
            yoffs.append(jnp.dot(cg, stg.astype(BF16), preferred_element_type=F32))
            xw = (xs_ref[rows, gsl].astype(F32) * wst_w[:, gsl]).astype(BF16)
            new = stg * decay_w[:, gsl] + lax.dot_general(bg, xw, TN_DIMS, preferred_element_type=F32)
            for k in range(spg):
                st_ref[g * spg + k] = new[:, k * ST_SLAB:(k + 1) * ST_SLAB]
        return cbs, yoffs

    def within_chunk(c, dtt2, acst2, acs_w, e_acs_w, cbs, yoffs):
        rows = pl.ds(pl.multiple_of(c * CHUNK, CHUNK), CHUNK)
        for g, gsl in enumerate(gsls):
            xg = xs_ref[rows, gsl]
            cb2 = jnp.concatenate([cbs[g], cbs[g]], axis=1)
            ys = []
            for jj in range(SSM_HPG // 2):
                j = g * (SSM_HPG // 2) + jj
                seg = acs_w[:, j * LANE:(j + 1) * LANE] - acst2[j:j + 1, :]
                wts = cb2 * jnp.exp(jnp.where(causal2, seg, -jnp.inf)) * dtt2[j:j + 1, :]
                xp = xg[:, jj * LANE:(jj + 1) * LANE]
                xbd = jnp.where(same_head, jnp.concatenate([xp, xp], axis=0), jnp.zeros((), BF16))
                ys.append(jnp.dot(wts.astype(BF16), xbd, preferred_element_type=F32))
            y = jnp.concatenate(ys, axis=1)
            y = y + yoffs[g] * e_acs_w[:, gsl] + xg.astype(F32) * dskip_ref[:, gsl]
            yg = y * _silu(z_ref[rows, gsl].astype(F32))
            yg_ref[rows, gsl] = _rms(yg, gn_ref[:, gsl]).astype(BF16)

    unroll = min(nc, SSD_UNROLL)

    def body(i, carry):
        cs = [i * unroll + k for k in range(unroll)]
        pre = [decays(c) for c in cs]
        mid = [through_state(c, p[4], p[5]) for c, p in zip(cs, pre)]
        for c, p, m in zip(cs, pre, mid):
            within_chunk(c, p[0], p[1], p[2], p[3], m[0], m[1])
        return carry

    lax.fori_loop(0, nc // unroll, body, 0)

    @pl.when(step == pl.num_programs(1) - 1)
    def _():
        hout_ref[...] = jnp.concatenate([st_ref[k] for k in range(st_ref.shape[0])], axis=1).T


def _ssd(xbc, z, dt, dtt2, arow, acol2, expand, dskip, gn, h0t, nc):
    b, t, _ = xbc.shape
    lt = nc * CHUNK
    grid = (b, t // lt)
    state_spec = pl.BlockSpec((None, D_INNER, D_STATE), lambda i, j: (i, 0, 0))
    state_in = [] if h0t is None else [h0t]
    return pl.pallas_call(
        functools.partial(_ssd_kernel, nc=nc, has_state=h0t is not None),
        grid=grid,
        in_specs=[
            pl.BlockSpec((None, lt, D_INNER), lambda i, j: (i, j, 0)),
            pl.BlockSpec((None, lt, BC_W), lambda i, j: (i, j, D_INNER // BC_W)),
            pl.BlockSpec((None, lt, BC_W), lambda i, j: (i, j, D_INNER // BC_W + 1)),
            pl.BlockSpec((None, lt, D_INNER), lambda i, j: (i, j, 0)),
            pl.BlockSpec((None, lt, SSM_HEADS), lambda i, j: (i, j, 0)),
            pl.BlockSpec((None, nc, SSM_HEADS // 2, 2 * CHUNK), lambda i, j: (i, j, 0, 0)),
            _const_spec((1, SSM_HEADS)),
            _const_spec((SSM_HEADS // 2, 2 * CHUNK)),
            _const_spec((SSM_HEADS, D_INNER)),
            _const_spec((1, D_INNER)),
            _const_spec((1, D_INNER)),
        ] + [state_spec] * len(state_in),
        out_specs=[
            pl.BlockSpec((None, lt, D_INNER), lambda i, j: (i, j, 0)),
            state_spec,
        ],
        out_shape=[
            jax.ShapeDtypeStruct((b, t, D_INNER), BF16),
            jax.ShapeDtypeStruct((b, D_INNER, D_STATE), F32),
        ],
        scratch_shapes=[pltpu.VMEM((D_INNER // ST_SLAB, D_STATE, ST_SLAB), F32)],
        compiler_params=_params("parallel", "arbitrary"),
        name="ssd_scan",
    )(xbc, xbc, xbc, z, dt, dtt2, arow, acol2, expand, dskip, gn, *state_in)


def _mixer_residual(mix_bf16, x, w_ref, g_ref):
    return x + _rms(jnp.dot(mix_bf16, w_ref[...], preferred_element_type=F32), g_ref[...])


def _mla_values(o_ref, wuv_ref):
    nseg = o_ref.shape[0]
    parts = []
    for h in range(MLA_HEADS):
        o_h = o_ref[0, h] if nseg == 1 else jnp.concatenate([o_ref[s, h] for s in range(nseg)], axis=0)
        parts.append(jnp.dot(o_h, wuv_ref[h], preferred_element_type=F32).astype(BF16))
    return jnp.concatenate(parts, axis=-1)


FFN_WEIGHT_SPECS = (
    (1, D_MODEL), (D_MODEL, 2 * D_FF), (FFN_CONV, 2 * D_FF), (1, 2 * D_FF), (D_FF, D_MODEL), (1, D_MODEL))


def _ffn_body(x, buf_ref, gpre_ref, wup_ref, cw_ref, cb_ref, wdn_ref, gpost_ref, nbuf_ref, ext_ref, act_ref):
    _init_carry(ext_ref, buf_ref, pl.program_id(1) == 0)
    h = _rms(x, gpre_ref[...]).astype(BF16)
    nchunk = D_FF // FFN_CHUNK
    vcols = [slice(j * FFN_CHUNK, (j + 1) * FFN_CHUNK) for j in range(nchunk)]
    gcols = [slice(D_FF + j * FFN_CHUNK, D_FF + (j + 1) * FFN_CHUNK) for j in range(nchunk)]
    up = lambda j: (jnp.dot(h, wup_ref[:, vcols[j]], preferred_element_type=F32),
                    jnp.dot(h, wup_ref[:, gcols[j]], preferred_element_type=F32))
    raw = up(0)
    for j in range(nchunk):
        nxt = up(j + 1) if j + 1 < nchunk else None
        val = _causal_conv(raw[0], ext_ref, vcols[j], cw_ref, cb_ref, FFN_CONV)
        gate = _causal_conv(raw[1], ext_ref, gcols[j], cw_ref, cb_ref, FFN_CONV)
        act_ref[:, vcols[j]] = (jax.nn.gelu(gate, approximate=True) * val).astype(BF16)
        raw = nxt
    _store_history(nbuf_ref, ext_ref)
    f = jnp.dot(act_ref[...], wdn_ref[...], preferred_element_type=F32)
    return x + _rms(f, gpost_ref[...])


def _ffn_scratch(tm, nseg):
    return [pltpu.VMEM((nseg * CARRY_ROWS, 2 * D_FF), F32), pltpu.VMEM((tm, D_FF), BF16)]


KV_WEIGHT_SPECS = ((1, D_MODEL), (D_MODEL, KV_RANK + 2 * QK_ROPE), (1, KV_RANK))


def _kv_stages(x, gin_ref, w_ref, gkv_ref, cos_ref, sin_ref, lat_ref, kpe_ref, kcat_ref, latt_ref):
    box = []

    def project():
        h = _rms(x, gin_ref[...]).astype(BF16)
        box.append(jnp.dot(h, w_ref[...], preferred_element_type=F32))

    def finish():
        hw = box[0]
        lat = _rms(hw[:, :KV_RANK], gkv_ref[...])
        kpe = hw[:, KV_RANK:QK_CAT] * cos_ref[:, :QK_ROPE] + hw[:, QK_CAT:] * sin_ref[:, :QK_ROPE]
        lat_ref[...] = lat
        kpe_ref[...] = kpe
        kcat_ref[:, :KV_RANK] = lat.astype(BF16)
        kcat_ref[:, KV_RANK:] = kpe.astype(BF16)
        latt_ref[...] = lat.T.astype(BF16)

    return project, finish


Q_SCALE = (QK_NOPE + QK_ROPE) ** -0.5 * LOG2E
Q_WEIGHT_SPECS = (
    (1, D_MODEL), (D_MODEL, Q_RANK), (1, Q_RANK), (Q_RANK, MLA_HEADS * QK_NOPE), (MLA_HEADS * QK_ROPE, Q_RANK),
    (MLA_HEADS * QK_ROPE, Q_RANK), (MLA_HEADS, KV_RANK, QK_NOPE))


def _q_body(x, gpre_ref, wdq_ref, gq_ref, wn_ref, wpt_ref, wprt_ref, wuk_ref, cost_ref, sint_ref, qt_ref,
            kv_stages):
    tm = x.shape[0]
    nseg = qt_ref.shape[0]
    seg_len = tm // nseg
    scale = Q_SCALE
    kv_project, kv_finish = kv_stages
    h = _rms(x, gpre_ref[...]).astype(BF16)
    cq_raw = jnp.dot(h, wdq_ref[...], preferred_element_type=F32)
    kv_project()
    cq = _rms(cq_raw, gq_ref[...]).astype(BF16)
    qn = jnp.dot(cq, wn_ref[...], preferred_element_type=F32).astype(BF16)
    qpt = lax.dot_general(wpt_ref[...], cq, NT_DIMS, preferred_element_type=F32)
    qprt = lax.dot_general(wprt_ref[...], cq, NT_DIMS, preferred_element_type=F32)
    kv_finish()
    reps = MLA_HEADS * QK_ROPE // LANE
    cost = jnp.concatenate([cost_ref[...]] * reps, axis=0)
    sint = jnp.concatenate([sint_ref[...]] * reps, axis=0)
    qpet = ((qpt * cost + qprt * sint) * scale).astype(BF16)
    for hd in range(MLA_HEADS):
        qlt = lax.dot_general(wuk_ref[hd], qn[:, hd * QK_NOPE:(hd + 1) * QK_NOPE], NT_DIMS,
                              preferred_element_type=F32)
        qlt = (qlt * scale).astype(BF16)
        hps = qt_ref.shape[-1] // seg_len
        for s in range(nseg):
            src = slice(s * seg_len, (s + 1) * seg_len)
            dst = slice((hd % hps) * seg_len, (hd % hps + 1) * seg_len)
            qt_ref[s, hd // hps, :KV_RANK, dst] = qlt[:, src]
            qt_ref[s, hd // hps, KV_RANK:, dst] = qpet[hd * QK_ROPE:(hd + 1) * QK_ROPE, src]


def _layer0_tail_kernel(*refs):
    yg_ref, x_ref, buf_ref, cos_ref, sin_ref, cost_ref, sint_ref, wout_ref, gmix_ref = refs[:9]
    ffn_w = refs[9:15]
    kv_w = refs[15:18]
    q_w = refs[18:25]
    x_out_ref, nbuf_ref, lat_ref, kpe_ref, kcat_ref, qt_ref, latt_ref, ext_ref, act_ref = refs[25:]
    x = _mixer_residual(yg_ref[...], x_ref[...], wout_ref, gmix_ref)
    x = _ffn_body(x, buf_ref, *ffn_w, nbuf_ref, ext_ref, act_ref)
    x_out_ref[...] = x
    kv_stages = _kv_stages(x, *kv_w, cos_ref, sin_ref, lat_ref, kpe_ref, kcat_ref, latt_ref)
    _q_body(x, *q_w, cost_ref, sint_ref, qt_ref, kv_stages)


def _layer0_tail(yg, x, ffn_buf, cos, sin, w_out, g_mix, ffn_w, kv_w, q_w, tm, qseg):
    b, t, _ = x.shape
    qslabs = min(MLA_HEADS, MLA_HEADS * (tm // qseg) // LANE)
    grid = (b, t // tm)
    row = lambda wd: pl.BlockSpec((None, tm, wd), lambda i, j: (i, j, 0))
    nseg = ffn_buf.shape[1]
    buf_spec = pl.BlockSpec((None, nseg, FFN_CONV - 1, 2 * D_FF), lambda i, j: (i, 0, 0, 0))
    tab = pl.BlockSpec((tm, LANE), lambda i, j: (j, 0))
    tab_t = pl.BlockSpec((LANE, tm), lambda i, j: (0, j))
    weight_specs = ([_const_spec(s) for s in ((D_INNER, D_MODEL), (1, D_MODEL))]
                    + [_layer_spec(s, 0) for s in FFN_WEIGHT_SPECS]
                    + [_const_spec(s) for s in KV_WEIGHT_SPECS + Q_WEIGHT_SPECS])
    return pl.pallas_call(
        _layer0_tail_kernel,
        grid=grid,
        in_specs=[row(D_INNER), row(D_MODEL), buf_spec, tab, tab, tab_t, tab_t] + weight_specs,
        out_specs=[
            row(D_MODEL), buf_spec, row(KV_RANK), row(QK_ROPE), row(QK_CAT),
            pl.BlockSpec((None, None, qseg, qslabs, QK_CAT, MLA_HEADS * tm // qseg // qslabs),
                         lambda i, j: (i, j, 0, 0, 0, 0)),
            pl.BlockSpec((None, None, KV_RANK, tm), lambda i, j: (i, j, 0, 0)),
        ],
        out_shape=[
            jax.ShapeDtypeStruct((b, t, D_MODEL), F32),
            jax.ShapeDtypeStruct((b, nseg, FFN_CONV - 1, 2 * D_FF), F32),
            jax.ShapeDtypeStruct((b, t, KV_RANK), F32),
            jax.ShapeDtypeStruct((b, t, QK_ROPE), F32),
            jax.ShapeDtypeStruct((b, t, QK_CAT), BF16),
            jax.ShapeDtypeStruct((b, t // tm, qseg, qslabs, QK_CAT, MLA_HEADS * tm // qseg // qslabs), BF16),
            jax.ShapeDtypeStruct((b, t // tm, KV_RANK, tm), BF16),
        ],
        scratch_shapes=_ffn_scratch(tm, nseg),
        compiler_params=_params("parallel", "arbitrary"),
        name="layer0_tail",
    )(yg, x, ffn_buf, cos, sin, cos.T, sin.T, w_out, g_mix, *ffn_w, *kv_w, *q_w)


def _layer1_tail_kernel(*refs):
    o_ref, x_ref, buf_ref, wuv_ref, wo_ref, gmix_ref = refs[:6]
    ffn_w = refs[6:12]
    x_out_ref, nbuf_ref, ext_ref, act_ref = refs[12:]
    x = _mixer_residual(_mla_values(o_ref, wuv_ref), x_ref[...], wo_ref, gmix_ref)
    x_out_ref[...] = _ffn_body(x, buf_ref, *ffn_w, nbuf_ref, ext_ref, act_ref)


def _layer1_tail(o_lat, x, ffn_buf, wuv, wo, g_mix, ffn_w, tm):
    b, t, _ = x.shape
    grid = (b, t // tm)
    row = lambda wd: pl.BlockSpec((None, tm, wd), lambda i, j: (i, j, 0))
    nseg = ffn_buf.shape[1]
    buf_spec = pl.BlockSpec((None, nseg, FFN_CONV - 1, 2 * D_FF), lambda i, j: (i, 0, 0, 0))
    weight_specs = ([_const_spec(s) for s in ((MLA_HEADS, KV_RANK, V_DIM), (MLA_HEADS * V_DIM, D_MODEL), (1, D_MODEL))]
                    + [_layer_spec(s, 1) for s in FFN_WEIGHT_SPECS])
    return pl.pallas_call(
        _layer1_tail_kernel,
        grid=grid,
        in_specs=[pl.BlockSpec((None, nseg, MLA_HEADS, tm // nseg, KV_RANK), lambda i, j: (i, 0, 0, j, 0)),
                  row(D_MODEL), buf_spec] + weight_specs,
        out_specs=[row(D_MODEL), buf_spec],
        out_shape=[
            jax.ShapeDtypeStruct((b, t, D_MODEL), F32),
            jax.ShapeDtypeStruct((b, nseg, FFN_CONV - 1, 2 * D_FF), F32),
        ],
        scratch_shapes=_ffn_scratch(tm, nseg),
        compiler_params=_params("parallel", "arbitrary"),
        name="layer1_tail",
    )(o_lat, x, ffn_buf, wuv, wo, g_mix, *ffn_w)


def _attn_kernel(qt_ref, *rest, tq, tk, past, nkv, mode):
    cached = mode == "cached"
    if cached:
        clat_ref, crope_ref, knew_ref = rest[:3]
        rest = rest[3:]
    else:
        k_ref, vt_all = rest[:2]
        rest = rest[2:]
    o_all, m_all, l_all, a_all, acc_all, s_all, p_all_ref = rest
    nb = qt_ref.shape[0]
    qi = pl.program_id(1)
    rows = MLA_HEADS * tq
    nblk = rows // LANE
    blocks = [slice(b * LANE, (b + 1) * LANE) for b in range(nblk)]
    m_all[...] = jnp.full(m_all.shape, -jnp.inf, F32)
    l_all[...] = jnp.zeros(l_all.shape, F32)
    acc_all[...] = jnp.zeros(acc_all.shape, F32)
    n_vis = jnp.minimum(nkv, (past + (qi + 1) * tq + tk - 1) // tk)

    def key_value_tile(bi, j, new):
        if not cached:
            ks = k_ref[bi, pl.ds(pl.multiple_of(j * tk, tk), tk), :]
            return ks, lambda: vt_all[bi, j]
        if new:
            ks = jnp.concatenate([knew_ref[bi], jnp.zeros((tk - knew_ref.shape[1], QK_CAT), BF16)], axis=0)
            return ks, lambda: ks[:, :KV_RANK].astype(F32).T.astype(BF16)
        tile = pl.ds(pl.multiple_of(j * tk, tk), tk)
        lat = clat_ref[bi, tile, :]
        ks = jnp.concatenate([lat.astype(BF16), crope_ref[bi, tile, :].astype(BF16)], axis=1)
        return ks, lambda: lat.T.astype(BF16)

    def scores(bi, j, new=False):
        ks, _ = key_value_tile(bi, j, new)
        qt = jnp.concatenate([qt_ref[bi, sl] for sl in range(qt_ref.shape[1])], axis=1)
        return jnp.dot(ks, qt, preferred_element_type=F32)

    def store_scores(bi, slot, s):
        for b, csl in enumerate(blocks):
            s_all[bi, slot, b] = s[:, csl]

    def hidden_bias(j):
        key_chunk = lax.broadcasted_iota(jnp.int32, (tk, LANE), 0) // CHUNK + j * (tk // CHUNK)
        lane = lax.broadcasted_iota(jnp.int32, (tk, LANE), 1)
        pats = []
        for ph in range(max(1, tq // LANE)):
            q_chunk = (past + qi * tq) // CHUNK + ((ph * LANE + lane) % tq) // CHUNK
            pats.append(jnp.where(key_chunk <= q_chunk, 0.0, -jnp.inf))
        return pats

    def absorb(bi, j, masked, new=False):
        m_ref, l_ref, a_ref, acc_ref, p_ref = m_all.at[bi], l_all.at[bi], a_all.at[bi], acc_all.at[bi], p_all_ref.at[bi]
        slot = j % 2
        pats = hidden_bias(j) if masked else None
        for b, csl in enumerate(blocks):
            s = s_all[bi, slot, b]
            if masked:
                s = s + pats[b % len(pats)]
            m_prev = m_ref[:, csl]
            m_new = jnp.maximum(m_prev, jnp.max(s, axis=0, keepdims=True))
            alpha = jnp.exp2(m_prev - m_new)
            p = jnp.exp2(s - m_new)
            l_ref[:, csl] = alpha * l_ref[:, csl] + jnp.sum(p, axis=0, keepdims=True)
            m_ref[:, csl] = m_new
            a_ref[:, csl] = alpha
            p_ref[b] = p.astype(BF16)
        p_all = jnp.concatenate([p_ref[b] for b in range(nblk)], axis=1)
        _, value_tile = key_value_tile(bi, j, new)
        pv = jnp.dot(value_tile(), p_all, preferred_element_type=F32)
        for b, csl in enumerate(blocks):
            acc_ref[b] = a_ref[:, csl] * acc_ref[b] + pv[:, csl]

    for bi in range(nb):
        store_scores(bi, 0, scores(bi, 0))

    def body(j, carry):
        s_new = [scores(bi, j) for bi in range(nb)]
        for bi in range(nb):
            absorb(bi, j - 1, masked=False)
        for bi in range(nb):
            store_scores(bi, j % 2, s_new[bi])
        return carry

    if cached:
        n_cache = past // tk
        lax.fori_loop(1, n_cache, body, 0)
        s_new = [scores(bi, n_cache, new=True) for bi in range(nb)]
        for bi in range(nb):
            absorb(bi, n_cache - 1, masked=False)
        for bi in range(nb):
            store_scores(bi, n_cache % 2, s_new[bi])
        for bi in range(nb):
            absorb(bi, n_cache, masked=True, new=True)
    else:
        lax.fori_loop(1, n_vis, body, 0)
        for bi in range(nb):
            absorb(bi, n_vis - 1, masked=True)
    for bi in range(nb):
        o = jnp.concatenate([acc_all[bi, b] * (1.0 / l_all[bi, :, csl]) for b, csl in enumerate(blocks)], axis=1).T
        o_all[bi] = o.astype(BF16).reshape(MLA_HEADS, tq, KV_RANK)


def _attention(qt, kcat, vt, cache, tq, tk):
    b, ng, qslabs, _, slab_w = qt.shape
    rows = qslabs * slab_w
    t = ng * tq
    past = 0 if cache is None else cache[0].shape[1]
    assert tk % tq == 0 and past % tk == 0 and rows == MLA_HEADS * tq and tq % CHUNK == 0
    nb = max(1, ATT_STEP_ROWS // rows)
    nb = nb if b % nb == 0 else 1
    grid = (b // nb, ng)
    nblk = rows // LANE
    whole = lambda a: pl.BlockSpec((nb,) + a.shape[1:], lambda i, j: (i,) + (0,) * (a.ndim - 1))
    if cache is None:
        nkv = kcat.shape[1] // tk
        assert kcat.shape[1] % tk == 0
        operands = [qt, kcat, vt]
    else:
        nkv = past // tk + 1
        assert ng == 1 and t <= tk
        operands = [qt, cache[0], cache[1], kcat]
    return pl.pallas_call(
        functools.partial(_attn_kernel, tq=tq, tk=tk, past=past, nkv=nkv,
                          mode="tiles" if cache is None else "cached"),
        grid=grid,
        in_specs=[pl.BlockSpec((nb, None, qslabs, QK_CAT, rows // qslabs), lambda i, j: (i, j, 0, 0, 0))]
        + [whole(a) for a in operands[1:]],
        out_specs=pl.BlockSpec((nb, MLA_HEADS, tq, KV_RANK), lambda i, j: (i, 0, j, 0)),
        out_shape=jax.ShapeDtypeStruct((b, MLA_HEADS, t, KV_RANK), BF16),
        scratch_shapes=[
            pltpu.VMEM((nb, 1, rows), F32),
            pltpu.VMEM((nb, 1, rows), F32),
            pltpu.VMEM((nb, 1, rows), F32),
            pltpu.VMEM((nb, nblk, KV_RANK, LANE), F32),
            pltpu.VMEM((nb, 2, nblk, tk, LANE), F32),
            pltpu.VMEM((nb, nblk, tk, LANE), BF16),
        ],
        compiler_params=_params("parallel", "parallel"),
        name="mla_attention",
    )(*operands)


def _rope_tables(past, t):
    half = QK_ROPE // 2
    inv = jnp.exp(-math.log(ROPE_THETA) * jnp.arange(half, dtype=F32) / half)
    ang = (past + jnp.arange(t, dtype=jnp.int32)).astype(F32)[:, None] * inv[None, :]
    reps = 2 * LANE // QK_ROPE
    return jnp.tile(jnp.cos(ang), (1, reps)), jnp.tile(jnp.sin(ang), (1, reps))


def _rotate_half_cols(w):
    shp = w.shape
    w4 = w.reshape(shp[:-1] + (shp[-1] // QK_ROPE, 2, QK_ROPE // 2))
    return jnp.concatenate([-w4[..., 1:2, :], w4[..., 0:1, :]], axis=-2).reshape(shp)


def _tile(t, pref):
    return pref if t % pref == 0 else t


def _prep_weights(norm_mix_pre, norm_mix_post, norm_ffn_pre, norm_ffn_post,
                  ssm_w_in, ssm_conv_w, ssm_conv_b, ssm_dt_bias, ssm_a_log, ssm_d, ssm_norm, ssm_w_out,
                  kv_norm_in, kv_w_dkv, kv_norm, kv_w_kr, kv_w_uk, kv_w_uv,
                  mla_w_dq, mla_q_norm, mla_w_uq, mla_w_o,
                  ffn_w_up, ffn_conv_w, ffn_conv_b, ffn_w_down):
    w = {}
    w_in = ssm_w_in[0]
    w["wzx"] = w_in.astype(BF16)
    w["wdt"] = jnp.pad(w_in[:, D_INNER + CONV_DIM:], ((0, 0), (0, LANE - SSM_HEADS))).astype(BF16)
    w_dt_t = w_in[:, D_INNER + CONV_DIM:].T
    w["wdtt"] = jnp.concatenate([w_dt_t[0::2], w_dt_t[1::2]]).astype(BF16)
    w["ssm_cw"] = ssm_conv_w[0]
    w["ssm_cb"] = ssm_conv_b[0][None, :]
    w["dtb"] = ssm_dt_bias[0][None, :]
    w["dtbt"] = jnp.concatenate([ssm_dt_bias[0][0::2], ssm_dt_bias[0][1::2]])[:, None]
    a = -jnp.exp(ssm_a_log[0].astype(F32))
    w["arow"] = a[None, :]
    w["acol2"] = jnp.broadcast_to(a.reshape(SSM_HEADS // 2, 2, 1),
                                  (SSM_HEADS // 2, 2, CHUNK)).reshape(SSM_HEADS // 2, 2 * CHUNK)
    w["expand"] = jnp.repeat(jnp.eye(SSM_HEADS, dtype=BF16), SSM_HEADDIM, axis=1)
    w["dskip"] = jnp.repeat(ssm_d[0], SSM_HEADDIM)[None, :]
    w["gn"] = ssm_norm[0][None, :]
    w["w_out"] = ssm_w_out[0].astype(BF16)
    w["g_mix_pre"] = norm_mix_pre[:, None, :]
    w["g_mix_post"] = norm_mix_post[:, None, :]
    w["g_ffn_pre"] = norm_ffn_pre[:, None, :]
    w["g_ffn_post"] = norm_ffn_post[:, None, :]
    w["ffn_up"] = ffn_w_up.astype(BF16)
    w["ffn_cw"] = ffn_conv_w
    w["ffn_cb"] = ffn_conv_b[:, None, :]
    w["ffn_down"] = ffn_w_down.astype(BF16)
    w["kv_gin"] = kv_norm_in[None, :]
    w["kv_w"] = jnp.concatenate([kv_w_dkv, kv_w_kr, _rotate_half_cols(kv_w_kr)], axis=-1).astype(BF16)
    w["kv_g"] = kv_norm[None, :]
    w["wdq"] = mla_w_dq[0].astype(BF16)
    w["gq"] = mla_q_norm[0][None, :]
    wuq = mla_w_uq[0].reshape(Q_RANK, MLA_HEADS, QK_NOPE + QK_ROPE)
    w["wq_nope"] = wuq[:, :, :QK_NOPE].reshape(Q_RANK, MLA_HEADS * QK_NOPE).astype(BF16)
    wq_pe = wuq[:, :, QK_NOPE:].reshape(Q_RANK, MLA_HEADS * QK_ROPE)
    w["wq_pe_t"] = wq_pe.T.astype(BF16)
    w["wq_pe_rot_t"] = _rotate_half_cols(wq_pe).T.astype(BF16)
    w["wuk"] = jnp.transpose(kv_w_uk, (1, 0, 2)).astype(BF16)
    w["wuv"] = jnp.transpose(kv_w_uv, (1, 0, 2)).astype(BF16)
    w["wo"] = mla_w_o[0].astype(BF16)
    return w


def _trunk(x, ssm_conv_buf, ssm_state, ffn_buf, past_kcat, w):
    b, t, _ = x.shape
    past = 0 if past_kcat is None else past_kcat[0].shape[1]
    assert t % CHUNK == 0 and past % CHUNK == 0
    tm = TOKEN_TILE
    nseq = tm // t if t < tm and tm % t == 0 and b % (tm // t) == 0 else 1
    if t % tm != 0 and nseq == 1:
        tm = t
    bb, tt = b // nseq, nseq * t
    pack = lambda a: a.reshape((bb, tt) + a.shape[2:])
    unpack = lambda a: a.reshape((b, t) + a.shape[2:])
    pack_hist = lambda a: a.reshape((bb, nseq) + a.shape[1:])
    unpack_hist = lambda a: a.reshape((b,) + a.shape[2:])

    z, xbc, dt, dtt2, new_conv = _mamba_in(pack(x), pack_hist(ssm_conv_buf), w["g_mix_pre"][0], w["wzx"], w["wdt"],
                                           w["wdtt"], w["ssm_cw"], w["ssm_cb"], w["dtb"], w["dtbt"], tm)
    nc = _tile(t, SSD_TILE) // CHUNK
    dtt2 = dtt2.reshape(b, t // CHUNK, SSM_HEADS // 2, 2 * CHUNK)
    yg, new_state = _ssd(unpack(xbc), unpack(z), unpack(dt), dtt2, w["arow"], w["acol2"], w["expand"], w["dskip"],
                         w["gn"], None if ssm_state is None else ssm_state.reshape(b, D_INNER, D_STATE), nc)
    new_state = new_state.reshape(b, SSM_HEADS, SSM_HEADDIM, D_STATE)

    ffn_w = lambda i: (w["g_ffn_pre"], w["ffn_up"], w["ffn_cw"], w["ffn_cb"], w["ffn_down"], w["g_ffn_post"])
    cos, sin = _rope_tables(past, t)
    tq = t if nseq > 1 else (ATT_TQ if tm % ATT_TQ == 0 else tm)
    x, new_ffn0, lat, kpe, kcat, q, latt = _layer0_tail(
        pack(yg), pack(x), pack_hist(ffn_buf[0]), jnp.tile(cos, (nseq, 1)), jnp.tile(sin, (nseq, 1)),
        w["w_out"], w["g_mix_post"][0], ffn_w(0), (w["kv_gin"], w["kv_w"], w["kv_g"]),
        (w["g_mix_pre"][1], w["wdq"], w["gq"], w["wq_nope"], w["wq_pe_t"], w["wq_pe_rot_t"], w["wuk"]), tm, tm // tq)
    lat, kpe, kcat = unpack(lat), unpack(kpe), unpack(kcat)
    q = q.reshape((b, t // tq) + q.shape[3:])

    if past_kcat is not None:
        vt = None
    elif tm == ATT_TK:
        vt = latt
    else:
        vt = kcat[:, :, :KV_RANK].reshape(b, t // ATT_TK, ATT_TK, KV_RANK).transpose(0, 1, 3, 2)
    o_lat = _attention(q, kcat, vt, past_kcat, tq, ATT_TK)
    o_lat = o_lat.reshape(bb, nseq, MLA_HEADS, t, KV_RANK)
    x, new_ffn1 = _layer1_tail(o_lat, x, pack_hist(ffn_buf[1]), w["wuv"], w["wo"], w["g_mix_post"][1], ffn_w(1), tm)
    return (unpack(x), new_state[None], unpack_hist(new_conv)[None],
            jnp.stack([unpack_hist(new_ffn0), unpack_hist(new_ffn1)]), lat, kpe)


def kernel(x_prompt, x_sample, state_ssm, state_ssm_conv, state_ffn_conv, cache_kv_latent, cache_k_rope, norm_mix_pre, norm_mix_post, norm_ffn_pre, norm_ffn_post, ssm_w_in, ssm_conv_w, ssm_conv_b, ssm_dt_bias, ssm_a_log, ssm_d, ssm_norm, ssm_w_out, kv_norm_in, kv_w_dkv, kv_norm, kv_w_kr, kv_w_uk, kv_w_uv, mla_w_dq, mla_q_norm, mla_w_uq, mla_w_o, ffn_w_up, ffn_conv_w, ffn_conv_b, ffn_w_down):
    w = _prep_weights(norm_mix_pre, norm_mix_post, norm_ffn_pre, norm_ffn_post,
                      ssm_w_in, ssm_conv_w, ssm_conv_b, ssm_dt_bias, ssm_a_log, ssm_d, ssm_norm, ssm_w_out,
                      kv_norm_in, kv_w_dkv, kv_norm, kv_w_kr, kv_w_uk, kv_w_uv,
                      mla_w_dq, mla_q_norm, mla_w_uq, mla_w_o,
                      ffn_w_up, ffn_conv_w, ffn_conv_b, ffn_w_down)
    bp = x_prompt.shape[0]
    dtp = x_prompt.dtype
    y_p, p_ssm, p_conv, p_ffn, p_lat, p_kpe = _trunk(
        x_prompt,
        jnp.zeros((bp, SSM_CONV - 1, CONV_DIM), dtp),
        None,
        jnp.zeros((2, bp, FFN_CONV - 1, 2 * D_FF), dtp),
        None, w)
    y_s, s_ssm, s_conv, s_ffn, s_lat, s_kpe = _trunk(
        x_sample, state_ssm_conv[0], state_ssm[0], state_ffn_conv, (cache_kv_latent, cache_k_rope), w)
    return (y_p, y_s, p_ssm, p_conv, p_ffn, p_lat, p_kpe, s_ssm, s_conv, s_ffn, s_lat, s_kpe)
```

```python
import functools
import math

import jax
import jax.numpy as jnp
from jax import lax
from jax.experimental import pallas as pl
from jax.experimental.pallas import tpu as pltpu

F32 = jnp.float32
BF16 = jnp.bfloat16

D_MODEL = 1024
CHUNK = 64
EPS = 1e-6
D_INNER = 2048
SSM_HEADDIM = 64
SSM_HEADS = 32
SSM_GROUPS = 4
SSM_HPG = 8
D_STATE = 128
SSM_CONV = 4
GROUP_W = D_INNER // SSM_GROUPS
BC_W = SSM_GROUPS * D_STATE
CONV_DIM = D_INNER + 2 * BC_W
MLA_HEADS = 16
Q_RANK = 384
KV_RANK = 256
QK_NOPE = 128
QK_ROPE = 64
V_DIM = 128
QK_CAT = KV_RANK + QK_ROPE
ROPE_THETA = 10000.0
D_FF = 2816
FFN_CONV = 3
LANE = 128
CARRY_ROWS = 8
TOKEN_TILE = 256
COL_CHUNK = 512
FFN_CHUNK = 256
SSD_TILE = 512
ST_SLAB = 256
SSD_UNROLL = 8
ATT_TK = 256
ATT_TQ = 256
SUM_ROWS = 16
ATT_STEP_ROWS = 4096
LOG2E = 1.4426950408889634
VMEM_LIMIT = 56 * 1024 * 1024

NT_DIMS = (((1,), (1,)), ((), ()))
TN_DIMS = (((0,), (0,)), ((), ()))


def _rms(x, g):
    return x * lax.rsqrt(jnp.mean(x * x, axis=-1, keepdims=True) + EPS) * g


def _silu(x):
    return x * (1.0 / (1.0 + jnp.exp(-x)))


def _softplus(x):
    return jnp.maximum(x, 0.0) + jnp.log(1.0 + jnp.exp(-jnp.abs(x)))


def _const_spec(shape):
    zeros = (0,) * len(shape)
    return pl.BlockSpec(shape, lambda *_: zeros, pipeline_mode=pl.Buffered(1))


def _layer_spec(shape, layer):
    zeros = (0,) * len(shape)
    return pl.BlockSpec((None,) + tuple(shape), lambda *_: (layer,) + zeros, pipeline_mode=pl.Buffered(1))


def _params(*sem):
    return pltpu.CompilerParams(dimension_semantics=sem, vmem_limit_bytes=VMEM_LIMIT)


def _init_carry(carry_ref, hist_ref, first_step):
    nseg, hist, _ = hist_ref.shape

    @pl.when(first_step)
    def _():
        carry_ref[...] = jnp.zeros(carry_ref.shape, F32)
        for s in range(nseg):
            carry_ref[(s + 1) * CARRY_ROWS - hist:(s + 1) * CARRY_ROWS, :] = hist_ref[s]


def _store_history(nbuf_ref, carry_ref):
    nseg, hist, _ = nbuf_ref.shape
    for s in range(nseg):
        nbuf_ref[s] = carry_ref[(s + 1) * CARRY_ROWS - hist:(s + 1) * CARRY_ROWS, :]


def _causal_conv(raw, carry_ref, cols, w_ref, b_ref, taps):
    nseg = carry_ref.shape[0] // CARRY_ROWS
    seg_len = raw.shape[0] // nseg
    hist = taps - 1
    outs = []
    for s in range(nseg):
        cur = raw[s * seg_len:(s + 1) * seg_len, :]
        prev = carry_ref[s * CARRY_ROWS:(s + 1) * CARRY_ROWS, cols]
        ext = jnp.concatenate([prev, cur], axis=0)
        acc = b_ref[:, cols] + cur * w_ref[hist:hist + 1, cols]
        for k in range(1, taps):
            acc = acc + ext[CARRY_ROWS - k:CARRY_ROWS - k + seg_len, :] * w_ref[hist - k:hist - k + 1, cols]
        carry_ref[s * CARRY_ROWS:(s + 1) * CARRY_ROWS, cols] = cur[seg_len - CARRY_ROWS:, :]
        outs.append(acc)
    return outs[0] if nseg == 1 else jnp.concatenate(outs, axis=0)


def _mamba_in_kernel(x_ref, buf_ref, g_ref, wzx_ref, wdt_ref, wdtt_ref, cw_ref, cb_ref, dtb_ref, dtbt_ref,
                     z_ref, xbc_ref, dt_ref, dtt_ref, nbuf_ref, ext_ref):
    _init_carry(ext_ref, buf_ref, pl.program_id(1) == 0)
    h = _rms(x_ref[...], g_ref[...]).astype(BF16)
    for j in range(D_INNER // COL_CHUNK):
        cols = slice(j * COL_CHUNK, (j + 1) * COL_CHUNK)
        z_ref[:, cols] = jnp.dot(h, wzx_ref[:, cols], preferred_element_type=F32).astype(BF16)
    nchunk = CONV_DIM // COL_CHUNK
    proj = lambda j: jnp.dot(h, wzx_ref[:, D_INNER + j * COL_CHUNK:D_INNER + (j + 1) * COL_CHUNK],
                             preferred_element_type=F32)
    raw = proj(0)
    for j in range(nchunk):
        nxt = proj(j + 1) if j + 1 < nchunk else None
        cols = slice(j * COL_CHUNK, (j + 1) * COL_CHUNK)
        xbc_ref[:, cols] = _silu(_causal_conv(raw, ext_ref, cols, cw_ref, cb_ref, SSM_CONV)).astype(BF16)
        raw = nxt
    _store_history(nbuf_ref, ext_ref)
    dt = jnp.dot(h, wdt_ref[...], preferred_element_type=F32)[:, :SSM_HEADS]
    dt_ref[...] = _softplus(dt + dtb_ref[...])
    dtt = _softplus(lax.dot_general(wdtt_ref[...], h, NT_DIMS, preferred_element_type=F32) + dtbt_ref[...])
    for c in range(h.shape[0] // CHUNK):
        for i in range(2):
            dtt_ref[c, :, i * CHUNK:(i + 1) * CHUNK] = dtt[i * (SSM_HEADS // 2):(i + 1) * (SSM_HEADS // 2),
                                                          c * CHUNK:(c + 1) * CHUNK]


def _mamba_in(x, conv_buf, g, wzx, wdt, wdtt, cw, cb, dtb, dtbt, tm):
    b, t, _ = x.shape
    nseg = conv_buf.shape[1]
    grid = (b, t // tm)
    row = lambda w: pl.BlockSpec((None, tm, w), lambda i, j: (i, j, 0))
    return pl.pallas_call(
        _mamba_in_kernel,
        grid=grid,
        in_specs=[
            row(D_MODEL),
            pl.BlockSpec((None, nseg, SSM_CONV - 1, CONV_DIM), lambda i, j: (i, 0, 0, 0)),
            _const_spec((1, D_MODEL)),
            _const_spec((D_MODEL, D_INNER + CONV_DIM + SSM_HEADS)),
            _const_spec((D_MODEL, LANE)),
            _const_spec((SSM_HEADS, D_MODEL)),
            _const_spec((SSM_CONV, CONV_DIM)),
            _const_spec((1, CONV_DIM)),
            _const_spec((1, SSM_HEADS)),
            _const_spec((SSM_HEADS, 1)),
        ],
        out_specs=[
            row(D_INNER),
            row(CONV_DIM),
            row(SSM_HEADS),
            pl.BlockSpec((None, tm // CHUNK, SSM_HEADS // 2, 2 * CHUNK), lambda i, j: (i, j, 0, 0)),
            pl.BlockSpec((None, nseg, SSM_CONV - 1, CONV_DIM), lambda i, j: (i, 0, 0, 0)),
        ],
        out_shape=[
            jax.ShapeDtypeStruct((b, t, D_INNER), BF16),
            jax.ShapeDtypeStruct((b, t, CONV_DIM), BF16),
            jax.ShapeDtypeStruct((b, t, SSM_HEADS), F32),
            jax.ShapeDtypeStruct((b, t // CHUNK, SSM_HEADS // 2, 2 * CHUNK), F32),
            jax.ShapeDtypeStruct((b, nseg, SSM_CONV - 1, CONV_DIM), F32),
        ],
        scratch_shapes=[pltpu.VMEM((nseg * CARRY_ROWS, CONV_DIM), F32)],
        compiler_params=_params("parallel", "arbitrary"),
        name="mamba_in",
    )(x, conv_buf, g, wzx, wdt, wdtt, cw, cb, dtb, dtbt)


def _ssd_kernel(xs_ref, bm_ref, cm_ref, z_ref, dt_ref, dtt_ref, arow_ref, acol_ref, exp_ref, dskip_ref, gn_ref,
                *rest, nc, has_state):
    h0_ref = rest[0] if has_state else None
    yg_ref, hout_ref, st_ref = rest[1:] if has_state else rest
    step = pl.program_id(1)

    @pl.when(step == 0)
    def _():
        if has_state:
            h0t = h0_ref[...].T
            for k in range(st_ref.shape[0]):
                st_ref[k] = h0t[:, k * ST_SLAB:(k + 1) * ST_SLAB]
        else:
            st_ref[...] = jnp.zeros(st_ref.shape, F32)

    li = lax.broadcasted_iota(jnp.int32, (CHUNK, CHUNK), 0)
    si = lax.broadcasted_iota(jnp.int32, (CHUNK, CHUNK), 1)
    tril = (si <= li).astype(F32)
    pi = lax.broadcasted_iota(jnp.int32, (LANE, LANE), 0)
    pj = lax.broadcasted_iota(jnp.int32, (LANE, LANE), 1)
    same_head = (pi // CHUNK) == (pj // CHUNK)
    triu2 = jnp.where(same_head, (pi <= pj).astype(F32), 0.0)
    l2 = lax.broadcasted_iota(jnp.int32, (CHUNK, LANE), 0)
    s2 = lax.broadcasted_iota(jnp.int32, (CHUNK, LANE), 1) % CHUNK
    causal2 = s2 <= l2
    expand = exp_ref[...]

    gsls = [slice(g * GROUP_W, (g + 1) * GROUP_W) for g in range(SSM_GROUPS)]

    def decays(c):
        rows = pl.ds(pl.multiple_of(c * CHUNK, CHUNK), CHUNK)
        dt = dt_ref[rows, :]
        dtt2 = dtt_ref[c]
        acs = jnp.dot(tril, dt * arow_ref[...], precision=lax.Precision.HIGHEST,
                      preferred_element_type=F32)
        acst2 = jnp.dot(dtt2 * acol_ref[...], triu2, precision=lax.Precision.HIGHEST,
                        preferred_element_type=F32)
        last = acs[CHUNK - 1:CHUNK, :]
        wst = (jnp.exp(last - acs) * dt).astype(BF16)
        hi = acs.astype(BF16)
        r1 = acs - hi.astype(F32)
        mid = r1.astype(BF16)
        lo = (r1 - mid.astype(F32)).astype(BF16)
        acs_w = (jnp.dot(hi, expand, preferred_element_type=F32) + jnp.dot(mid, expand, preferred_element_type=F32)
                 + jnp.dot(lo, expand, preferred_element_type=F32))
        e_acs_w = jnp.exp(acs_w)
        wst_w = jnp.dot(wst, expand, preferred_element_type=F32)
        decay_w = jnp.exp(acs_w[CHUNK - 1:CHUNK, :])
        return dtt2, acst2, acs_w, e_acs_w, wst_w, decay_w

    def through_state(c, wst_w, decay_w):
        rows = pl.ds(pl.multiple_of(c * CHUNK, CHUNK), CHUNK)
        cbs, yoffs = [], []
        for g, gsl in enumerate(gsls):
            bg = bm_ref[rows, g * D_STATE:(g + 1) * D_STATE]
            cg = cm_ref[rows, g * D_STATE:(g + 1) * D_STATE]
            cbs.append(lax.dot_general(cg, bg, NT_DIMS, preferred_element_type=F32))
            spg = GROUP_W // ST_SLAB
            stg = jnp.concatenate([st_ref[g * spg + k] for k in range(spg)], axis=1)
            yoffs.append(jnp.dot(cg, stg.astype(BF16), preferred_element_type=F32))
            xw = (xs_ref[rows, gsl].astype(F32) * wst_w[:, gsl]).astype(BF16)
            new = stg * decay_w[:, gsl] + lax.dot_general(bg, xw, TN_DIMS, preferred_element_type=F32)
            for k in range(spg):
                st_ref[g * spg + k] = new[:, k * ST_SLAB:(k + 1) * ST_SLAB]
        return cbs, yoffs

    def within_chunk(c, dtt2, acst2, acs_w, e_acs_w, cbs, yoffs):
        rows = pl.ds(pl.multiple_of(c * CHUNK, CHUNK), CHUNK)
        for g, gsl in enumerate(gsls):
            xg = xs_ref[rows, gsl]
            cb2 = jnp.concatenate([cbs[g], cbs[g]], axis=1)
            ys = []
            for jj in range(SSM_HPG // 2):
                j = g * (SSM_HPG // 2) + jj
                seg = acs_w[:, j * LANE:(j + 1) * LANE] - acst2[j:j + 1, :]
                wts = cb2 * jnp.exp(jnp.where(causal2, seg, -jnp.inf)) * dtt2[j:j + 1, :]
                xp = xg[:, jj * LANE:(jj + 1) * LANE]
                xbd = jnp.where(same_head, jnp.concatenate([xp, xp], axis=0), jnp.zeros((), BF16))
                ys.append(jnp.dot(wts.astype(BF16), xbd, preferred_element_type=F32))
            y = jnp.concatenate(ys, axis=1)
            y = y + yoffs[g] * e_acs_w[:, gsl] + xg.astype(F32) * dskip_ref[:, gsl]
            yg = y * _silu(z_ref[rows, gsl].astype(F32))
            yg_ref[rows, gsl] = _rms(yg, gn_ref[:, gsl]).astype(BF16)

    unroll = min(nc, SSD_UNROLL)

    def body(i, carry):
        cs = [i * unroll + k for k in range(unroll)]
        pre = [decays(c) for c in cs]
        mid = [through_state(c, p[4], p[5]) for c, p in zip(cs, pre)]
        for c, p, m in zip(cs, pre, mid):
            within_chunk(c, p[0], p[1], p[2], p[3], m[0], m[1])
        return carry

    lax.fori_loop(0, nc // unroll, body, 0)

    @pl.when(step == pl.num_programs(1) - 1)
    def _():
        hout_ref[...] = jnp.concatenate([st_ref[k] for k in range(st_ref.shape[0])], axis=1).T


def _ssd(xbc, z, dt, dtt2, arow, acol2, expand, dskip, gn, h0t, nc):
    b, t, _ = xbc.shape
    lt = nc * CHUNK
    grid = (b, t // lt)
    state_spec = pl.BlockSpec((None, D_INNER, D_STATE), lambda i, j: (i, 0, 0))
    state_in = [] if h0t is None else [h0t]
    return pl.pallas_call(
        functools.partial(_ssd_kernel, nc=nc, has_state=h0t is not None),
        grid=grid,
        in_specs=[
            pl.BlockSpec((None, lt, D_INNER), lambda i, j: (i, j, 0)),
            pl.BlockSpec((None, lt, BC_W), lambda i, j: (i, j, D_INNER // BC_W)),
            pl.BlockSpec((None, lt, BC_W), lambda i, j: (i, j, D_INNER // BC_W + 1)),
            pl.BlockSpec((None, lt, D_INNER), lambda i, j: (i, j, 0)),
            pl.BlockSpec((None, lt, SSM_HEADS), lambda i, j: (i, j, 0)),
            pl.BlockSpec((None, nc, SSM_HEADS // 2, 2 * CHUNK), lambda i, j: (i, j, 0, 0)),
            _const_spec((1, SSM_HEADS)),
            _const_spec((SSM_HEADS // 2, 2 * CHUNK)),
            _const_spec((SSM_HEADS, D_INNER)),
            _const_spec((1, D_INNER)),
            _const_spec((1, D_INNER)),
        ] + [state_spec] * len(state_in),
        out_specs=[
            pl.BlockSpec((None, lt, D_INNER), lambda i, j: (i, j, 0)),
            state_spec,
        ],
        out_shape=[
            jax.ShapeDtypeStruct((b, t, D_INNER), BF16),
            jax.ShapeDtypeStruct((b, D_INNER, D_STATE), F32),
        ],
        scratch_shapes=[pltpu.VMEM((D_INNER // ST_SLAB, D_STATE, ST_SLAB), F32)],
        compiler_params=_params("parallel", "arbitrary"),
        name="ssd_scan",
    )(xbc, xbc, xbc, z, dt, dtt2, arow, acol2, expand, dskip, gn, *state_in)


def _mixer_residual(mix_bf16, x, w_ref, g_ref):
    return x + _rms(jnp.dot(mix_bf16, w_ref[...], preferred_element_type=F32), g_ref[...])


def _mla_values(o_ref, wuv_ref):
    nseg = o_ref.shape[0]
    parts = []
    for h in range(MLA_HEADS):
        o_h = o_ref[0, h] if nseg == 1 else jnp.concatenate([o_ref[s, h] for s in range(nseg)], axis=0)
        parts.append(jnp.dot(o_h, wuv_ref[h], preferred_element_type=F32).astype(BF16))
    return jnp.concatenate(parts, axis=-1)


FFN_WEIGHT_SPECS = (
    (1, D_MODEL), (D_MODEL, 2 * D_FF), (FFN_CONV, 2 * D_FF), (1, 2 * D_FF), (D_FF, D_MODEL), (1, D_MODEL))


def _ffn_body(x, buf_ref, gpre_ref, wup_ref, cw_ref, cb_ref, wdn_ref, gpost_ref, nbuf_ref, ext_ref, act_ref):
    _init_carry(ext_ref, buf_ref, pl.program_id(1) == 0)
    h = _rms(x, gpre_ref[...]).astype(BF16)
    nchunk = D_FF // FFN_CHUNK
    vcols = [slice(j * FFN_CHUNK, (j + 1) * FFN_CHUNK) for j in range(nchunk)]
    gcols = [slice(D_FF + j * FFN_CHUNK, D_FF + (j + 1) * FFN_CHUNK) for j in range(nchunk)]
    up = lambda j: (jnp.dot(h, wup_ref[:, vcols[j]], preferred_element_type=F32),
                    jnp.dot(h, wup_ref[:, gcols[j]], preferred_element_type=F32))
    raw = up(0)
    for j in range(nchunk):
        nxt = up(j + 1) if j + 1 < nchunk else None
        val = _causal_conv(raw[0], ext_ref, vcols[j], cw_ref, cb_ref, FFN_CONV)
        gate = _causal_conv(raw[1], ext_ref, gcols[j], cw_ref, cb_ref, FFN_CONV)
        act_ref[:, vcols[j]] = (jax.nn.gelu(gate, approximate=True) * val).astype(BF16)
        raw = nxt
    _store_history(nbuf_ref, ext_ref)
    f = jnp.dot(act_ref[...], wdn_ref[...], preferred_element_type=F32)
    return x + _rms(f, gpost_ref[...])


def _ffn_scratch(tm, nseg):
    return [pltpu.VMEM((nseg * CARRY_ROWS, 2 * D_FF), F32), pltpu.VMEM((tm, D_FF), BF16)]


KV_WEIGHT_SPECS = ((1, D_MODEL), (D_MODEL, KV_RANK + 2 * QK_ROPE), (1, KV_RANK))


def _kv_stages(x, gin_ref, w_ref, gkv_ref, cos_ref, sin_ref, lat_ref, kpe_ref, kcat_ref, latt_ref):
    box = []

    def project():
        h = _rms(x, gin_ref[...]).astype(BF16)
        box.append(jnp.dot(h, w_ref[...], preferred_element_type=F32))

    def finish():
        hw = box[0]
        lat = _rms(hw[:, :KV_RANK], gkv_ref[...])
        kpe = hw[:, KV_RANK:QK_CAT] * cos_ref[:, :QK_ROPE] + hw[:, QK_CAT:] * sin_ref[:, :QK_ROPE]
        lat_ref[...] = lat
        kpe_ref[...] = kpe
        kcat_ref[:, :KV_RANK] = lat.astype(BF16)
        kcat_ref[:, KV_RANK:] = kpe.astype(BF16)
        latt_ref[...] = lat.T.astype(BF16)

    return project, finish


Q_SCALE = (QK_NOPE + QK_ROPE) ** -0.5 * LOG2E
Q_WEIGHT_SPECS = (
    (1, D_MODEL), (D_MODEL, Q_RANK), (1, Q_RANK), (Q_RANK, MLA_HEADS * QK_NOPE), (MLA_HEADS * QK_ROPE, Q_RANK),
    (MLA_HEADS * QK_ROPE, Q_RANK), (MLA_HEADS, KV_RANK, QK_NOPE))


def _q_body(x, gpre_ref, wdq_ref, gq_ref, wn_ref, wpt_ref, wprt_ref, wuk_ref, cost_ref, sint_ref, qt_ref,
            kv_stages):
    tm = x.shape[0]
    nseg = qt_ref.shape[0]
    seg_len = tm // nseg
    scale = Q_SCALE
    kv_project, kv_finish = kv_stages
    h = _rms(x, gpre_ref[...]).astype(BF16)
    cq_raw = jnp.dot(h, wdq_ref[...], preferred_element_type=F32)
    kv_project()
    cq = _rms(cq_raw, gq_ref[...]).astype(BF16)
    qn = jnp.dot(cq, wn_ref[...], preferred_element_type=F32).astype(BF16)
    qpt = lax.dot_general(wpt_ref[...], cq, NT_DIMS, preferred_element_type=F32)
    qprt = lax.dot_general(wprt_ref[...], cq, NT_DIMS, preferred_element_type=F32)
    kv_finish()
    reps = MLA_HEADS * QK_ROPE // LANE
    cost = jnp.concatenate([cost_ref[...]] * reps, axis=0)
    sint = jnp.concatenate([sint_ref[...]] * reps, axis=0)
    qpet = ((qpt * cost + qprt * sint) * scale).astype(BF16)
    for hd in range(MLA_HEADS):
        qlt = lax.dot_general(wuk_ref[hd], qn[:, hd * QK_NOPE:(hd + 1) * QK_NOPE], NT_DIMS,
                              preferred_element_type=F32)
        qlt = (qlt * scale).astype(BF16)
        hps = qt_ref.shape[-1] // seg_len
        for s in range(nseg):
            src = slice(s * seg_len, (s + 1) * seg_len)
            dst = slice((hd % hps) * seg_len, (hd % hps + 1) * seg_len)
            qt_ref[s, hd // hps, :KV_RANK, dst] = qlt[:, src]
            qt_ref[s, hd // hps, KV_RANK:, dst] = qpet[hd * QK_ROPE:(hd + 1) * QK_ROPE, src]


def _layer0_tail_kernel(*refs):
    yg_ref, x_ref, buf_ref, cos_ref, sin_ref, cost_ref, sint_ref, wout_ref, gmix_ref = refs[:9]
    ffn_w = refs[9:15]
    kv_w = refs[15:18]
    q_w = refs[18:25]
    x_out_ref, nbuf_ref, lat_ref, kpe_ref, kcat_ref, qt_ref, latt_ref, ext_ref, act_ref = refs[25:]
    x = _mixer_residual(yg_ref[...], x_ref[...], wout_ref, gmix_ref)
    x = _ffn_body(x, buf_ref, *ffn_w, nbuf_ref, ext_ref, act_ref)
    x_out_ref[...] = x
    kv_stages = _kv_stages(x, *kv_w, cos_ref, sin_ref, lat_ref, kpe_ref, kcat_ref, latt_ref)
    _q_body(x, *q_w, cost_ref, sint_ref, qt_ref, kv_stages)


def _layer0_tail(yg, x, ffn_buf, cos, sin, w_out, g_mix, ffn_w, kv_w, q_w, tm, qseg):
    b, t, _ = x.shape
    qslabs = min(MLA_HEADS, MLA_HEADS * (tm // qseg) // LANE)
    grid = (b, t // tm)
    row = lambda wd: pl.BlockSpec((None, tm, wd), lambda i, j: (i, j, 0))
    nseg = ffn_buf.shape[1]
    buf_spec = pl.BlockSpec((None, nseg, FFN_CONV - 1, 2 * D_FF), lambda i, j: (i, 0, 0, 0))
    tab = pl.BlockSpec((tm, LANE), lambda i, j: (j, 0))
    tab_t = pl.BlockSpec((LANE, tm), lambda i, j: (0, j))
    weight_specs = ([_const_spec(s) for s in ((D_INNER, D_MODEL), (1, D_MODEL))]
                    + [_layer_spec(s, 0) for s in FFN_WEIGHT_SPECS]
                    + [_const_spec(s) for s in KV_WEIGHT_SPECS + Q_WEIGHT_SPECS])
    return pl.pallas_call(
        _layer0_tail_kernel,
        grid=grid,
        in_specs=[row(D_INNER), row(D_MODEL), buf_spec, tab, tab, tab_t, tab_t] + weight_specs,
        out_specs=[
            row(D_MODEL), buf_spec, row(KV_RANK), row(QK_ROPE), row(QK_CAT),
            pl.BlockSpec((None, None, qseg, qslabs, QK_CAT, MLA_HEADS * tm // qseg // qslabs),
                         lambda i, j: (i, j, 0, 0, 0, 0)),
            pl.BlockSpec((None, None, KV_RANK, tm), lambda i, j: (i, j, 0, 0)),
        ],
        out_shape=[
            jax.ShapeDtypeStruct((b, t, D_MODEL), F32),
            jax.ShapeDtypeStruct((b, nseg, FFN_CONV - 1, 2 * D_FF), F32),
            jax.ShapeDtypeStruct((b, t, KV_RANK), F32),
            jax.ShapeDtypeStruct((b, t, QK_ROPE), F32),
            jax.ShapeDtypeStruct((b, t, QK_CAT), BF16),
            jax.ShapeDtypeStruct((b, t // tm, qseg, qslabs, QK_CAT, MLA_HEADS * tm // qseg // qslabs), BF16),
            jax.ShapeDtypeStruct((b, t // tm, KV_RANK, tm), BF16),
        ],
        scratch_shapes=_ffn_scratch(tm, nseg),
        compiler_params=_params("parallel", "arbitrary"),
        name="layer0_tail",
    )(yg, x, ffn_buf, cos, sin, cos.T, sin.T, w_out, g_mix, *ffn_w, *kv_w, *q_w)


def _layer1_tail_kernel(*refs):
    o_ref, x_ref, buf_ref, wuv_ref, wo_ref, gmix_ref = refs[:6]
    ffn_w = refs[6:12]
    x_out_ref, nbuf_ref, ext_ref, act_ref = refs[12:]
    x = _mixer_residual(_mla_values(o_ref, wuv_ref), x_ref[...], wo_ref, gmix_ref)
    x_out_ref[...] = _ffn_body(x, buf_ref, *ffn_w, nbuf_ref, ext_ref, act_ref)


def _layer1_tail(o_lat, x, ffn_buf, wuv, wo, g_mix, ffn_w, tm):
    b, t, _ = x.shape
    grid = (b, t // tm)
    row = lambda wd: pl.BlockSpec((None, tm, wd), lambda i, j: (i, j, 0))
    nseg = ffn_buf.shape[1]
    buf_spec = pl.BlockSpec((None, nseg, FFN_CONV - 1, 2 * D_FF), lambda i, j: (i, 0, 0, 0))
    weight_specs = ([_const_spec(s) for s in ((MLA_HEADS, KV_RANK, V_DIM), (MLA_HEADS * V_DIM, D_MODEL), (1, D_MODEL))]
                    + [_layer_spec(s, 1) for s in FFN_WEIGHT_SPECS])
    return pl.pallas_call(
        _layer1_tail_kernel,
        grid=grid,
        in_specs=[pl.BlockSpec((None, nseg, MLA_HEADS, tm // nseg, KV_RANK), lambda i, j: (i, 0, 0, j, 0)),
                  row(D_MODEL), buf_spec] + weight_specs,
        out_specs=[row(D_MODEL), buf_spec],
        out_shape=[
            jax.ShapeDtypeStruct((b, t, D_MODEL), F32),
            jax.ShapeDtypeStruct((b, nseg, FFN_CONV - 1, 2 * D_FF), F32),
        ],
        scratch_shapes=_ffn_scratch(tm, nseg),
        compiler_params=_params("parallel", "arbitrary"),
        name="layer1_tail",
    )(o_lat, x, ffn_buf, wuv, wo, g_mix, *ffn_w)


def _attn_kernel(qt_ref, *rest, tq, tk, past, nkv, mode):
    cached = mode == "cached"
    if cached:
        clat_ref, crope_ref, knew_ref = rest[:3]
        rest = rest[3:]
    else:
        k_ref, vt_all = rest[:2]
        rest = rest[2:]
    o_all, m_all, l_all, a_all, acc_all, s_all, p_all_ref = rest
    nb = qt_ref.shape[0]
    qi = pl.program_id(1)
    rows = MLA_HEADS * tq
    nblk = rows // LANE
    blocks = [slice(b * LANE, (b + 1) * LANE) for b in range(nblk)]
    m_all[...] = jnp.full(m_all.shape, -jnp.inf, F32)
    l_all[...] = jnp.zeros(l_all.shape, F32)
    acc_all[...] = jnp.zeros(acc_all.shape, F32)
    n_vis = jnp.minimum(nkv, (past + (qi + 1) * tq + tk - 1) // tk)

    def key_value_tile(bi, j, new):
        if not cached:
            ks = k_ref[bi, pl.ds(pl.multiple_of(j * tk, tk), tk), :]
            return ks, lambda: vt_all[bi, j]
        if new:
            ks = jnp.concatenate([knew_ref[bi], jnp.zeros((tk - knew_ref.shape[1], QK_CAT), BF16)], axis=0)
            return ks, lambda: ks[:, :KV_RANK].astype(F32).T.astype(BF16)
        tile = pl.ds(pl.multiple_of(j * tk, tk), tk)
        lat = clat_ref[bi, tile, :]
        ks = jnp.concatenate([lat.astype(BF16), crope_ref[bi, tile, :].astype(BF16)], axis=1)
        return ks, lambda: lat.T.astype(BF16)

    def scores(bi, j, new=False):
        ks, _ = key_value_tile(bi, j, new)
        qt = jnp.concatenate([qt_ref[bi, sl] for sl in range(qt_ref.shape[1])], axis=1)
        return jnp.dot(ks, qt, preferred_element_type=F32)

    def store_scores(bi, slot, s):
        for b, csl in enumerate(blocks):
            s_all[bi, slot, b] = s[:, csl]

    def hidden_bias(j):
        key_chunk = lax.broadcasted_iota(jnp.int32, (tk, LANE), 0) // CHUNK + j * (tk // CHUNK)
        lane = lax.broadcasted_iota(jnp.int32, (tk, LANE), 1)
        pats = []
        for ph in range(max(1, tq // LANE)):
            q_chunk = (past + qi * tq) // CHUNK + ((ph * LANE + lane) % tq) // CHUNK
            pats.append(jnp.where(key_chunk <= q_chunk, 0.0, -jnp.inf))
        return pats

    def absorb(bi, j, masked, new=False):
        m_ref, l_ref, a_ref, acc_ref, p_ref = m_all.at[bi], l_all.at[bi], a_all.at[bi], acc_all.at[bi], p_all_ref.at[bi]
        slot = j % 2
        pats = hidden_bias(j) if masked else None
        for b, csl in enumerate(blocks):
            s = s_all[bi, slot, b]
            if masked:
                s = s + pats[b % len(pats)]
            m_prev = m_ref[:, csl]
            m_new = jnp.maximum(m_prev, jnp.max(s, axis=0, keepdims=True))
            alpha = jnp.exp2(m_prev - m_new)
            p_ref[b] = jnp.exp2((s - m_new).astype(BF16))
            m_ref[:, csl] = m_new
            a_ref[:, csl] = alpha
        p_all = jnp.concatenate([p_ref[b] for b in range(nblk)], axis=1)
        _, value_tile = key_value_tile(bi, j, new)
        vt1 = jnp.concatenate([value_tile(), jnp.ones((SUM_ROWS, tk), BF16)], axis=0)
        pv = jnp.dot(vt1, p_all, preferred_element_type=F32)
        l_ref[...] = a_ref[...] * l_ref[...] + pv[KV_RANK:KV_RANK + 1, :]
        for b, csl in enumerate(blocks):
            acc_ref[b] = a_ref[:, csl] * acc_ref[b] + pv[:KV_RANK, csl]

    for bi in range(nb):
        store_scores(bi, 0, scores(bi, 0))

    def body(j, carry):
        s_new = [scores(bi, j) for bi in range(nb)]
        for bi in range(nb):
            absorb(bi, j - 1, masked=False)
        for bi in range(nb):
            store_scores(bi, j % 2, s_new[bi])
        return carry

    if cached:
        n_cache = past // tk
        lax.fori_loop(1, n_cache, body, 0)
        s_new = [scores(bi, n_cache, new=True) for bi in range(nb)]
        for bi in range(nb):
            absorb(bi, n_cache - 1, masked=False)
        for bi in range(nb):
            store_scores(bi, n_cache % 2, s_new[bi])
        for bi in range(nb):
            absorb(bi, n_cache, masked=True, new=True)
    else:
        lax.fori_loop(1, n_vis, body, 0)
        for bi in range(nb):
            absorb(bi, n_vis - 1, masked=True)
    for bi in range(nb):
        o = jnp.concatenate([acc_all[bi, b] * (1.0 / l_all[bi, :, csl]) for b, csl in enumerate(blocks)], axis=1).T
        o_all[bi] = o.astype(BF16).reshape(MLA_HEADS, tq, KV_RANK)


def _attention(qt, kcat, vt, cache, tq, tk):
    b, ng, qslabs, _, slab_w = qt.shape
    rows = qslabs * slab_w
    t = ng * tq
    past = 0 if cache is None else cache[0].shape[1]
    assert tk % tq == 0 and past % tk == 0 and rows == MLA_HEADS * tq and tq % CHUNK == 0
    nb = max(1, ATT_STEP_ROWS // rows)
    nb = nb if b % nb == 0 else 1
    grid = (b // nb, ng)
    nblk = rows // LANE
    whole = lambda a: pl.BlockSpec((nb,) + a.shape[1:], lambda i, j: (i,) + (0,) * (a.ndim - 1))
    if cache is None:
        nkv = kcat.shape[1] // tk
        assert kcat.shape[1] % tk == 0
        operands = [qt, kcat, vt]
    else:
        nkv = past // tk + 1
        assert ng == 1 and t <= tk
        operands = [qt, cache[0], cache[1], kcat]
    return pl.pallas_call(
        functools.partial(_attn_kernel, tq=tq, tk=tk, past=past, nkv=nkv,
                          mode="tiles" if cache is None else "cached"),
        grid=grid,
        in_specs=[pl.BlockSpec((nb, None, qslabs, QK_CAT, rows // qslabs), lambda i, j: (i, j, 0, 0, 0))]
        + [whole(a) for a in operands[1:]],
        out_specs=pl.BlockSpec((nb, MLA_HEADS, tq, KV_RANK), lambda i, j: (i, 0, j, 0)),
        out_shape=jax.ShapeDtypeStruct((b, MLA_HEADS, t, KV_RANK), BF16),
        scratch_shapes=[
            pltpu.VMEM((nb, 1, rows), F32),
            pltpu.VMEM((nb, 1, rows), F32),
            pltpu.VMEM((nb, 1, rows), F32),
            pltpu.VMEM((nb, nblk, KV_RANK, LANE), F32),
            pltpu.VMEM((nb, 2, nblk, tk, LANE), F32),
            pltpu.VMEM((nb, nblk, tk, LANE), BF16),
        ],
        compiler_params=_params("parallel", "parallel"),
        name="mla_attention",
    )(*operands)


def _rope_tables(past, t):
    half = QK_ROPE // 2
    inv = jnp.exp(-math.log(ROPE_THETA) * jnp.arange(half, dtype=F32) / half)
    ang = (past + jnp.arange(t, dtype=jnp.int32)).astype(F32)[:, None] * inv[None, :]
    reps = 2 * LANE // QK_ROPE
    return jnp.tile(jnp.cos(ang), (1, reps)), jnp.tile(jnp.sin(ang), (1, reps))


def _rotate_half_cols(w):
    shp = w.shape
    w4 = w.reshape(shp[:-1] + (shp[-1] // QK_ROPE, 2, QK_ROPE // 2))
    return jnp.concatenate([-w4[..., 1:2, :], w4[..., 0:1, :]], axis=-2).reshape(shp)


def _tile(t, pref):
    return pref if t % pref == 0 else t


def _prep_weights(norm_mix_pre, norm_mix_post, norm_ffn_pre, norm_ffn_post,
                  ssm_w_in, ssm_conv_w, ssm_conv_b, ssm_dt_bias, ssm_a_log, ssm_d, ssm_norm, ssm_w_out,
                  kv_norm_in, kv_w_dkv, kv_norm, kv_w_kr, kv_w_uk, kv_w_uv,
                  mla_w_dq, mla_q_norm, mla_w_uq, mla_w_o,
                  ffn_w_up, ffn_conv_w, ffn_conv_b, ffn_w_down):
    w = {}
    w_in = ssm_w_in[0]
    w["wzx"] = w_in.astype(BF16)
    w["wdt"] = jnp.pad(w_in[:, D_INNER + CONV_DIM:], ((0, 0), (0, LANE - SSM_HEADS))).astype(BF16)
    w_dt_t = w_in[:, D_INNER + CONV_DIM:].T
    w["wdtt"] = jnp.concatenate([w_dt_t[0::2], w_dt_t[1::2]]).astype(BF16)
    w["ssm_cw"] = ssm_conv_w[0]
    w["ssm_cb"] = ssm_conv_b[0][None, :]
    w["dtb"] = ssm_dt_bias[0][None, :]
    w["dtbt"] = jnp.concatenate([ssm_dt_bias[0][0::2], ssm_dt_bias[0][1::2]])[:, None]
    a = -jnp.exp(ssm_a_log[0].astype(F32))
    w["arow"] = a[None, :]
    w["acol2"] = jnp.broadcast_to(a.reshape(SSM_HEADS // 2, 2, 1),
                                  (SSM_HEADS // 2, 2, CHUNK)).reshape(SSM_HEADS // 2, 2 * CHUNK)
    w["expand"] = jnp.repeat(jnp.eye(SSM_HEADS, dtype=BF16), SSM_HEADDIM, axis=1)
    w["dskip"] = jnp.repeat(ssm_d[0], SSM_HEADDIM)[None, :]
    w["gn"] = ssm_norm[0][None, :]
    w["w_out"] = ssm_w_out[0].astype(BF16)
    w["g_mix_pre"] = norm_mix_pre[:, None, :]
    w["g_mix_post"] = norm_mix_post[:, None, :]
    w["g_ffn_pre"] = norm_ffn_pre[:, None, :]
    w["g_ffn_post"] = norm_ffn_post[:, None, :]
    w["ffn_up"] = ffn_w_up.astype(BF16)
    w["ffn_cw"] = ffn_conv_w
    w["ffn_cb"] = ffn_conv_b[:, None, :]
    w["ffn_down"] = ffn_w_down.astype(BF16)
    w["kv_gin"] = kv_norm_in[None, :]
    w["kv_w"] = jnp.concatenate([kv_w_dkv, kv_w_kr, _rotate_half_cols(kv_w_kr)], axis=-1).astype(BF16)
    w["kv_g"] = kv_norm[None, :]
    w["wdq"] = mla_w_dq[0].astype(BF16)
    w["gq"] = mla_q_norm[0][None, :]
    wuq = mla_w_uq[0].reshape(Q_RANK, MLA_HEADS, QK_NOPE + QK_ROPE)
    w["wq_nope"] = wuq[:, :, :QK_NOPE].reshape(Q_RANK, MLA_HEADS * QK_NOPE).astype(BF16)
    wq_pe = wuq[:, :, QK_NOPE:].reshape(Q_RANK, MLA_HEADS * QK_ROPE)
    w["wq_pe_t"] = wq_pe.T.astype(BF16)
    w["wq_pe_rot_t"] = _rotate_half_cols(wq_pe).T.astype(BF16)
    w["wuk"] = jnp.transpose(kv_w_uk, (1, 0, 2)).astype(BF16)
    w["wuv"] = jnp.transpose(kv_w_uv, (1, 0, 2)).astype(BF16)
    w["wo"] = mla_w_o[0].astype(BF16)
    return w


def _trunk(x, ssm_conv_buf, ssm_state, ffn_buf, past_kcat, w):
    b, t, _ = x.shape
    past = 0 if past_kcat is None else past_kcat[0].shape[1]
    assert t % CHUNK == 0 and past % CHUNK == 0
    tm = TOKEN_TILE
    nseq = tm // t if t < tm and tm % t == 0 and b % (tm // t) == 0 else 1
    if t % tm != 0 and nseq == 1:
        tm = t
    bb, tt = b // nseq, nseq * t
    pack = lambda a: a.reshape((bb, tt) + a.shape[2:])
    unpack = lambda a: a.reshape((b, t) + a.shape[2:])
    pack_hist = lambda a: a.reshape((bb, nseq) + a.shape[1:])
    unpack_hist = lambda a: a.reshape((b,) + a.shape[2:])

    z, xbc, dt, dtt2, new_conv = _mamba_in(pack(x), pack_hist(ssm_conv_buf), w["g_mix_pre"][0], w["wzx"], w["wdt"],
                                           w["wdtt"], w["ssm_cw"], w["ssm_cb"], w["dtb"], w["dtbt"], tm)
    nc = _tile(t, SSD_TILE) // CHUNK
    dtt2 = dtt2.reshape(b, t // CHUNK, SSM_HEADS // 2, 2 * CHUNK)
    yg, new_state = _ssd(unpack(xbc), unpack(z), unpack(dt), dtt2, w["arow"], w["acol2"], w["expand"], w["dskip"],
                         w["gn"], None if ssm_state is None else ssm_state.reshape(b, D_INNER, D_STATE), nc)
    new_state = new_state.reshape(b, SSM_HEADS, SSM_HEADDIM, D_STATE)

    ffn_w = lambda i: (w["g_ffn_pre"], w["ffn_up"], w["ffn_cw"], w["ffn_cb"], w["ffn_down"], w["g_ffn_post"])
    cos, sin = _rope_tables(past, t)
    tq = t if nseq > 1 else (ATT_TQ if tm % ATT_TQ == 0 else tm)
    x, new_ffn0, lat, kpe, kcat, q, latt = _layer0_tail(
        pack(yg), pack(x), pack_hist(ffn_buf[0]), jnp.tile(cos, (nseq, 1)), jnp.tile(sin, (nseq, 1)),
        w["w_out"], w["g_mix_post"][0], ffn_w(0), (w["kv_gin"], w["kv_w"], w["kv_g"]),
        (w["g_mix_pre"][1], w["wdq"], w["gq"], w["wq_nope"], w["wq_pe_t"], w["wq_pe_rot_t"], w["wuk"]), tm, tm // tq)
    lat, kpe, kcat = unpack(lat), unpack(kpe), unpack(kcat)
    q = q.reshape((b, t // tq) + q.shape[3:])

    if past_kcat is not None:
        vt = None
    elif tm == ATT_TK:
        vt = latt
    else:
        vt = kcat[:, :, :KV_RANK].reshape(b, t // ATT_TK, ATT_TK, KV_RANK).transpose(0, 1, 3, 2)
    o_lat = _attention(q, kcat, vt, past_kcat, tq, ATT_TK)
    o_lat = o_lat.reshape(bb, nseq, MLA_HEADS, t, KV_RANK)
    x, new_ffn1 = _layer1_tail(o_lat, x, pack_hist(ffn_buf[1]), w["wuv"], w["wo"], w["g_mix_post"][1], ffn_w(1), tm)
    return (unpack(x), new_state[None], unpack_hist(new_conv)[None],
            jnp.stack([unpack_hist(new_ffn0), unpack_hist(new_ffn1)]), lat, kpe)


def kernel(x_prompt, x_sample, state_ssm, state_ssm_conv, state_ffn_conv, cache_kv_latent, cache_k_rope, norm_mix_pre, norm_mix_post, norm_ffn_pre, norm_ffn_post, ssm_w_in, ssm_conv_w, ssm_conv_b, ssm_dt_bias, ssm_a_log, ssm_d, ssm_norm, ssm_w_out, kv_norm_in, kv_w_dkv, kv_norm, kv_w_kr, kv_w_uk, kv_w_uv, mla_w_dq, mla_q_norm, mla_w_uq, mla_w_o, ffn_w_up, ffn_conv_w, ffn_conv_b, ffn_w_down):
    w = _prep_weights(norm_mix_pre, norm_mix_post, norm_ffn_pre, norm_ffn_post,
                      ssm_w_in, ssm_conv_w, ssm_conv_b, ssm_dt_bias, ssm_a_log, ssm_d, ssm_norm, ssm_w_out,
                      kv_norm_in, kv_w_dkv, kv_norm, kv_w_kr, kv_w_uk, kv_w_uv,
                      mla_w_dq, mla_q_norm, mla_w_uq, mla_w_o,
                      ffn_w_up, ffn_conv_w, ffn_conv_b, ffn_w_down)
    bp = x_prompt.shape[0]
    dtp = x_prompt.dtype
    y_p, p_ssm, p_conv, p_ffn, p_lat, p_kpe = _trunk(
        x_prompt,
        jnp.zeros((bp, SSM_CONV - 1, CONV_DIM), dtp),
        None,
        jnp.zeros((2, bp, FFN_CONV - 1, 2 * D_FF), dtp),
        None, w)
    y_s, s_ssm, s_conv, s_ffn, s_lat, s_kpe = _trunk(
        x_sample, state_ssm_conv[0], state_ssm[0], state_ffn_conv, (cache_kv_latent, cache_k_rope), w)
    return (y_p, y_s, p_ssm, p_conv, p_ffn, p_lat, p_kpe, s_ssm, s_conv, s_ffn, s_lat, s_kpe)
```

```python
import functools
import math

import jax
import jax.numpy as jnp
from jax import lax
from jax.experimental import pallas as pl
from jax.experimental.pallas import tpu as pltpu

F32 = jnp.float32
BF16 = jnp.bfloat16

D_MODEL = 1024
CHUNK = 64
EPS = 1e-6
D_INNER = 2048
SSM_HEADDIM = 64
SSM_HEADS = 32
SSM_GROUPS = 4
SSM_HPG = 8
D_STATE = 128
SSM_CONV = 4
GROUP_W = D_INNER // SSM_GROUPS
BC_W = SSM_GROUPS * D_STATE
CONV_DIM = D_INNER + 2 * BC_W
MLA_HEADS = 16
Q_RANK = 384
KV_RANK = 256
QK_NOPE = 128
QK_ROPE = 64
V_DIM = 128
QK_CAT = KV_RANK + QK_ROPE
ROPE_THETA = 10000.0
D_FF = 2816
FFN_CONV = 3
LANE = 128
CARRY_ROWS = 8
TOKEN_TILE = 256
COL_CHUNK = 512
FFN_CHUNK = 256
SSD_TILE = 512
ST_SLAB = 256
SSD_UNROLL = 8
ATT_TK = 256
ATT_TQ = 256
SUM_ROWS = 16
ATT_STEP_ROWS = 4096
LOG2E = 1.4426950408889634
VMEM_LIMIT = 56 * 1024 * 1024

NT_DIMS = (((1,), (1,)), ((), ()))
TN_DIMS = (((0,), (0,)), ((), ()))


def _rms(x, g):
    return x * lax.rsqrt(jnp.mean(x * x, axis=-1, keepdims=True) + EPS) * g


def _silu(x):
    return x * (1.0 / (1.0 + jnp.exp(-x)))


def _softplus(x):
    return jnp.maximum(x, 0.0) + jnp.log(1.0 + jnp.exp(-jnp.abs(x)))


def _const_spec(shape):
    zeros = (0,) * len(shape)
    return pl.BlockSpec(shape, lambda *_: zeros, pipeline_mode=pl.Buffered(1))


def _layer_spec(shape, layer):
    zeros = (0,) * len(shape)
    return pl.BlockSpec((None,) + tuple(shape), lambda *_: (layer,) + zeros, pipeline_mode=pl.Buffered(1))


def _params(*sem):
    return pltpu.CompilerParams(dimension_semantics=sem, vmem_limit_bytes=VMEM_LIMIT)


def _init_carry(carry_ref, hist_ref, first_step):
    nseg, hist, _ = hist_ref.shape

    @pl.when(first_step)
    def _():
        carry_ref[...] = jnp.zeros(carry_ref.shape, F32)
        for s in range(nseg):
            carry_ref[(s + 1) * CARRY_ROWS - hist:(s + 1) * CARRY_ROWS, :] = hist_ref[s]


def _store_history(nbuf_ref, carry_ref):
    nseg, hist, _ = nbuf_ref.shape
    for s in range(nseg):
        nbuf_ref[s] = carry_ref[(s + 1) * CARRY_ROWS - hist:(s + 1) * CARRY_ROWS, :]


def _causal_conv(raw, carry_ref, cols, w_ref, b_ref, taps):
    nseg = carry_ref.shape[0] // CARRY_ROWS
    seg_len = raw.shape[0] // nseg
    hist = taps - 1
    outs = []
    for s in range(nseg):
        cur = raw[s * seg_len:(s + 1) * seg_len, :]
        prev = carry_ref[s * CARRY_ROWS:(s + 1) * CARRY_ROWS, cols]
        ext = jnp.concatenate([prev, cur], axis=0)
        acc = b_ref[:, cols] + cur * w_ref[hist:hist + 1, cols]
        for k in range(1, taps):
            acc = acc + ext[CARRY_ROWS - k:CARRY_ROWS - k + seg_len, :] * w_ref[hist - k:hist - k + 1, cols]
        carry_ref[s * CARRY_ROWS:(s + 1) * CARRY_ROWS, cols] = cur[seg_len - CARRY_ROWS:, :]
        outs.append(acc)
    return outs[0] if nseg == 1 else jnp.concatenate(outs, axis=0)


def _mamba_in_kernel(x_ref, buf_ref, g_ref, wzx_ref, wdt_ref, wdtt_ref, cw_ref, cb_ref, dtb_ref, dtbt_ref,
                     z_ref, xbc_ref, dt_ref, dtt_ref, nbuf_ref, ext_ref):
    _init_carry(ext_ref, buf_ref, pl.program_id(1) == 0)
    h = _rms(x_ref[...], g_ref[...]).astype(BF16)
    for j in range(D_INNER // COL_CHUNK):
        cols = slice(j * COL_CHUNK, (j + 1) * COL_CHUNK)
        z_ref[:, cols] = jnp.dot(h, wzx_ref[:, cols], preferred_element_type=F32).astype(BF16)
    nchunk = CONV_DIM // COL_CHUNK
    proj = lambda j: jnp.dot(h, wzx_ref[:, D_INNER + j * COL_CHUNK:D_INNER + (j + 1) * COL_CHUNK],
                             preferred_element_type=F32)
    raw = proj(0)
    for j in range(nchunk):
        nxt = proj(j + 1) if j + 1 < nchunk else None
        cols = slice(j * COL_CHUNK, (j + 1) * COL_CHUNK)
        xbc_ref[:, cols] = _silu(_causal_conv(raw, ext_ref, cols, cw_ref, cb_ref, SSM_CONV)).astype(BF16)
        raw = nxt
    _store_history(nbuf_ref, ext_ref)
    dt = jnp.dot(h, wdt_ref[...], preferred_element_type=F32)[:, :SSM_HEADS]
    dt_ref[...] = _softplus(dt + dtb_ref[...])
    dtt = _softplus(lax.dot_general(wdtt_ref[...], h, NT_DIMS, preferred_element_type=F32) + dtbt_ref[...])
    for c in range(h.shape[0] // CHUNK):
        for i in range(2):
            dtt_ref[c, :, i * CHUNK:(i + 1) * CHUNK] = dtt[i * (SSM_HEADS // 2):(i + 1) * (SSM_HEADS // 2),
                                                          c * CHUNK:(c + 1) * CHUNK]


def _mamba_in(x, conv_buf, g, wzx, wdt, wdtt, cw, cb, dtb, dtbt, tm):
    b, t, _ = x.shape
    nseg = conv_buf.shape[1]
    grid = (b, t // tm)
    row = lambda w: pl.BlockSpec((None, tm, w), lambda i, j: (i, j, 0))
    return pl.pallas_call(
        _mamba_in_kernel,
        grid=grid,
        in_specs=[
            row(D_MODEL),
            pl.BlockSpec((None, nseg, SSM_CONV - 1, CONV_DIM), lambda i, j: (i, 0, 0, 0)),
            _const_spec((1, D_MODEL)),
            _const_spec((D_MODEL, D_INNER + CONV_DIM + SSM_HEADS)),
            _const_spec((D_MODEL, LANE)),
            _const_spec((SSM_HEADS, D_MODEL)),
            _const_spec((SSM_CONV, CONV_DIM)),
            _const_spec((1, CONV_DIM)),
            _const_spec((1, SSM_HEADS)),
            _const_spec((SSM_HEADS, 1)),
        ],
        out_specs=[
            row(D_INNER),
            row(CONV_DIM),
            row(SSM_HEADS),
            pl.BlockSpec((None, tm // CHUNK, SSM_HEADS // 2, 2 * CHUNK), lambda i, j: (i, j, 0, 0)),
            pl.BlockSpec((None, nseg, SSM_CONV - 1, CONV_DIM), lambda i, j: (i, 0, 0, 0)),
        ],
        out_shape=[
            jax.ShapeDtypeStruct((b, t, D_INNER), BF16),
            jax.ShapeDtypeStruct((b, t, CONV_DIM), BF16),
            jax.ShapeDtypeStruct((b, t, SSM_HEADS), F32),
            jax.ShapeDtypeStruct((b, t // CHUNK, SSM_HEADS // 2, 2 * CHUNK), F32),
            jax.ShapeDtypeStruct((b, nseg, SSM_CONV - 1, CONV_DIM), F32),
        ],
        scratch_shapes=[pltpu.VMEM((nseg * CARRY_ROWS, CONV_DIM), F32)],
        compiler_params=_params("parallel", "arbitrary"),
        name="mamba_in",
    )(x, conv_buf, g, wzx, wdt, wdtt, cw, cb, dtb, dtbt)


def _ssd_kernel(xs_ref, bm_ref, cm_ref, z_ref, dt_ref, dtt_ref, arow_ref, acol_ref, exp_ref, dskip_ref, gn_ref,
                *rest, nc, has_state):
    h0_ref = rest[0] if has_state else None
    yg_ref, hout_ref, st_ref = rest[1:] if has_state else rest
    step = pl.program_id(1)

    @pl.when(step == 0)
    def _():
        if has_state:
            h0t = h0_ref[...].T
            for k in range(st_ref.shape[0]):
                st_ref[k] = h0t[:, k * ST_SLAB:(k + 1) * ST_SLAB]
        else:
            st_ref[...] = jnp.zeros(st_ref.shape, F32)

    li = lax.broadcasted_iota(jnp.int32, (CHUNK, CHUNK), 0)
    si = lax.broadcasted_iota(jnp.int32, (CHUNK, CHUNK), 1)
    tril = (si <= li).astype(F32)
    pi = lax.broadcasted_iota(jnp.int32, (LANE, LANE), 0)
    pj = lax.broadcasted_iota(jnp.int32, (LANE, LANE), 1)
    same_head = (pi // CHUNK) == (pj // CHUNK)
    triu2 = jnp.where(same_head, (pi <= pj).astype(F32), 0.0)
    l2 = lax.broadcasted_iota(jnp.int32, (CHUNK, LANE), 0)
    s2 = lax.broadcasted_iota(jnp.int32, (CHUNK, LANE), 1) % CHUNK
    causal2 = s2 <= l2
    expand = exp_ref[...]

    gsls = [slice(g * GROUP_W, (g + 1) * GROUP_W) for g in range(SSM_GROUPS)]

    def decays(c):
        rows = pl.ds(pl.multiple_of(c * CHUNK, CHUNK), CHUNK)
        dt = dt_ref[rows, :]
        dtt2 = dtt_ref[c]
        acs = jnp.dot(tril, dt * arow_ref[...], precision=lax.Precision.HIGHEST,
                      preferred_element_type=F32)
        acst2 = jnp.dot(dtt2 * acol_ref[...], triu2, precision=lax.Precision.HIGHEST,
                        preferred_element_type=F32)
        last = acs[CHUNK - 1:CHUNK, :]
        wst = (jnp.exp(last - acs) * dt).astype(BF16)
        hi = acs.astype(BF16)
        r1 = acs - hi.astype(F32)
        mid = r1.astype(BF16)
        lo = (r1 - mid.astype(F32)).astype(BF16)
        acs_w = (jnp.dot(hi, expand, preferred_element_type=F32) + jnp.dot(mid, expand, preferred_element_type=F32)
                 + jnp.dot(lo, expand, preferred_element_type=F32))
        e_acs_w = jnp.exp(acs_w)
        wst_w = jnp.dot(wst, expand, preferred_element_type=F32)
        decay_w = jnp.exp(acs_w[CHUNK - 1:CHUNK, :])
        return dtt2, acst2, acs_w, e_acs_w, wst_w, decay_w

    def through_state(c, wst_w, decay_w):
        rows = pl.ds(pl.multiple_of(c * CHUNK, CHUNK), CHUNK)
        cbs, yoffs = [], []
        for g, gsl in enumerate(gsls):
            bg = bm_ref[rows, g * D_STATE:(g + 1) * D_STATE]
            cg = cm_ref[rows, g * D_STATE:(g + 1) * D_STATE]
            cbs.append(lax.dot_general(cg, bg, NT_DIMS, preferred_element_type=F32))
            spg = GROUP_W // ST_SLAB
            stg = jnp.concatenate([st_ref[g * spg + k] for k in range(spg)], axis=1)
            yoffs.append(jnp.dot(cg, stg.astype(BF16), preferred_element_type=F32))
            xw = (xs_ref[rows, gsl].astype(F32) * wst_w[:, gsl]).astype(BF16)
            new = stg * decay_w[:, gsl] + lax.dot_general(bg, xw, TN_DIMS, preferred_element_type=F32)
            for k in range(spg):
                st_ref[g * spg + k] = new[:, k * ST_SLAB:(k + 1) * ST_SLAB]
        return cbs, yoffs

    def within_chunk(c, dtt2, acst2, acs_w, e_acs_w, cbs, yoffs):
        rows = pl.ds(pl.multiple_of(c * CHUNK, CHUNK), CHUNK)
        for g, gsl in enumerate(gsls):
            xg = xs_ref[rows, gsl]
            cb2 = jnp.concatenate([cbs[g], cbs[g]], axis=1)
            ys = []
            for jj in range(SSM_HPG // 2):
                j = g * (SSM_HPG // 2) + jj
                seg = acs_w[:, j * LANE:(j + 1) * LANE] - acst2[j:j + 1, :]
                wts = cb2 * jnp.exp(jnp.where(causal2, seg, -jnp.inf)) * dtt2[j:j + 1, :]
                xp = xg[:, jj * LANE:(jj + 1) * LANE]
                xbd = jnp.where(same_head, jnp.concatenate([xp, xp], axis=0), jnp.zeros((), BF16))
                ys.append(jnp.dot(wts.astype(BF16), xbd, preferred_element_type=F32))
            y = jnp.concatenate(ys, axis=1)
            y = y + yoffs[g] * e_acs_w[:, gsl] + xg.astype(F32) * dskip_ref[:, gsl]
            yg = y * _silu(z_ref[rows, gsl].astype(F32))
            yg_ref[rows, gsl] = _rms(yg, gn_ref[:, gsl]).astype(BF16)

    unroll = min(nc, SSD_UNROLL)

    def body(i, carry):
        cs = [i * unroll + k for k in range(unroll)]
        pre = [decays(c) for c in cs]
        mid = [through_state(c, p[4], p[5]) for c, p in zip(cs, pre)]
        for c, p, m in zip(cs, pre, mid):
            within_chunk(c, p[0], p[1], p[2], p[3], m[0], m[1])
        return carry

    lax.fori_loop(0, nc // unroll, body, 0)

    @pl.when(step == pl.num_programs(1) - 1)
    def _():
        hout_ref[...] = jnp.concatenate([st_ref[k] for k in range(st_ref.shape[0])], axis=1).T


def _ssd(xbc, z, dt, dtt2, arow, acol2, expand, dskip, gn, h0t, nc):
    b, t, _ = xbc.shape
    lt = nc * CHUNK
    grid = (b, t // lt)
    state_spec = pl.BlockSpec((None, D_INNER, D_STATE), lambda i, j: (i, 0, 0))
    state_in = [] if h0t is None else [h0t]
    return pl.pallas_call(
        functools.partial(_ssd_kernel, nc=nc, has_state=h0t is not None),
        grid=grid,
        in_specs=[
            pl.BlockSpec((None, lt, D_INNER), lambda i, j: (i, j, 0)),
            pl.BlockSpec((None, lt, BC_W), lambda i, j: (i, j, D_INNER // BC_W)),
            pl.BlockSpec((None, lt, BC_W), lambda i, j: (i, j, D_INNER // BC_W + 1)),
            pl.BlockSpec((None, lt, D_INNER), lambda i, j: (i, j, 0)),
            pl.BlockSpec((None, lt, SSM_HEADS), lambda i, j: (i, j, 0)),
            pl.BlockSpec((None, nc, SSM_HEADS // 2, 2 * CHUNK), lambda i, j: (i, j, 0, 0)),
            _const_spec((1, SSM_HEADS)),
            _const_spec((SSM_HEADS // 2, 2 * CHUNK)),
            _const_spec((SSM_HEADS, D_INNER)),
            _const_spec((1, D_INNER)),
            _const_spec((1, D_INNER)),
        ] + [state_spec] * len(state_in),
        out_specs=[
            pl.BlockSpec((None, lt, D_INNER), lambda i, j: (i, j, 0)),
            state_spec,
        ],
        out_shape=[
            jax.ShapeDtypeStruct((b, t, D_INNER), BF16),
            jax.ShapeDtypeStruct((b, D_INNER, D_STATE), F32),
        ],
        scratch_shapes=[pltpu.VMEM((D_INNER // ST_SLAB, D_STATE, ST_SLAB), F32)],
        compiler_params=_params("parallel", "arbitrary"),
        name="ssd_scan",
    )(xbc, xbc, xbc, z, dt, dtt2, arow, acol2, expand, dskip, gn, *state_in)


def _mixer_residual(mix_bf16, x, w_ref, g_ref):
    return x + _rms(jnp.dot(mix_bf16, w_ref[...], preferred_element_type=F32), g_ref[...])


def _mla_values(o_ref, wuv_ref):
    nseg = o_ref.shape[0]
    parts = []
    for h in range(MLA_HEADS):
        o_h = o_ref[0, h] if nseg == 1 else jnp.concatenate([o_ref[s, h] for s in range(nseg)], axis=0)
        parts.append(jnp.dot(o_h, wuv_ref[h], preferred_element_type=F32).astype(BF16))
    return jnp.concatenate(parts, axis=-1)


FFN_WEIGHT_SPECS = (
    (1, D_MODEL), (D_MODEL, 2 * D_FF), (FFN_CONV, 2 * D_FF), (1, 2 * D_FF), (D_FF, D_MODEL), (1, D_MODEL))


def _ffn_body(x, buf_ref, gpre_ref, wup_ref, cw_ref, cb_ref, wdn_ref, gpost_ref, nbuf_ref, ext_ref, act_ref):
    _init_carry(ext_ref, buf_ref, pl.program_id(1) == 0)
    h = _rms(x, gpre_ref[...]).astype(BF16)
    nchunk = D_FF // FFN_CHUNK
    vcols = [slice(j * FFN_CHUNK, (j + 1) * FFN_CHUNK) for j in range(nchunk)]
    gcols = [slice(D_FF + j * FFN_CHUNK, D_FF + (j + 1) * FFN_CHUNK) for j in range(nchunk)]
    up = lambda j: (jnp.dot(h, wup_ref[:, vcols[j]], preferred_element_type=F32),
                    jnp.dot(h, wup_ref[:, gcols[j]], preferred_element_type=F32))
    raw = up(0)
    for j in range(nchunk):
        nxt = up(j + 1) if j + 1 < nchunk else None
        val = _causal_conv(raw[0], ext_ref, vcols[j], cw_ref, cb_ref, FFN_CONV)
        gate = _causal_conv(raw[1], ext_ref, gcols[j], cw_ref, cb_ref, FFN_CONV)
        act_ref[:, vcols[j]] = (jax.nn.gelu(gate, approximate=True) * val).astype(BF16)
        raw = nxt
    _store_history(nbuf_ref, ext_ref)
    f = jnp.dot(act_ref[...], wdn_ref[...], preferred_element_type=F32)
    return x + _rms(f, gpost_ref[...])


def _ffn_scratch(tm, nseg):
    return [pltpu.VMEM((nseg * CARRY_ROWS, 2 * D_FF), F32), pltpu.VMEM((tm, D_FF), BF16)]


KV_WEIGHT_SPECS = ((1, D_MODEL), (D_MODEL, KV_RANK + 2 * QK_ROPE), (1, KV_RANK))


def _kv_stages(x, gin_ref, w_ref, gkv_ref, cos_ref, sin_ref, lat_ref, kpe_ref, kcat_ref, latt_ref):
    box = []

    def project():
        h = _rms(x, gin_ref[...]).astype(BF16)
        box.append(jnp.dot(h, w_ref[...], preferred_element_type=F32))

    def finish():
        hw = box[0]
        lat = _rms(hw[:, :KV_RANK], gkv_ref[...])
        kpe = hw[:, KV_RANK:QK_CAT] * cos_ref[:, :QK_ROPE] + hw[:, QK_CAT:] * sin_ref[:, :QK_ROPE]
        lat_ref[...] = lat
        kpe_ref[...] = kpe
        kcat_ref[:, :KV_RANK] = lat.astype(BF16)
        kcat_ref[:, KV_RANK:] = kpe.astype(BF16)
        latt_ref[...] = lat.T.astype(BF16)

    return project, finish


Q_SCALE = (QK_NOPE + QK_ROPE) ** -0.5 * LOG2E
Q_WEIGHT_SPECS = (
    (1, D_MODEL), (D_MODEL, Q_RANK), (1, Q_RANK), (Q_RANK, MLA_HEADS * QK_NOPE), (MLA_HEADS * QK_ROPE, Q_RANK),
    (MLA_HEADS * QK_ROPE, Q_RANK), (MLA_HEADS, KV_RANK, QK_NOPE))


def _q_body(x, gpre_ref, wdq_ref, gq_ref, wn_ref, wpt_ref, wprt_ref, wuk_ref, cost_ref, sint_ref, qt_ref,
            kv_stages):
    tm = x.shape[0]
    nseg = qt_ref.shape[0]
    seg_len = tm // nseg
    scale = Q_SCALE
    kv_project, kv_finish = kv_stages
    h = _rms(x, gpre_ref[...]).astype(BF16)
    cq_raw = jnp.dot(h, wdq_ref[...], preferred_element_type=F32)
    kv_project()
    cq = _rms(cq_raw, gq_ref[...]).astype(BF16)
    qn = jnp.dot(cq, wn_ref[...], preferred_element_type=F32).astype(BF16)
    qpt = lax.dot_general(wpt_ref[...], cq, NT_DIMS, preferred_element_type=F32)
    qprt = lax.dot_general(wprt_ref[...], cq, NT_DIMS, preferred_element_type=F32)
    kv_finish()
    reps = MLA_HEADS * QK_ROPE // LANE
    cost = jnp.concatenate([cost_ref[...]] * reps, axis=0)
    sint = jnp.concatenate([sint_ref[...]] * reps, axis=0)
    qpet = ((qpt * cost + qprt * sint) * scale).astype(BF16)
    for hd in range(MLA_HEADS):
        qlt = lax.dot_general(wuk_ref[hd], qn[:, hd * QK_NOPE:(hd + 1) * QK_NOPE], NT_DIMS,
                              preferred_element_type=F32)
        qlt = (qlt * scale).astype(BF16)
        hps = qt_ref.shape[-1] // seg_len
        for s in range(nseg):
            src = slice(s * seg_len, (s + 1) * seg_len)
            dst = slice((hd % hps) * seg_len, (hd % hps + 1) * seg_len)
            qt_ref[s, hd // hps, :KV_RANK, dst] = qlt[:, src]
            qt_ref[s, hd // hps, KV_RANK:, dst] = qpet[hd * QK_ROPE:(hd + 1) * QK_ROPE, src]


def _layer0_tail_kernel(*refs):
    yg_ref, x_ref, buf_ref, cos_ref, sin_ref, cost_ref, sint_ref, wout_ref, gmix_ref = refs[:9]
    ffn_w = refs[9:15]
    kv_w = refs[15:18]
    q_w = refs[18:25]
    x_out_ref, nbuf_ref, lat_ref, kpe_ref, kcat_ref, qt_ref, latt_ref, ext_ref, act_ref = refs[25:]
    x = _mixer_residual(yg_ref[...], x_ref[...], wout_ref, gmix_ref)
    x = _ffn_body(x, buf_ref, *ffn_w, nbuf_ref, ext_ref, act_ref)
    x_out_ref[...] = x
    kv_stages = _kv_stages(x, *kv_w, cos_ref, sin_ref, lat_ref, kpe_ref, kcat_ref, latt_ref)
    _q_body(x, *q_w, cost_ref, sint_ref, qt_ref, kv_stages)


def _layer0_tail(yg, x, ffn_buf, cos, sin, w_out, g_mix, ffn_w, kv_w, q_w, tm, qseg):
    b, t, _ = x.shape
    qslabs = min(MLA_HEADS, MLA_HEADS * (tm // qseg) // LANE)
    grid = (b, t // tm)
    row = lambda wd: pl.BlockSpec((None, tm, wd), lambda i, j: (i, j, 0))
    nseg = ffn_buf.shape[1]
    buf_spec = pl.BlockSpec((None, nseg, FFN_CONV - 1, 2 * D_FF), lambda i, j: (i, 0, 0, 0))
    tab = pl.BlockSpec((tm, LANE), lambda i, j: (j, 0))
    tab_t = pl.BlockSpec((LANE, tm), lambda i, j: (0, j))
    weight_specs = ([_const_spec(s) for s in ((D_INNER, D_MODEL), (1, D_MODEL))]
                    + [_layer_spec(s, 0) for s in FFN_WEIGHT_SPECS]
                    + [_const_spec(s) for s in KV_WEIGHT_SPECS + Q_WEIGHT_SPECS])
    return pl.pallas_call(
        _layer0_tail_kernel,
        grid=grid,
        in_specs=[row(D_INNER), row(D_MODEL), buf_spec, tab, tab, tab_t, tab_t] + weight_specs,
        out_specs=[
            row(D_MODEL), buf_spec, row(KV_RANK), row(QK_ROPE), row(QK_CAT),
            pl.BlockSpec((None, None, qseg, qslabs, QK_CAT, MLA_HEADS * tm // qseg // qslabs),
                         lambda i, j: (i, j, 0, 0, 0, 0)),
            pl.BlockSpec((None, None, KV_RANK, tm), lambda i, j: (i, j, 0, 0)),
        ],
        out_shape=[
            jax.ShapeDtypeStruct((b, t, D_MODEL), F32),
            jax.ShapeDtypeStruct((b, nseg, FFN_CONV - 1, 2 * D_FF), F32),
            jax.ShapeDtypeStruct((b, t, KV_RANK), F32),
            jax.ShapeDtypeStruct((b, t, QK_ROPE), F32),
            jax.ShapeDtypeStruct((b, t, QK_CAT), BF16),
            jax.ShapeDtypeStruct((b, t // tm, qseg, qslabs, QK_CAT, MLA_HEADS * tm // qseg // qslabs), BF16),
            jax.ShapeDtypeStruct((b, t // tm, KV_RANK, tm), BF16),
        ],
        scratch_shapes=_ffn_scratch(tm, nseg),
        compiler_params=_params("parallel", "arbitrary"),
        name="layer0_tail",
    )(yg, x, ffn_buf, cos, sin, cos.T, sin.T, w_out, g_mix, *ffn_w, *kv_w, *q_w)


def _layer1_tail_kernel(*refs):
    o_ref, x_ref, buf_ref, wuv_ref, wo_ref, gmix_ref = refs[:6]
    ffn_w = refs[6:12]
    x_out_ref, nbuf_ref, ext_ref, act_ref = refs[12:]
    x = _mixer_residual(_mla_values(o_ref, wuv_ref), x_ref[...], wo_ref, gmix_ref)
    x_out_ref[...] = _ffn_body(x, buf_ref, *ffn_w, nbuf_ref, ext_ref, act_ref)


def _layer1_tail(o_lat, x, ffn_buf, wuv, wo, g_mix, ffn_w, tm):
    b, t, _ = x.shape
    grid = (b, t // tm)
    row = lambda wd: pl.BlockSpec((None, tm, wd), lambda i, j: (i, j, 0))
    nseg = ffn_buf.shape[1]
    buf_spec = pl.BlockSpec((None, nseg, FFN_CONV - 1, 2 * D_FF), lambda i, j: (i, 0, 0, 0))
    weight_specs = ([_const_spec(s) for s in ((MLA_HEADS, KV_RANK, V_DIM), (MLA_HEADS * V_DIM, D_MODEL), (1, D_MODEL))]
                    + [_layer_spec(s, 1) for s in FFN_WEIGHT_SPECS])
    return pl.pallas_call(
        _layer1_tail_kernel,
        grid=grid,
        in_specs=[pl.BlockSpec((None, nseg, MLA_HEADS, tm // nseg, KV_RANK), lambda i, j: (i, 0, 0, j, 0)),
                  row(D_MODEL), buf_spec] + weight_specs,
        out_specs=[row(D_MODEL), buf_spec],
        out_shape=[
            jax.ShapeDtypeStruct((b, t, D_MODEL), F32),
            jax.ShapeDtypeStruct((b, nseg, FFN_CONV - 1, 2 * D_FF), F32),
        ],
        scratch_shapes=_ffn_scratch(tm, nseg),
        compiler_params=_params("parallel", "arbitrary"),
        name="layer1_tail",
    )(o_lat, x, ffn_buf, wuv, wo, g_mix, *ffn_w)


def _attn_kernel(qt_ref, *rest, tq, tk, past, nkv, mode):
    cached = mode == "cached"
    if cached:
        clat_ref, crope_ref, knew_ref = rest[:3]
        rest = rest[3:]
    else:
        k_ref, vt_all = rest[:2]
        rest = rest[2:]
    o_all, m_all, l_all, a_all, acc_all, s_all, p_all_ref = rest
    nb = qt_ref.shape[0]
    qi = pl.program_id(1)
    rows = MLA_HEADS * tq
    nblk = rows // LANE
    blocks = [slice(b * LANE, (b + 1) * LANE) for b in range(nblk)]
    m_all[...] = jnp.full(m_all.shape, -jnp.inf, F32)
    l_all[...] = jnp.zeros(l_all.shape, F32)
    acc_all[...] = jnp.zeros(acc_all.shape, F32)
    n_vis = jnp.minimum(nkv, (past + (qi + 1) * tq + tk - 1) // tk)

    def key_value_tile(bi, j, new):
        if not cached:
            ks = k_ref[bi, pl.ds(pl.multiple_of(j * tk, tk), tk), :]
            return ks, lambda: vt_all[bi, j]
        if new:
            ks = jnp.concatenate([knew_ref[bi], jnp.zeros((tk - knew_ref.shape[1], QK_CAT), BF16)], axis=0)
            return ks, lambda: ks[:, :KV_RANK].astype(F32).T.astype(BF16)
        tile = pl.ds(pl.multiple_of(j * tk, tk), tk)
        lat = clat_ref[bi, tile, :]
        rope = crope_ref[bi, :, tile].T
        ks = jnp.concatenate([lat.astype(BF16), rope.astype(BF16)], axis=1)
        return ks, lambda: lat.T.astype(BF16)

    def scores(bi, j, new=False):
        ks, _ = key_value_tile(bi, j, new)
        qt = jnp.concatenate([qt_ref[bi, sl] for sl in range(qt_ref.shape[1])], axis=1)
        return jnp.dot(ks, qt, preferred_element_type=F32)

    def store_scores(bi, slot, s):
        for b, csl in enumerate(blocks):
            s_all[bi, slot, b] = s[:, csl]

    def hidden_bias(j):
        key_chunk = lax.broadcasted_iota(jnp.int32, (tk, LANE), 0) // CHUNK + j * (tk // CHUNK)
        lane = lax.broadcasted_iota(jnp.int32, (tk, LANE), 1)
        pats = []
        for ph in range(max(1, tq // LANE)):
            q_chunk = (past + qi * tq) // CHUNK + ((ph * LANE + lane) % tq) // CHUNK
            pats.append(jnp.where(key_chunk <= q_chunk, 0.0, -jnp.inf))
        return pats

    def absorb(bi, j, masked, new=False):
        m_ref, l_ref, a_ref, acc_ref, p_ref = m_all.at[bi], l_all.at[bi], a_all.at[bi], acc_all.at[bi], p_all_ref.at[bi]
        slot = j % 2
        pats = hidden_bias(j) if masked else None
        for b, csl in enumerate(blocks):
            s = s_all[bi, slot, b]
            if masked:
                s = s + pats[b % len(pats)]
            m_prev = m_ref[:, csl]
            m_new = jnp.maximum(m_prev, jnp.max(s, axis=0, keepdims=True))
            alpha = jnp.exp2(m_prev - m_new)
            p_ref[b] = jnp.exp2((s - m_new).astype(BF16))
            m_ref[:, csl] = m_new
            a_ref[:, csl] = alpha
        p_all = jnp.concatenate([p_ref[b] for b in range(nblk)], axis=1)
        _, value_tile = key_value_tile(bi, j, new)
        vt1 = jnp.concatenate([value_tile(), jnp.ones((SUM_ROWS, tk), BF16)], axis=0)
        pv = jnp.dot(vt1, p_all, preferred_element_type=F32)
        l_ref[...] = a_ref[...] * l_ref[...] + pv[KV_RANK:KV_RANK + 1, :]
        for b, csl in enumerate(blocks):
            acc_ref[b] = a_ref[:, csl] * acc_ref[b] + pv[:KV_RANK, csl]

    for bi in range(nb):
        store_scores(bi, 0, scores(bi, 0))

    def body(j, carry):
        s_new = [scores(bi, j) for bi in range(nb)]
        for bi in range(nb):
            absorb(bi, j - 1, masked=False)
        for bi in range(nb):
            store_scores(bi, j % 2, s_new[bi])
        return carry

    if cached:
        n_cache = past // tk
        lax.fori_loop(1, n_cache, body, 0)
        s_new = [scores(bi, n_cache, new=True) for bi in range(nb)]
        for bi in range(nb):
            absorb(bi, n_cache - 1, masked=False)
        for bi in range(nb):
            store_scores(bi, n_cache % 2, s_new[bi])
        for bi in range(nb):
            absorb(bi, n_cache, masked=True, new=True)
    else:
        lax.fori_loop(1, n_vis, body, 0)
        for bi in range(nb):
            absorb(bi, n_vis - 1, masked=True)
    for bi in range(nb):
        o = jnp.concatenate([acc_all[bi, b] * (1.0 / l_all[bi, :, csl]) for b, csl in enumerate(blocks)], axis=1).T
        o_all[bi] = o.astype(BF16).reshape(MLA_HEADS, tq, KV_RANK)


def _attention(qt, kcat, vt, cache, tq, tk):
    b, ng, qslabs, _, slab_w = qt.shape
    rows = qslabs * slab_w
    t = ng * tq
    past = 0 if cache is None else cache[0].shape[1]
    assert tk % tq == 0 and past % tk == 0 and rows == MLA_HEADS * tq and tq % CHUNK == 0
    nb = max(1, ATT_STEP_ROWS // rows)
    nb = nb if b % nb == 0 else 1
    grid = (b // nb, ng)
    nblk = rows // LANE
    whole = lambda a: pl.BlockSpec((nb,) + a.shape[1:], lambda i, j: (i,) + (0,) * (a.ndim - 1))
    if cache is None:
        nkv = kcat.shape[1] // tk
        assert kcat.shape[1] % tk == 0
        operands = [qt, kcat, vt]
    else:
        nkv = past // tk + 1
        assert ng == 1 and t <= tk
        operands = [qt, cache[0], cache[1], kcat]
    return pl.pallas_call(
        functools.partial(_attn_kernel, tq=tq, tk=tk, past=past, nkv=nkv,
                          mode="tiles" if cache is None else "cached"),
        grid=grid,
        in_specs=[pl.BlockSpec((nb, None, qslabs, QK_CAT, rows // qslabs), lambda i, j: (i, j, 0, 0, 0))]
        + [whole(a) for a in operands[1:]],
        out_specs=pl.BlockSpec((nb, MLA_HEADS, tq, KV_RANK), lambda i, j: (i, 0, j, 0)),
        out_shape=jax.ShapeDtypeStruct((b, MLA_HEADS, t, KV_RANK), BF16),
        scratch_shapes=[
            pltpu.VMEM((nb, 1, rows), F32),
            pltpu.VMEM((nb, 1, rows), F32),
            pltpu.VMEM((nb, 1, rows), F32),
            pltpu.VMEM((nb, nblk, KV_RANK, LANE), F32),
            pltpu.VMEM((nb, 2, nblk, tk, LANE), F32),
            pltpu.VMEM((nb, nblk, tk, LANE), BF16),
        ],
        compiler_params=_params("parallel", "parallel"),
        name="mla_attention",
    )(*operands)


def _rope_tables(past, t):
    half = QK_ROPE // 2
    inv = jnp.exp(-math.log(ROPE_THETA) * jnp.arange(half, dtype=F32) / half)
    ang = (past + jnp.arange(t, dtype=jnp.int32)).astype(F32)[:, None] * inv[None, :]
    reps = 2 * LANE // QK_ROPE
    return jnp.tile(jnp.cos(ang), (1, reps)), jnp.tile(jnp.sin(ang), (1, reps))


def _rotate_half_cols(w):
    shp = w.shape
    w4 = w.reshape(shp[:-1] + (shp[-1] // QK_ROPE, 2, QK_ROPE // 2))
    return jnp.concatenate([-w4[..., 1:2, :], w4[..., 0:1, :]], axis=-2).reshape(shp)


def _tile(t, pref):
    return pref if t % pref == 0 else t


def _prep_weights(norm_mix_pre, norm_mix_post, norm_ffn_pre, norm_ffn_post,
                  ssm_w_in, ssm_conv_w, ssm_conv_b, ssm_dt_bias, ssm_a_log, ssm_d, ssm_norm, ssm_w_out,
                  kv_norm_in, kv_w_dkv, kv_norm, kv_w_kr, kv_w_uk, kv_w_uv,
                  mla_w_dq, mla_q_norm, mla_w_uq, mla_w_o,
                  ffn_w_up, ffn_conv_w, ffn_conv_b, ffn_w_down):
    w = {}
    w_in = ssm_w_in[0]
    w["wzx"] = w_in.astype(BF16)
    w["wdt"] = jnp.pad(w_in[:, D_INNER + CONV_DIM:], ((0, 0), (0, LANE - SSM_HEADS))).astype(BF16)
    w_dt_t = w_in[:, D_INNER + CONV_DIM:].T
    w["wdtt"] = jnp.concatenate([w_dt_t[0::2], w_dt_t[1::2]]).astype(BF16)
    w["ssm_cw"] = ssm_conv_w[0]
    w["ssm_cb"] = ssm_conv_b[0][None, :]
    w["dtb"] = ssm_dt_bias[0][None, :]
    w["dtbt"] = jnp.concatenate([ssm_dt_bias[0][0::2], ssm_dt_bias[0][1::2]])[:, None]
    a = -jnp.exp(ssm_a_log[0].astype(F32))
    w["arow"] = a[None, :]
    w["acol2"] = jnp.broadcast_to(a.reshape(SSM_HEADS // 2, 2, 1),
                                  (SSM_HEADS // 2, 2, CHUNK)).reshape(SSM_HEADS // 2, 2 * CHUNK)
    w["expand"] = jnp.repeat(jnp.eye(SSM_HEADS, dtype=BF16), SSM_HEADDIM, axis=1)
    w["dskip"] = jnp.repeat(ssm_d[0], SSM_HEADDIM)[None, :]
    w["gn"] = ssm_norm[0][None, :]
    w["w_out"] = ssm_w_out[0].astype(BF16)
    w["g_mix_pre"] = norm_mix_pre[:, None, :]
    w["g_mix_post"] = norm_mix_post[:, None, :]
    w["g_ffn_pre"] = norm_ffn_pre[:, None, :]
    w["g_ffn_post"] = norm_ffn_post[:, None, :]
    w["ffn_up"] = ffn_w_up.astype(BF16)
    w["ffn_cw"] = ffn_conv_w
    w["ffn_cb"] = ffn_conv_b[:, None, :]
    w["ffn_down"] = ffn_w_down.astype(BF16)
    w["kv_gin"] = kv_norm_in[None, :]
    w["kv_w"] = jnp.concatenate([kv_w_dkv, kv_w_kr, _rotate_half_cols(kv_w_kr)], axis=-1).astype(BF16)
    w["kv_g"] = kv_norm[None, :]
    w["wdq"] = mla_w_dq[0].astype(BF16)
    w["gq"] = mla_q_norm[0][None, :]
    wuq = mla_w_uq[0].reshape(Q_RANK, MLA_HEADS, QK_NOPE + QK_ROPE)
    w["wq_nope"] = wuq[:, :, :QK_NOPE].reshape(Q_RANK, MLA_HEADS * QK_NOPE).astype(BF16)
    wq_pe = wuq[:, :, QK_NOPE:].reshape(Q_RANK, MLA_HEADS * QK_ROPE)
    w["wq_pe_t"] = wq_pe.T.astype(BF16)
    w["wq_pe_rot_t"] = _rotate_half_cols(wq_pe).T.astype(BF16)
    w["wuk"] = jnp.transpose(kv_w_uk, (1, 0, 2)).astype(BF16)
    w["wuv"] = jnp.transpose(kv_w_uv, (1, 0, 2)).astype(BF16)
    w["wo"] = mla_w_o[0].astype(BF16)
    return w


def _trunk(x, ssm_conv_buf, ssm_state, ffn_buf, past_kcat, w):
    b, t, _ = x.shape
    past = 0 if past_kcat is None else past_kcat[0].shape[1]
    assert t % CHUNK == 0 and past % CHUNK == 0
    tm = TOKEN_TILE
    nseq = tm // t if t < tm and tm % t == 0 and b % (tm // t) == 0 else 1
    if t % tm != 0 and nseq == 1:
        tm = t
    bb, tt = b // nseq, nseq * t
    pack = lambda a: a.reshape((bb, tt) + a.shape[2:])
    unpack = lambda a: a.reshape((b, t) + a.shape[2:])
    pack_hist = lambda a: a.reshape((bb, nseq) + a.shape[1:])
    unpack_hist = lambda a: a.reshape((b,) + a.shape[2:])

    z, xbc, dt, dtt2, new_conv = _mamba_in(pack(x), pack_hist(ssm_conv_buf), w["g_mix_pre"][0], w["wzx"], w["wdt"],
                                           w["wdtt"], w["ssm_cw"], w["ssm_cb"], w["dtb"], w["dtbt"], tm)
    nc = _tile(t, SSD_TILE) // CHUNK
    dtt2 = dtt2.reshape(b, t // CHUNK, SSM_HEADS // 2, 2 * CHUNK)
    yg, new_state = _ssd(unpack(xbc), unpack(z), unpack(dt), dtt2, w["arow"], w["acol2"], w["expand"], w["dskip"],
                         w["gn"], None if ssm_state is None else ssm_state.reshape(b, D_INNER, D_STATE), nc)
    new_state = new_state.reshape(b, SSM_HEADS, SSM_HEADDIM, D_STATE)

    ffn_w = lambda i: (w["g_ffn_pre"], w["ffn_up"], w["ffn_cw"], w["ffn_cb"], w["ffn_down"], w["g_ffn_post"])
    cos, sin = _rope_tables(past, t)
    tq = t if nseq > 1 else (ATT_TQ if tm % ATT_TQ == 0 else tm)
    x, new_ffn0, lat, kpe, kcat, q, latt = _layer0_tail(
        pack(yg), pack(x), pack_hist(ffn_buf[0]), jnp.tile(cos, (nseq, 1)), jnp.tile(sin, (nseq, 1)),
        w["w_out"], w["g_mix_post"][0], ffn_w(0), (w["kv_gin"], w["kv_w"], w["kv_g"]),
        (w["g_mix_pre"][1], w["wdq"], w["gq"], w["wq_nope"], w["wq_pe_t"], w["wq_pe_rot_t"], w["wuk"]), tm, tm // tq)
    lat, kpe, kcat = unpack(lat), unpack(kpe), unpack(kcat)
    q = q.reshape((b, t // tq) + q.shape[3:])

    if past_kcat is not None:
        vt = None
    elif tm == ATT_TK:
        vt = latt
    else:
        vt = kcat[:, :, :KV_RANK].reshape(b, t // ATT_TK, ATT_TK, KV_RANK).transpose(0, 1, 3, 2)
    o_lat = _attention(q, kcat, vt, past_kcat, tq, ATT_TK)
    o_lat = o_lat.reshape(bb, nseq, MLA_HEADS, t, KV_RANK)
    x, new_ffn1 = _layer1_tail(o_lat, x, pack_hist(ffn_buf[1]), w["wuv"], w["wo"], w["g_mix_post"][1], ffn_w(1), tm)
    return (unpack(x), new_state[None], unpack_hist(new_conv)[None],
            jnp.stack([unpack_hist(new_ffn0), unpack_hist(new_ffn1)]), lat, kpe)


def kernel(x_prompt, x_sample, state_ssm, state_ssm_conv, state_ffn_conv, cache_kv_latent, cache_k_rope, norm_mix_pre, norm_mix_post, norm_ffn_pre, norm_ffn_post, ssm_w_in, ssm_conv_w, ssm_conv_b, ssm_dt_bias, ssm_a_log, ssm_d, ssm_norm, ssm_w_out, kv_norm_in, kv_w_dkv, kv_norm, kv_w_kr, kv_w_uk, kv_w_uv, mla_w_dq, mla_q_norm, mla_w_uq, mla_w_o, ffn_w_up, ffn_conv_w, ffn_conv_b, ffn_w_down):
    w = _prep_weights(norm_mix_pre, norm_mix_post, norm_ffn_pre, norm_ffn_post,
                      ssm_w_in, ssm_conv_w, ssm_conv_b, ssm_dt_bias, ssm_a_log, ssm_d, ssm_norm, ssm_w_out,
                      kv_norm_in, kv_w_dkv, kv_norm, kv_w_kr, kv_w_uk, kv_w_uv,
                      mla_w_dq, mla_q_norm, mla_w_uq, mla_w_o,
                      ffn_w_up, ffn_conv_w, ffn_conv_b, ffn_w_down)
    bp = x_prompt.shape[0]
    dtp = x_prompt.dtype
    y_p, p_ssm, p_conv, p_ffn, p_lat, p_kpe = _trunk(
        x_prompt,
        jnp.zeros((bp, SSM_CONV - 1, CONV_DIM), dtp),
        None,
        jnp.zeros((2, bp, FFN_CONV - 1, 2 * D_FF), dtp),
        None, w)
    y_s, s_ssm, s_conv, s_ffn, s_lat, s_kpe = _trunk(
        x_sample, state_ssm_conv[0], state_ssm[0], state_ffn_conv,
        (cache_kv_latent, jnp.swapaxes(cache_k_rope, 1, 2)), w)
    return (y_p, y_s, p_ssm, p_conv, p_ffn, p_lat, p_kpe, s_ssm, s_conv, s_ffn, s_lat, s_kpe)
```
